```python
import math
import jax, jax.numpy as jnp
from jax import lax
import numpy as np

D_MODEL = 1024
BATCH = 32
SEQ = 256
DEPTH = 1
DEC_BATCH = 2
DEC_SEQ = 1024
PAST_LEN = 256

GRID_W = 64
D_MIX = D_MODEL
D_RWKV = D_MIX // 2
D_HYENA = D_MIX - D_RWKV
RWKV_HEAD = 64
RWKV_HEADS = D_RWKV // RWKV_HEAD
DECAY_LORA = 64
ICLR_LORA = 64
GATE_LORA = 128
HYENA_ORDER = 2
FILTER_BANDS = 8
FILTER_EMB = 1 + 2 * FILTER_BANDS
FILTER_HIDDEN = 64
DECAY_TARGET = 1e-2
FAST_DECAY_PCT = 0.3
SLOW_DECAY_PCT = 1.5
N_KEYS = 128
N_EXPERTS = N_KEYS * N_KEYS
PEER_HEADS = 8
PEER_TOPK = 16
PEER_DKEY = 256
PEER_HALF = PEER_DKEY // 2
PEER_CHUNK = 128
D_IN = 3 * D_RWKV + 3 * D_HYENA
EPS = 1e-6
GN_EPS = 64e-5

kernel_name = 'hymba_rwkv7_hyena_peer_diffusion_step'


def rmsnorm(x, g):
    xf = x.astype(jnp.float32)
    y = xf * lax.rsqrt(jnp.mean(xf * xf, axis=-1, keepdims=True) + EPS)
    return (y * g.astype(jnp.float32)).astype(x.dtype)


def short_conv(u, w, grid_w):
    B, L, C = u.shape
    if grid_w is not None:
        rows = L // grid_w
        u = u.reshape(B * rows, grid_w, C)
    up = jnp.pad(u, ((0, 0), (1, 1), (0, 0)))
    y = up[:, :-2] * w[0] + up[:, 1:-1] * w[1] + up[:, 2:] * w[2]
    return y.reshape(B, L, C)


def wkv_scan(s0, r, dec, k, v, kk, kka, reverse):
    def step(s, inp):
        r_t, d_t, k_t, v_t, kk_t, kka_t = inp
        sa = jnp.einsum('bhij,bhj->bhi', s, kk_t)
        s = s * d_t[:, :, None, :] - sa[..., None] * kka_t[:, :, None, :] + v_t[..., None] * k_t[:, :, None, :]
        y = jnp.einsum('bhij,bhj->bhi', s, r_t)
        return s, y
    xs = tuple(jnp.moveaxis(a, 1, 0) for a in (r, dec, k, v, kk, kka))
    s_fin, ys = lax.scan(step, s0, xs, reverse=reverse)
    return s_fin, jnp.moveaxis(ys, 0, 1)


def rwkv_mix(h, r, k, v, s0, p, l):
    B, L, _ = h.shape
    def heads(t):
        return t.astype(jnp.float32).reshape(B, L, RWKV_HEADS, RWKV_HEAD)
    g = jax.nn.sigmoid(h @ p['rwkv_g1'][l]) @ p['rwkv_g2'][l]
    kk = heads(k * p['rwkv_k_k'][l])
    kk = kk * lax.rsqrt(jnp.maximum(jnp.sum(kk * kk, -1, keepdims=True), 1e-24))
    rh = heads(r)
    vh = heads(v)
    y_sum = jnp.zeros_like(rh)
    bonus = jnp.zeros_like(rh)
    finals = []
    for d in range(2):
        w = p['rwkv_w0'][l, d] + jnp.tanh(h @ p['rwkv_w1'][l, d]) @ p['rwkv_w2'][l, d]
        dec = jnp.exp(-jnp.exp(-jax.nn.softplus(-w.astype(jnp.float32)) - 0.5))
        a = jax.nn.sigmoid(p['rwkv_a0'][l, d] + (h @ p['rwkv_a1'][l, d]) @ p['rwkv_a2'][l, d])
        kd = heads(k * (1 + (a - 1) * p['rwkv_k_a'][l]))
        s_fin, y = wkv_scan(s0[:, d].astype(jnp.float32), rh, heads(dec), kd, vh, kk, kk * heads(a), reverse=(d == 1))
        y_sum = y_sum + y
        bonus = bonus + jnp.sum(rh * kd * p['rwkv_r_k'][l, d].astype(jnp.float32), -1, keepdims=True) * vh
        finals.append(s_fin)
    mu = jnp.mean(y_sum, -1, keepdims=True)
    var = jnp.mean(jnp.square(y_sum - mu), -1, keepdims=True)
    yn = ((y_sum - mu) * lax.rsqrt(var + GN_EPS)).reshape(B, L, D_RWKV)
    yn = yn * p['rwkv_ln_w'][l] + p['rwkv_ln_b'][l]
    out = (yn + bonus.reshape(B, L, D_RWKV)) * g
    return out.astype(h.dtype), jnp.stack(finals, axis=1)


def hyena_filters(L, p, l):
    t = jnp.linspace(0.0, 1.0, L, dtype=jnp.float32)[:, None]
    wpos = 2.0 * math.pi * jnp.arange(L, dtype=jnp.float32)[:, None] / L
    f = jnp.linspace(1e-4, FILTER_BANDS - 1, FILTER_BANDS, dtype=jnp.float32)[None, :]
    z = jnp.concatenate([t, jnp.cos(f * wpos), -jnp.sin(f * wpos)], axis=-1)
    hid = jnp.sin(p['hy_freq'][l, 0] * (z @ p['hy_f_w1'][l] + p['hy_f_b1'][l]))
    hid = jnp.sin(p['hy_freq'][l, 1] * (hid @ p['hy_f_w2'][l] + p['hy_f_b2'][l]))
    filt = (hid @ p['hy_f_w3'][l]).astype(jnp.float32).reshape(L, 2 * HYENA_ORDER, D_HYENA)
    max_decay = math.log(DECAY_TARGET) / FAST_DECAY_PCT
    min_decay = math.log(DECAY_TARGET) / SLOW_DECAY_PCT
    deltas = jnp.linspace(min_decay, max_decay, D_HYENA, dtype=jnp.float32)
    window = jnp.exp(-t * jnp.abs(deltas))
    return filt * window[:, None, :]


def fft_causal(u, filt):
    L = u.shape[1]
    n = 2 * L
    uf = jnp.fft.rfft(u, n=n, axis=1)
    hf = jnp.fft.rfft(filt, n=n, axis=0)
    return jnp.fft.irfft(uf * hf[None], n=n, axis=1)[:, :L]


def bidir_long_conv(u, h_fwd, h_bwd, bias):
    y = fft_causal(u, h_fwd) + jnp.flip(fft_causal(jnp.flip(u, 1), h_bwd), 1)
    return y + u * bias


def hyena_mix(v, x1, x2, p, l):
    L = v.shape[1]
    filt = hyena_filters(L, p, l)
    bias = p['hy_bias'][l].astype(jnp.float32)
    z = x1.astype(jnp.float32) * bidir_long_conv(v.astype(jnp.float32), filt[:, 0], filt[:, 1], bias[0])
    z = x2.astype(jnp.float32) * bidir_long_conv(z, filt[:, 2], filt[:, 3], bias[1])
    return rmsnorm(z, p['hy_norm'][l]).astype(v.dtype)


def mixer(h, s0, grid_w, p, l):
    u = short_conv(h @ p['w_in'][l], p['conv_w'][l], grid_w)
    cuts = [D_RWKV, 2 * D_RWKV, 3 * D_RWKV, 3 * D_RWKV + D_HYENA, 3 * D_RWKV + 2 * D_HYENA]
    r, k, v, hv, hx1, hx2 = jnp.split(u, cuts, axis=-1)
    ya, s_fin = rwkv_mix(h, r, k, v, s0, p, l)
    yb = hyena_mix(hv, hx1, hx2, p, l)
    return jnp.concatenate([ya, yb], axis=-1) @ p['w_out'][l], s_fin


def peer(h, p, l):
    B, L, D = h.shape
    T = B * L
    x = h.reshape(T, D)
    q = (x @ p['peer_wq'][l]).reshape(T, PEER_HEADS, 2, PEER_HALF)
    s = jnp.einsum('thpd,phkd->thpk', q, p['peer_keys'][l])
    sv, si = lax.top_k(s, PEER_TOPK)
    cand = sv[:, :, 0, :, None] + sv[:, :, 1, None, :]
    cidx = si[:, :, 0, :, None] * N_KEYS + si[:, :, 1, None, :]
    top_s, pos = lax.top_k(cand.reshape(T, PEER_HEADS, PEER_TOPK * PEER_TOPK), PEER_TOPK)
    eidx = jnp.take_along_axis(cidx.reshape(T, PEER_HEADS, PEER_TOPK * PEER_TOPK), pos, axis=-1)
    gate = jax.nn.softmax(top_s.astype(jnp.float32), axis=-1).astype(h.dtype)
    u_tab = p['peer_u'][l]
    v_tab = p['peer_v'][l]
    nc = T // PEER_CHUNK
    def chunk(args):
        xc, ic, gc = args
        act = jax.nn.gelu(jnp.einsum('cd,ced->ce', xc, u_tab[ic]), approximate=False) * gc
        return jnp.einsum('ce,ced->cd', act, v_tab[ic])
    out = lax.map(chunk, (x.reshape(nc, PEER_CHUNK, D),
                          eidx.reshape(nc, PEER_CHUNK, PEER_HEADS * PEER_TOPK),
                          gate.reshape(nc, PEER_CHUNK, PEER_HEADS * PEER_TOPK)))
    return out.reshape(B, L, D)


def layer(x, mod, s0, grid_w, p, l):
    shift1, scale1, gate1, shift2, scale2, gate2 = jnp.split(mod, 6, axis=-1)
    h = rmsnorm(x, p['norm_mix'][l]) * (1 + scale1) + shift1
    m, s_fin = mixer(h, s0, grid_w, p, l)
    x = x + gate1 * m
    h = rmsnorm(x, p['norm_ffn'][l]) * (1 + scale2) + shift2
    x = x + gate2 * peer(h, p, l)
    return x, s_fin


def setup_inputs(seed: int = 0) -> dict:
    key = jax.random.key(seed)
    ks = iter(jax.random.split(key, 64))
    def nrm(shape, scale):
        return scale * jax.random.normal(next(ks), shape, jnp.float32)
    def gain(shape):
        return 1.0 + nrm(shape, 0.05)
    return {
        'x_prompt': nrm((BATCH, SEQ, D_MODEL), 1.0),
        'x_sample': nrm((DEC_BATCH, DEC_SEQ, D_MODEL), 1.0),
        'state_rwkv': nrm((DEC_BATCH, DEPTH, 2, RWKV_HEADS, RWKV_HEAD, RWKV_HEAD), 0.3),
        'c': nrm((DEC_BATCH, D_MODEL), 1.0),
        'c_ctx': nrm((D_MODEL,), 1.0),
        'w_ada': nrm((DEPTH, D_MODEL, 6 * D_MODEL), 0.02),
        'b_ada': nrm((DEPTH, 6 * D_MODEL), 0.02),
        'norm_mix': gain((DEPTH, D_MODEL)),
        'norm_ffn': gain((DEPTH, D_MODEL)),
        'w_in': nrm((DEPTH, D_MODEL, D_IN), D_MODEL ** -0.5),
        'conv_w': nrm((DEPTH, 3, D_IN), 0.2) + jnp.array([0.0, 1.0, 0.0], jnp.float32)[None, :, None],
        'w_out': nrm((DEPTH, D_MIX, D_MODEL), D_MIX ** -0.5),
        'rwkv_w0': jax.random.uniform(next(ks), (DEPTH, 2, D_RWKV), jnp.float32, -6.0, 1.0),
        'rwkv_w1': nrm((DEPTH, 2, D_MODEL, DECAY_LORA), D_MODEL ** -0.5),
        'rwkv_w2': nrm((DEPTH, 2, DECAY_LORA, D_RWKV), 0.1 * DECAY_LORA ** -0.5),
        'rwkv_a0': nrm((DEPTH, 2, D_RWKV), 0.1),
        'rwkv_a1': nrm((DEPTH, 2, D_MODEL, ICLR_LORA), D_MODEL ** -0.5),
        'rwkv_a2': nrm((DEPTH, 2, ICLR_LORA, D_RWKV), 0.3 * ICLR_LORA ** -0.5),
        'rwkv_g1': nrm((DEPTH, D_MODEL, GATE_LORA), D_MODEL ** -0.5),
        'rwkv_g2': nrm((DEPTH, GATE_LORA, D_RWKV), GATE_LORA ** -0.5),
        'rwkv_k_k': 0.85 + nrm((DEPTH, D_RWKV), 0.05),
        'rwkv_k_a': gain((DEPTH, D_RWKV)),
        'rwkv_r_k': nrm((DEPTH, 2, RWKV_HEADS, RWKV_HEAD), 0.1),
        'rwkv_ln_w': gain((DEPTH, D_RWKV)),
        'rwkv_ln_b': nrm((DEPTH, D_RWKV), 0.02),
        'hy_f_w1': nrm((DEPTH, FILTER_EMB, FILTER_HIDDEN), FILTER_EMB ** -0.5),
        'hy_f_b1': nrm((DEPTH, FILTER_HIDDEN), 0.1),
        'hy_f_w2': nrm((DEPTH, FILTER_HIDDEN, FILTER_HIDDEN), FILTER_HIDDEN ** -0.5),
        'hy_f_b2': nrm((DEPTH, FILTER_HIDDEN), 0.1),
        'hy_freq': gain((DEPTH, 2, FILTER_HIDDEN)),
        'hy_f_w3': nrm((DEPTH, FILTER_HIDDEN, 2 * HYENA_ORDER * D_HYENA), 0.1 * FILTER_HIDDEN ** -0.5),
        'hy_bias': nrm((DEPTH, HYENA_ORDER, D_HYENA), 0.1),
        'hy_norm': gain((DEPTH, D_HYENA)),
        'peer_wq': nrm((DEPTH, D_MODEL, PEER_HEADS * PEER_DKEY), D_MODEL ** -0.5),
        'peer_keys': nrm((DEPTH, 2, PEER_HEADS, N_KEYS, PEER_HALF), PEER_HALF ** -0.5),
        'peer_u': nrm((DEPTH, N_EXPERTS, D_MODEL), D_MODEL ** -0.5),
        'peer_v': nrm((DEPTH, N_EXPERTS, D_MODEL), 0.5),
        'final_norm': gain((D_MODEL,)),
    }


def reference(x_prompt, x_sample, state_rwkv, c, c_ctx, w_ada, b_ada, norm_mix, norm_ffn, w_in, conv_w, w_out,
              rwkv_w0, rwkv_w1, rwkv_w2, rwkv_a0, rwkv_a1, rwkv_a2, rwkv_g1, rwkv_g2, rwkv_k_k, rwkv_k_a,
              rwkv_r_k, rwkv_ln_w, rwkv_ln_b, hy_f_w1, hy_f_b1, hy_f_w2, hy_f_b2, hy_freq, hy_f_w3, hy_bias,
              hy_norm, peer_wq, peer_keys, peer_u, peer_v, final_norm):
    p = dict(norm_mix=norm_mix, norm_ffn=norm_ffn, w_in=w_in, conv_w=conv_w, w_out=w_out,
             rwkv_w0=rwkv_w0, rwkv_w1=rwkv_w1, rwkv_w2=rwkv_w2, rwkv_a0=rwkv_a0, rwkv_a1=rwkv_a1,
             rwkv_a2=rwkv_a2, rwkv_g1=rwkv_g1, rwkv_g2=rwkv_g2, rwkv_k_k=rwkv_k_k, rwkv_k_a=rwkv_k_a,
             rwkv_r_k=rwkv_r_k, rwkv_ln_w=rwkv_ln_w, rwkv_ln_b=rwkv_ln_b, hy_f_w1=hy_f_w1, hy_f_b1=hy_f_b1,
             hy_f_w2=hy_f_w2, hy_f_b2=hy_f_b2, hy_freq=hy_freq, hy_f_w3=hy_f_w3, hy_bias=hy_bias,
             hy_norm=hy_norm, peer_wq=peer_wq, peer_keys=peer_keys, peer_u=peer_u, peer_v=peer_v)
    xp = x_prompt
    xs = x_sample
    s_zero = jnp.zeros((xp.shape[0], 2, RWKV_HEADS, RWKV_HEAD, RWKV_HEAD), jnp.float32)
    new_states = []
    for l in range(DEPTH):
        mod_ctx = (jax.nn.silu(c_ctx) @ w_ada[l] + b_ada[l])[None, None, :]
        mod_lat = (jax.nn.silu(c) @ w_ada[l] + b_ada[l])[:, None, :]
        xp, s_ctx = layer(xp, mod_ctx, s_zero, None, p, l)
        xs, _ = layer(xs, mod_lat, state_rwkv[:, l], GRID_W, p, l)
        new_states.append(s_ctx.astype(xp.dtype))
    y_prompt = rmsnorm(xp, final_norm)
    y_sample = rmsnorm(xs, final_norm)
    new_state_rwkv = jnp.stack(new_states, axis=1)
    return (y_prompt, y_sample, new_state_rwkv)
```

```python
import functools
import math

import jax
import jax.numpy as jnp
from jax import lax
from jax.experimental import pallas as pl
from jax.experimental.pallas import tpu as pltpu

F32 = jnp.float32
BF16 = jnp.bfloat16

GRID_W = 64
HEAD = 64
TOPK = 16
EPS = 1e-6
GN_EPS = 64e-5
LANES = 128
SUBLANES = 8
VMEM_LIMIT = 56 * 1024 * 1024


def _params(*sem):
    return pltpu.CompilerParams(dimension_semantics=sem, vmem_limit_bytes=VMEM_LIMIT)


def _split(x):
    hi = x.astype(BF16)
    lo = (x - hi.astype(F32)).astype(BF16)
    return hi, lo


def _dot(a, b):
    return jnp.dot(a, b, preferred_element_type=F32)


def _dot_split(ah, al, bh, bl):
    return _dot(ah, bh) + (_dot(al, bh) + _dot(ah, bl))


def _dot3(a, b):
    ah, al = _split(a)
    bh, bl = _split(b)
    return _dot_split(ah, al, bh, bl)


def _full(shape):
    n = len(shape)
    return pl.BlockSpec(shape, lambda *_: (0,) * n)


def _const(shape):
    n = len(shape)
    return pl.BlockSpec(shape, lambda *_: (0,) * n, pipeline_mode=pl.Buffered(1))


def _ada_kernel(c_ref, w_ref, b_ref, o_ref):
    c = c_ref[...]
    s = c * jax.nn.sigmoid(c)
    o_ref[...] = _dot3(s, w_ref[...]) + b_ref[...]


def _ada(c_rows, w, b):
    rows, d = c_rows.shape
    n = w.shape[1]
    bn = n // 4
    return pl.pallas_call(
        _ada_kernel,
        grid=(4,),
        in_specs=[_full((rows, d)),
                  pl.BlockSpec((d, bn), lambda j: (0, j)),
                  pl.BlockSpec((1, bn), lambda j: (0, j))],
        out_specs=pl.BlockSpec((rows, bn), lambda j: (0, j)),
        out_shape=jax.ShapeDtypeStruct((rows, n), F32),
        compiler_params=_params("arbitrary"),
        name="ada",
    )(c_rows, w, b.reshape(1, n))


def _headsum(z, hs):
    zh, zl = _split(z)
    return _dot(zh, hs) + _dot(zl, hs)


def _mix_pre_kernel(n_ctx_tiles, x_ref, mod_ref, nm_ref, win_ref, cw_ref, l1_ref, w2_ref, a2_ref,
                    g2_ref, vec_ref, hs_ref, sh_ref, dr_ref, gb_ref, hy_ref):
    tm, d = x_ref.shape
    dr = hs_ref.shape[0]
    x = x_ref[...]
    mod = mod_ref[0]
    shift1 = mod[:, 0:d]
    scale1 = mod[:, d:2 * d]
    xn = x * lax.rsqrt(jnp.mean(x * x, axis=-1, keepdims=True) + EPS) * nm_ref[...]
    hb = (xn * (1.0 + scale1) + shift1).astype(BF16)

    pos = lax.broadcasted_iota(jnp.int32, (tm, dr), 0)
    is_ctx = pl.program_id(0) < n_ctx_tiles
    row = pos % GRID_W
    keep_prev = jnp.where(is_ctx, jnp.where(pos == 0, 0.0, 1.0), jnp.where(row == 0, 0.0, 1.0))
    keep_next = jnp.where(is_ctx, jnp.where(pos == tm - 1, 0.0, 1.0),
                          jnp.where(row == GRID_W - 1, 0.0, 1.0))

    def proj(c):
        u = _dot(hb, win_ref[:, c * dr:(c + 1) * dr])
        cw = cw_ref[:, c * dr:(c + 1) * dr]
        up = pltpu.roll(u, 1, 0) * keep_prev
        un = pltpu.roll(u, tm - 1, 0) * keep_next
        return up * cw[0:1] + u * cw[1:2] + un * cw[2:3]

    r = proj(0)
    k = proj(1)
    v = proj(2)
    hy_ref[0] = proj(3)
    hy_ref[1] = proj(4)
    hy_ref[2] = proj(5)

    lo = _dot(hb, l1_ref[...])
    vec = vec_ref[...]
    k_k = vec[4:5]
    k_a = vec[5:6]
    hs = hs_ref[...]
    kkr = k * k_k
    kk = kkr * lax.rsqrt(jnp.maximum(_headsum(kkr * kkr, hs), 1e-24))
    sh_ref[0] = r
    sh_ref[1] = kk
    sh_ref[2] = v
    gb_ref[0] = _dot3(jax.nn.sigmoid(lo[:, 4 * LANES:5 * LANES]), g2_ref[...])
    bon = jnp.zeros_like(r)
    for dd in range(2):
        w = vec[dd:dd + 1] + _dot3(jnp.tanh(lo[:, dd * LANES:(dd + 1) * LANES]), w2_ref[dd])
        z = -w
        softplus = jnp.maximum(z, 0.0) + jnp.log1p(jnp.exp(-jnp.abs(z)))
        dr_ref[dd, 0] = jnp.exp(-jnp.exp(-softplus - 0.5))
        a = jax.nn.sigmoid(vec[2 + dd:3 + dd]
                           + _dot3(lo[:, (2 + dd) * LANES:(3 + dd) * LANES], a2_ref[dd]))
        kd = k * (1.0 + (a - 1.0) * k_a)
        dr_ref[dd, 1] = kd
        dr_ref[dd, 2] = kk * a
        bon = bon + r * kd * vec[6 + dd:7 + dd]
    gb_ref[1] = _headsum(bon, hs) * v


def _mix_pre(x, mods, n_ctx_tiles, tiles_per_lat, tm, norm_mix, w_in, conv_w, lora1, w2, a2, g2, vecs, hs):
    t, d = x.shape
    dr = hs.shape[0]
    n_tiles = t // tm

    def mod_idx(i):
        return (jnp.where(i < n_ctx_tiles, 0, 1 + (i - n_ctx_tiles) // tiles_per_lat), 0, 0)

    return pl.pallas_call(
        functools.partial(_mix_pre_kernel, n_ctx_tiles),
        grid=(n_tiles,),
        in_specs=[pl.BlockSpec((tm, d), lambda i: (i, 0)),
                  pl.BlockSpec((1, 1, mods.shape[-1]), mod_idx),
                  _full(norm_mix.shape), _full(w_in.shape), _full(conv_w.shape), _full(lora1.shape),
                  _full(w2.shape), _full(a2.shape), _full(g2.shape), _full(vecs.shape), _full(hs.shape)],
        out_specs=[pl.BlockSpec((3, tm, dr), lambda i: (0, i, 0)),
                   pl.BlockSpec((2, 3, tm, dr), lambda i: (0, 0, i, 0)),
                   pl.BlockSpec((2, tm, dr), lambda i: (0, i, 0)),
                   pl.BlockSpec((3, tm, dr), lambda i: (0, i, 0))],
        out_shape=[jax.ShapeDtypeStruct((3, t, dr), F32),
                   jax.ShapeDtypeStruct((2, 3, t, dr), F32),
                   jax.ShapeDtypeStruct((2, t, dr), F32),
                   jax.ShapeDtypeStruct((3, t, dr), F32)],
        compiler_params=_params("arbitrary"),
        name="mix_pre",
    )(x, mods, norm_mix, w_in, conv_w, lora1, w2, a2, g2, vecs, hs)


def _wkv_kernel(n_fwd, r_ref, kk_ref, v_ref, d_ref, k_ref, kka_ref, s0_ref, y_ref, sfin_ref, s_scr):
    tb = r_ref.shape[1]
    ni = s_scr.shape[0]
    rev = pl.program_id(0) >= n_fwd
    tblk = pl.program_id(1)

    @pl.when(tblk == 0)
    def _():
        s_scr[...] = s0_ref[0]

    def step(tau, carry):
        t = jnp.where(rev, tb - 1 - tau, tau)
        rt = r_ref[0, t]
        kkt = kk_ref[0, t]
        dt = d_ref[0, t]
        kt = k_ref[0, t]
        kkat = kka_ref[0, t]
        drt = dt * rt
        c1 = jnp.sum(kt * rt, axis=0, keepdims=True)
        c2 = jnp.sum(kkat * rt, axis=0, keepdims=True)

        def rows8(i8, c):
            base = pl.multiple_of(i8 * SUBLANES, SUBLANES)
            vblk = v_ref[0, t, pl.ds(base, SUBLANES), :]
            outs = []
            for jj in range(SUBLANES):
                s = s_scr[base + jj]
                vi = vblk[jj:jj + 1]
                sa = jnp.sum(s * kkt, axis=0, keepdims=True)
                q = jnp.sum(s * drt, axis=0, keepdims=True)
                s_scr[base + jj] = s * dt + (vi * kt - sa * kkat)
                outs.append(q + vi * c1 - sa * c2)
            y_ref[0, t, pl.ds(base, SUBLANES), :] = jnp.concatenate(outs, axis=0)
            return c

        lax.fori_loop(0, ni // SUBLANES, rows8, 0)
        return carry

    lax.fori_loop(0, tb, step, 0)

    @pl.when(tblk == pl.num_programs(1) - 1)
    def _():
        sfin_ref[0] = s_scr[...]


def _wkv(r, kk, v, d, k, kka, s0, tb):
    gh, t, _, _ = r.shape
    g = 2 * gh
    ni = v.shape[2]
    nt = t // tb

    def tsel(gi, ti):
        return jnp.where(gi >= gh, nt - 1 - ti, ti)

    shared = pl.BlockSpec((1, tb, HEAD, LANES), lambda gi, ti: (gi % gh, tsel(gi, ti), 0, 0))
    perdir = pl.BlockSpec((1, tb, HEAD, LANES), lambda gi, ti: (gi, tsel(gi, ti), 0, 0))
    state = pl.BlockSpec((1, ni, HEAD, LANES), lambda gi, ti: (gi, 0, 0, 0))
    return pl.pallas_call(
        functools.partial(_wkv_kernel, gh),
        grid=(g, nt),
        in_specs=[shared, shared,
                  pl.BlockSpec((1, tb, ni, LANES), lambda gi, ti: (gi % gh, tsel(gi, ti), 0, 0)),
                  perdir, perdir, perdir, state],
        out_specs=[pl.BlockSpec((1, tb, ni, LANES), lambda gi, ti: (gi, tsel(gi, ti), 0, 0)), state],
        out_shape=[jax.ShapeDtypeStruct((g, t, ni, LANES), F32),
                   jax.ShapeDtypeStruct((g, ni, HEAD, LANES), F32)],
        scratch_shapes=[pltpu.VMEM((ni, HEAD, LANES), F32)],
        compiler_params=_params("arbitrary", "arbitrary"),
        name="wkv",
    )(r, kk, v, d, k, kka, s0)


def _dft_mats(l):
    n = 2 * l
    nfp = l + SUBLANES
    kf = jnp.arange(nfp, dtype=jnp.int32)
    tt = jnp.arange(l, dtype=jnp.int32)
    ang = (2.0 * math.pi / n) * ((kf[:, None] * tt[None, :]) % n).astype(F32)
    valid = (kf <= l)[:, None]
    c = jnp.where(valid, jnp.cos(ang), 0.0)
    s = jnp.where(valid, jnp.sin(ang), 0.0)
    wk = jnp.where((kf == 0) | (kf == l), 1.0 / n, 2.0 / n)[None, :]
    ci = c.T * wk
    si = -(s.T) * wk
    return [m for mat in (c, s, ci, si) for m in _split(mat)]


def _hy_filter_kernel(zf_ref, w1_ref, b1_ref, w2_ref, b2_ref, fq_ref, w3_ref, dl_ref, bias_ref,
                      ch_ref, cl_ref, sh_ref, sl_ref, g_ref):
    zf = zf_ref[...]
    hid = jnp.sin(fq_ref[0:1] * (_dot3(zf, w1_ref[...]) + b1_ref[...]))
    hid = jnp.sin(fq_ref[1:2] * (_dot3(hid, w2_ref[...]) + b2_ref[...]))
    win = jnp.exp(-zf[:, 0:1] * jnp.abs(dl_ref[...]))
    hh, hl = _split(hid)
    filt = []
    for o in range(4):
        wh, wl = _split(w3_ref[o])
        filt.append(_dot_split(hh, hl, wh, wl) * win)
    for od in range(2):
        hf, hbk = filt[2 * od], filt[2 * od + 1]
        ph, plo = _split(hf + hbk)
        mh, ml = _split(hbk - hf)
        g_ref[od, 0] = _dot_split(ch_ref[...], cl_ref[...], ph, plo) + bias_ref[od:od + 1]
        g_ref[od, 1] = _dot_split(sh_ref[...], sl_ref[...], mh, ml)


def _hy_filter(zf, w1, b1, w2, b2, fq, w3, deltas, bias, mats):
    l = zf.shape[0]
    nfp = mats[0].shape[0]
    c = w3.shape[-1]
    cb = LANES
    return pl.pallas_call(
        _hy_filter_kernel,
        grid=(c // cb,),
        in_specs=[_full(zf.shape), _full(w1.shape), _full(b1.shape), _full(w2.shape), _full(b2.shape),
                  _full(fq.shape),
                  pl.BlockSpec((4, w3.shape[1], cb), lambda j: (0, 0, j)),
                  pl.BlockSpec((1, cb), lambda j: (0, j)),
                  pl.BlockSpec((2, cb), lambda j: (0, j)),
                  _const((nfp, l)), _const((nfp, l)), _const((nfp, l)), _const((nfp, l))],
        out_specs=pl.BlockSpec((2, 2, nfp, cb), lambda j: (0, 0, 0, j)),
        out_shape=jax.ShapeDtypeStruct((2, 2, nfp, c), F32),
        compiler_params=_params("arbitrary"),
        name="hy_filt",
    )(zf, w1, b1, w2, b2, fq, w3, deltas, bias, *mats[:4])


def _hy_conv_kernel(hy_ref, g_ref, ch_ref, cl_ref, sh_ref, sl_ref, cih_ref, cil_ref, sih_ref, sil_ref, o_ref):
    def conv(u, od):
        uh, ul = _split(u)
        re = _dot_split(ch_ref[...], cl_ref[...], uh, ul)
        im = -_dot_split(sh_ref[...], sl_ref[...], uh, ul)
        gre = g_ref[od, 0]
        gim = g_ref[od, 1]
        ah, al = _split(re * gre - im * gim)
        bh, bl = _split(re * gim + im * gre)
        return (_dot_split(cih_ref[...], cil_ref[...], ah, al)
                + _dot_split(sih_ref[...], sil_ref[...], bh, bl))

    z = hy_ref[1] * conv(hy_ref[0], 0)
    o_ref[...] = hy_ref[2] * conv(z, 1)


def _hy_conv(hy, g, mats, l, first_seq, n_seq):
    t, c = hy.shape[1], hy.shape[2]
    nfp = mats[0].shape[0]
    cb = 2 * LANES
    return pl.pallas_call(
        _hy_conv_kernel,
        grid=(n_seq, c // cb),
        in_specs=[pl.BlockSpec((3, l, cb), lambda b, j: (0, first_seq + b, j)),
                  pl.BlockSpec((2, 2, nfp, cb), lambda b, j: (0, 0, 0, j)),
                  _const((nfp, l)), _const((nfp, l)), _const((nfp, l)), _const((nfp, l)),
                  _const((l, nfp)), _const((l, nfp)), _const((l, nfp)), _const((l, nfp))],
        out_specs=pl.BlockSpec((l, cb), lambda b, j: (b, j)),
        out_shape=jax.ShapeDtypeStruct((n_seq * l, c), F32),
        compiler_params=_params("arbitrary", "arbitrary"),
        name="hy_conv",
    )(hy, g, *mats)


def _mix_post_kernel(y_ref, gb_ref, z_ref, x_ref, mod_ref, vec_ref, wout_ref, nf_ref, hs_ref, x1_ref, h2_ref):
    d = x_ref.shape[1]
    dr = hs_ref.shape[0]
    hs = hs_ref[...]
    vec = vec_ref[...]
    mod = mod_ref[0]
    gate1 = mod[:, 2 * d:3 * d]
    shift2 = mod[:, 3 * d:4 * d]
    scale2 = mod[:, 4 * d:5 * d]
    y = y_ref[...]
    mu = _headsum(y, hs) * (1.0 / HEAD)
    yc = y - mu
    var = _headsum(yc * yc, hs) * (1.0 / HEAD)
    yn = yc * lax.rsqrt(var + GN_EPS) * vec[0:1] + vec[1:2]
    ya = (yn + gb_ref[1]) * gb_ref[0]
    z = z_ref[...]
    yb = z * lax.rsqrt(jnp.mean(z * z, axis=-1, keepdims=True) + EPS) * vec[2:3]
    m = _dot(ya.astype(BF16), wout_ref[0:dr]) + _dot(yb.astype(BF16), wout_ref[dr:])
    x1 = x_ref[...] + gate1 * m
    x1_ref[...] = x1
    xn = x1 * lax.rsqrt(jnp.mean(x1 * x1, axis=-1, keepdims=True) + EPS) * nf_ref[...]
    h2_ref[...] = xn * (1.0 + scale2) + shift2


def _mix_post(ysum, gb, z, x, mods, n_ctx_tiles, tiles_per_lat, tm, vecs, w_out, norm_ffn, hs):
    t, d = x.shape
    dr = hs.shape[0]

    def mod_idx(i):
        return (jnp.where(i < n_ctx_tiles, 0, 1 + (i - n_ctx_tiles) // tiles_per_lat), 0, 0)

    tok = pl.BlockSpec((tm, dr), lambda i: (i, 0))
    tokd = pl.BlockSpec((tm, d), lambda i: (i, 0))
    return pl.pallas_call(
        _mix_post_kernel,
        grid=(t // tm,),
        in_specs=[tok, pl.BlockSpec((2, tm, dr), lambda i: (0, i, 0)), tok, tokd,
                  pl.BlockSpec((1, 1, mods.shape[-1]), mod_idx),
                  _full(vecs.shape), _full(w_out.shape), _full(norm_ffn.shape), _full(hs.shape)],
        out_specs=[tokd, tokd],
        out_shape=[jax.ShapeDtypeStruct((t, d), F32), jax.ShapeDtypeStruct((t, d), F32)],
        compiler_params=_params("arbitrary"),
        name="mix_post",
    )(ysum, gb, z, x, mods, vecs, w_out, norm_ffn, hs)


def _second_level_pairs():
    return [(a, b) for a in range(TOPK) for b in range(TOPK) if (a + 1) * (b + 1) <= TOPK]


def _route_kernel(h2_ref, wq_ref, keys_ref, r2_ref, b1_ref, e1_ref, e2_ref, sv_scr, s_scr, rk_scr):
    nh = keys_ref.shape[1]
    nk = keys_ref.shape[2]
    half = keys_ref.shape[3]
    tt = h2_ref.shape[0]
    q = _dot(h2_ref[...].astype(BF16), wq_ref[...])
    key_id = lax.broadcasted_iota(jnp.int32, (nk, tt), 0).astype(F32)
    neg = jnp.float32(-jnp.inf)

    for h in range(nh):
        for p in range(2):
            c0 = (h * 2 + p) * half
            qh, ql = _split(q[:, c0:c0 + half])
            kh, kl = _split(keys_ref[p, h])
            dims = (((1,), (1,)), ((), ()))
            s = (lax.dot_general(kh, qh, dims, preferred_element_type=F32)
                 + (lax.dot_general(kl, qh, dims, preferred_element_type=F32)
                    + lax.dot_general(kh, ql, dims, preferred_element_type=F32)))
            s_scr[p, h] = s

            def extract(r, carry, p=p, h=h):
                x, rank = carry
                m = jnp.max(x, axis=0, keepdims=True)
                first = jnp.min(jnp.where(x == m, key_id, float(nk)), axis=0, keepdims=True)
                sel = key_id == first
                sv_scr[p, r, h:h + 1, :] = m
                return jnp.where(sel, neg, x), jnp.where(sel, r.astype(F32), rank)

            _, rank = lax.fori_loop(0, TOPK, extract, (s, jnp.full((nk, tt), float(TOPK), F32)))
            rk_scr[p, h] = rank

    v1 = [sv_scr[0, a] for a in range(TOPK)]
    v2 = [sv_scr[1, b] for b in range(TOPK)]
    pairs = _second_level_pairs()
    top = v1[0] + v2[0]

    def select(_, carry):
        cand, cnt, zsum = list(carry[0]), list(carry[1]), carry[2]
        m = cand[0]
        for cv in cand[1:]:
            m = jnp.maximum(m, cv)
        zsum = zsum + jnp.exp(m - top)
        found = jnp.zeros((nh, tt), F32)
        for ci, (a, _b) in enumerate(pairs):
            eq = jnp.where(cand[ci] == m, 1.0, 0.0)
            hit = eq * (1.0 - found)
            found = jnp.maximum(found, eq)
            cand[ci] = jnp.where(hit > 0.0, neg, cand[ci])
            cnt[a] = cnt[a] + hit
        return tuple(cand), tuple(cnt), zsum

    zero = jnp.zeros((nh, tt), F32)
    _, cnt, zsum = lax.fori_loop(0, TOPK, select,
                                 (tuple(v1[a] + v2[b] for a, b in pairs), (zero,) * TOPK, zero))
    zinv = 1.0 / zsum

    for h in range(nh):
        rank1 = rk_scr[0, h]
        b1 = jnp.zeros((nk, tt), F32)
        for a in range(TOPK):
            b1 = b1 + jnp.where(rank1 == float(a), cnt[a][h:h + 1], 0.0)
        b1_ref[h] = b1
        r2_ref[h] = rk_scr[1, h]
        e1_ref[h] = jnp.exp(s_scr[0, h] - v1[0][h:h + 1]) * zinv[h:h + 1]
        e2_ref[h] = jnp.exp(s_scr[1, h] - v2[0][h:h + 1])


def _route(h2, wq, keys, tt):
    t, d = h2.shape
    _, nh, nk, _ = keys.shape
    out = jax.ShapeDtypeStruct((nh, nk, t), F32)
    ospec = pl.BlockSpec((nh, nk, tt), lambda i: (0, 0, i))
    return pl.pallas_call(
        _route_kernel,
        grid=(t // tt,),
        in_specs=[pl.BlockSpec((tt, d), lambda i: (i, 0)), _full(wq.shape), _full(keys.shape)],
        out_specs=[ospec, ospec, ospec, ospec],
        out_shape=[out, out, out, out],
        scratch_shapes=[pltpu.VMEM((2, TOPK, nh, tt), F32),
                        pltpu.VMEM((2, nh, nk, tt), F32),
                        pltpu.VMEM((2, nh, nk, tt), F32)],
        compiler_params=_params("arbitrary"),
        name="route",
    )(h2, wq, keys)


def _peer_kernel(h2t_ref, u_ref, vt_ref, r2_ref, e2_ref, b1_ref, e1_ref, x1_ref, mod_ref, fn_ref,
                 o_ref, acc_scr, a_scr):
    d = x1_ref.shape[1]
    nh, nk, tt = r2_ref.shape
    n1 = b1_ref.shape[1]
    eb = pl.program_id(1)

    @pl.when(eb == 0)
    def _():
        acc_scr[...] = jnp.zeros_like(acc_scr)

    hid = _dot(u_ref[...], h2t_ref[...])
    for i in range(n1):
        w = jnp.zeros((nk, tt), F32)
        for h in range(nh):
            gate = e1_ref[h, i:i + 1, :] * e2_ref[h]
            w = w + jnp.where(r2_ref[h] < b1_ref[h, i:i + 1, :], gate, 0.0)
        hi = hid[i * nk:(i + 1) * nk]
        act = 0.5 * hi * (1.0 + lax.erf(hi * (1.0 / math.sqrt(2.0))))
        a_scr[i * nk:(i + 1) * nk, :] = (act * w).astype(BF16)
    acc_scr[...] += _dot(vt_ref[...], a_scr[...])

    @pl.when(eb == pl.num_programs(1) - 1)
    def _():
        gate2 = mod_ref[0][:, 5 * d:6 * d]
        xo = x1_ref[...] + gate2 * acc_scr[...].T
        o_ref[...] = xo * lax.rsqrt(jnp.mean(xo * xo, axis=-1, keepdims=True) + EPS) * fn_ref[...]


def _peer(h2t, u, vt, r2, e2, b1, e1, x1, mods, final_norm, n_ctx_tiles, tiles_per_lat, tt, n1):
    d, t = h2t.shape
    ne = u.shape[0]
    nh, nk, _ = r2.shape
    eb = n1 * nk

    def mod_idx(i, e):
        return (jnp.where(i < n_ctx_tiles, 0, 1 + (i - n_ctx_tiles) // tiles_per_lat), 0, 0)

    dense = pl.BlockSpec((nh, nk, tt), lambda i, e: (0, 0, i))
    rows = pl.BlockSpec((nh, n1, tt), lambda i, e: (0, e, i))
    return pl.pallas_call(
        _peer_kernel,
        grid=(t // tt, ne // eb),
        in_specs=[pl.BlockSpec((d, tt), lambda i, e: (0, i)),
                  pl.BlockSpec((eb, d), lambda i, e: (e, 0)),
                  pl.BlockSpec((d, eb), lambda i, e: (0, e)),
                  dense, dense, rows, rows,
                  pl.BlockSpec((tt, d), lambda i, e: (i, 0)),
                  pl.BlockSpec((1, 1, mods.shape[-1]), mod_idx),
                  _full(final_norm.shape)],
        out_specs=pl.BlockSpec((tt, d), lambda i, e: (i, 0)),
        out_shape=jax.ShapeDtypeStruct((t, d), F32),
        scratch_shapes=[pltpu.VMEM((d, tt), F32), pltpu.VMEM((eb, tt), BF16)],
        compiler_params=_params("arbitrary", "arbitrary"),
        name="peer",
    )(h2t, u, vt, r2, e2, b1, e1, x1, mods, final_norm)


def _hyena_features(l, bands):
    t = jnp.linspace(0.0, 1.0, l, dtype=F32)[:, None]
    wpos = 2.0 * math.pi * jnp.arange(l, dtype=F32)[:, None] / l
    f = jnp.linspace(1e-4, bands - 1, bands, dtype=F32)[None, :]
    return jnp.concatenate([t, jnp.cos(f * wpos), -jnp.sin(f * wpos)], axis=-1)


def kernel(x_prompt, x_sample, state_rwkv, c, c_ctx, w_ada, b_ada, norm_mix, norm_ffn, w_in, conv_w, w_out, rwkv_w0, rwkv_w1, rwkv_w2, rwkv_a0, rwkv_a1, rwkv_a2, rwkv_g1, rwkv_g2, rwkv_k_k, rwkv_k_a, rwkv_r_k, rwkv_ln_w, rwkv_ln_b, hy_f_w1, hy_f_b1, hy_f_w2, hy_f_b2, hy_freq, hy_f_w3, hy_bias, hy_norm, peer_wq, peer_keys, peer_u, peer_v, final_norm):
    bc, lc, d = x_prompt.shape
    bl, ll, _ = x_sample.shape
    depth = w_ada.shape[0]
    assert depth == 1
    dr = rwkv_w0.shape[-1]
    dh = hy_norm.shape[-1]
    nh = dr // HEAD
    assert dr == dh and w_in.shape[-1] == 3 * dr + 3 * dh
    tc, tl = bc * lc, bl * ll
    t = tc + tl
    tm = lc
    assert ll % tm == 0 and tm % GRID_W == 0 and tc % ll == 0
    n_ctx_tiles = tc // tm
    tiles_per_lat = ll // tm
    ghc = (bc * nh) // LANES
    assert ghc * LANES == bc * nh
    parts = LANES // (bl * nh)
    assert parts * bl * nh == LANES and HEAD % (parts * SUBLANES) == 0
    nil = HEAD // parts
    l = 0

    rows = jnp.concatenate([c_ctx[None, :], c, jnp.zeros((SUBLANES - 1 - bl, d), F32)], axis=0)
    mods = _ada(rows, w_ada[l], b_ada[l]).reshape(SUBLANES, 1, 6 * d)

    x = jnp.concatenate([x_prompt.reshape(tc, d), x_sample.reshape(tl, d)], axis=0)

    def pad_cols(w):
        return jnp.pad(w, ((0, 0), (0, LANES - w.shape[1])))

    def pad_rows(w):
        return jnp.pad(w, ((0, LANES - w.shape[0]), (0, 0)))

    lora1 = jnp.concatenate([pad_cols(rwkv_w1[l, 0]), pad_cols(rwkv_w1[l, 1]), pad_cols(rwkv_a1[l, 0]),
                             pad_cols(rwkv_a1[l, 1]), pad_cols(rwkv_g1[l])], axis=1).astype(BF16)
    w2 = jnp.stack([pad_rows(rwkv_w2[l, 0]), pad_rows(rwkv_w2[l, 1])])
    a2 = jnp.stack([pad_rows(rwkv_a2[l, 0]), pad_rows(rwkv_a2[l, 1])])
    vecs_pre = jnp.stack([rwkv_w0[l, 0], rwkv_w0[l, 1], rwkv_a0[l, 0], rwkv_a0[l, 1], rwkv_k_k[l], rwkv_k_a[l],
                          rwkv_r_k[l, 0].reshape(dr), rwkv_r_k[l, 1].reshape(dr)])
    head_id = jnp.arange(dr, dtype=jnp.int32) // HEAD
    hs = (head_id[:, None] == head_id[None, :]).astype(BF16)
    shared, perdir, gb, hy = _mix_pre(x, mods, n_ctx_tiles, tiles_per_lat, tm, norm_mix[l][None, :],
                                      w_in[l].astype(BF16), conv_w[l], lora1, w2, a2, rwkv_g2[l], vecs_pre, hs)

    def ctx_lanes(a):
        a = a.reshape(ghc, bc // ghc, lc, nh, HEAD)
        return a.transpose(0, 2, 4, 1, 3).reshape(ghc, lc, HEAD, LANES)

    def lat_lanes(a):
        a = a.reshape(bl, ll, nh, HEAD).transpose(1, 3, 0, 2)
        return jnp.broadcast_to(a[..., None], (ll, HEAD, bl, nh, parts)).reshape(1, ll, HEAD, LANES)

    def lat_rows(a):
        a = a.reshape(bl, ll, nh, parts, nil).transpose(1, 4, 0, 2, 3)
        return a.reshape(1, ll, nil, LANES)

    sh_c, sh_l = shared[:, :tc], shared[:, tc:]
    pd_c, pd_l = perdir[:, :, :tc], perdir[:, :, tc:]
    ctx_dir = [jnp.concatenate([ctx_lanes(pd_c[0, j]), ctx_lanes(pd_c[1, j])], axis=0) for j in range(3)]
    y_ctx, s_ctx = _wkv(ctx_lanes(sh_c[0]), ctx_lanes(sh_c[1]), ctx_lanes(sh_c[2]), *ctx_dir,
                        jnp.zeros((2 * ghc, HEAD, HEAD, LANES), F32), tb=32)
    lat_dir = [jnp.concatenate([lat_lanes(pd_l[0, j]), lat_lanes(pd_l[1, j])], axis=0) for j in range(3)]
    s0 = state_rwkv[:, l].reshape(bl, 2, nh, parts, nil, HEAD).transpose(1, 4, 5, 0, 2, 3)
    y_lat, _ = _wkv(lat_lanes(sh_l[0]), lat_lanes(sh_l[1]), lat_rows(sh_l[2]), *lat_dir,
                    s0.reshape(2, nil, HEAD, LANES), tb=32)
    ys_ctx = (y_ctx[:ghc] + y_ctx[ghc:]).reshape(ghc, lc, HEAD, bc // ghc, nh)
    ys_ctx = ys_ctx.transpose(0, 3, 1, 4, 2).reshape(tc, dr)
    ys_lat = (y_lat[0] + y_lat[1]).reshape(ll, nil, bl, nh, parts).transpose(2, 0, 3, 4, 1).reshape(tl, dr)
    ysum = jnp.concatenate([ys_ctx, ys_lat], axis=0)
    new_state = s_ctx.reshape(2, ghc, HEAD, HEAD, bc // ghc, nh).transpose(1, 4, 0, 5, 2, 3)
    new_state = new_state.reshape(bc, 1, 2, nh, HEAD, HEAD)

    bands = (hy_f_w1.shape[1] - 1) // 2
    femb = hy_f_w1.shape[1]
    fpad = 32
    w1p = jnp.pad(hy_f_w1[l], ((0, fpad - femb), (0, 0)))
    hidden = hy_f_w2.shape[-1]
    w3 = hy_f_w3[l].reshape(hidden, 4, dh).transpose(1, 0, 2)
    max_decay = math.log(1e-2) / 0.3
    min_decay = math.log(1e-2) / 1.5
    deltas = jnp.linspace(min_decay, max_decay, dh, dtype=F32)[None, :]
    z_parts = []
    for seq_len, first_seq, n_seq in ((lc, 0, bc), (ll, tc // ll, bl)):
        mats = _dft_mats(seq_len)
        zf = jnp.pad(_hyena_features(seq_len, bands), ((0, 0), (0, fpad - femb)))
        g = _hy_filter(zf, w1p, hy_f_b1[l][None, :], hy_f_w2[l], hy_f_b2[l][None, :], hy_freq[l], w3,
                       deltas, hy_bias[l], mats)
        z_parts.append(_hy_conv(hy, g, mats, seq_len, first_seq, n_seq))
    zhy = jnp.concatenate(z_parts, axis=0)

    vecs_post = jnp.concatenate([jnp.stack([rwkv_ln_w[l], rwkv_ln_b[l], hy_norm[l]]),
                                 jnp.zeros((SUBLANES - 3, dr), F32)], axis=0)
    x1, h2 = _mix_post(ysum, gb, zhy, x, mods, n_ctx_tiles, tiles_per_lat, tm, vecs_post,
                       w_out[l].astype(BF16), norm_ffn[l][None, :], hs)

    r2, b1, e1, e2 = _route(h2, peer_wq[l].astype(BF16), peer_keys[l], tt=256)
    tt = 512
    assert ll % tt == 0 and tc % tt == 0
    out = _peer(h2.astype(BF16).T, peer_u[l].astype(BF16), peer_v[l].astype(BF16).T, r2, e2, b1, e1, x1, mods,
                final_norm[None, :], tc // tt, ll // tt, tt, n1=SUBLANES)
    return out[:tc].reshape(bc, lc, d), out[tc:].reshape(bl, ll, d), new_state
```

```python
import functools
import math

import jax
import jax.numpy as jnp
import numpy as np
from jax import lax
from jax.experimental import pallas as pl
from jax.experimental.pallas import tpu as pltpu

F32 = jnp.float32
BF16 = jnp.bfloat16

GRID_W = 64
HEAD = 64
TOPK = 16
EPS = 1e-6
GN_EPS = 64e-5
LANES = 128
SUBLANES = 8
VMEM_LIMIT = 56 * 1024 * 1024


def _params(*sem):
    return pltpu.CompilerParams(dimension_semantics=sem, vmem_limit_bytes=VMEM_LIMIT)


def _split(x):
    hi = x.astype(BF16)
    lo = (x - hi.astype(F32)).astype(BF16)
    return hi, lo


def _dot(a, b):
    return jnp.dot(a, b, preferred_element_type=F32)


def _dot_split(ah, al, bh, bl):
    return _dot(ah, bh) + (_dot(al, bh) + _dot(ah, bl))


def _dot3(a, b):
    ah, al = _split(a)
    bh, bl = _split(b)
    return _dot_split(ah, al, bh, bl)


def _full(shape):
    n = len(shape)
    return pl.BlockSpec(shape, lambda *_: (0,) * n)


def _const(shape):
    n = len(shape)
    return pl.BlockSpec(shape, lambda *_: (0,) * n, pipeline_mode=pl.Buffered(1))


def _ada_kernel(c_ref, w_ref, b_ref, o_ref):
    c = c_ref[...]
    s = c * jax.nn.sigmoid(c)
    o_ref[...] = _dot3(s, w_ref[...]) + b_ref[...]


def _ada(c_rows, w, b):
    rows, d = c_rows.shape
    n = w.shape[1]
    bn = n // 4
    return pl.pallas_call(
        _ada_kernel,
        grid=(4,),
        in_specs=[_full((rows, d)),
                  pl.BlockSpec((d, bn), lambda j: (0, j)),
                  pl.BlockSpec((1, bn), lambda j: (0, j))],
        out_specs=pl.BlockSpec((rows, bn), lambda j: (0, j)),
        out_shape=jax.ShapeDtypeStruct((rows, n), F32),
        compiler_params=_params("arbitrary"),
        name="ada",
    )(c_rows, w, b.reshape(1, n))


def _headsum(z, hs):
    zh, zl = _split(z)
    return _dot(zh, hs) + _dot(zl, hs)


def _mix_pre_kernel(n_ctx_tiles, x_ref, mod_ref, nm_ref, win_ref, cw_ref, l1_ref, w2_ref, a2_ref,
                    g2_ref, vec_ref, hs_ref, sh_ref, dr_ref, gb_ref, hy_ref):
    tm, d = x_ref.shape
    dr = hs_ref.shape[0]
    x = x_ref[...]
    mod = mod_ref[0]
    shift1 = mod[:, 0:d]
    scale1 = mod[:, d:2 * d]
    xn = x * lax.rsqrt(jnp.mean(x * x, axis=-1, keepdims=True) + EPS) * nm_ref[...]
    hb = (xn * (1.0 + scale1) + shift1).astype(BF16)

    pos = lax.broadcasted_iota(jnp.int32, (tm, dr), 0)
    is_ctx = pl.program_id(0) < n_ctx_tiles
    row = pos % GRID_W
    keep_prev = jnp.where(is_ctx, jnp.where(pos == 0, 0.0, 1.0), jnp.where(row == 0, 0.0, 1.0))
    keep_next = jnp.where(is_ctx, jnp.where(pos == tm - 1, 0.0, 1.0),
                          jnp.where(row == GRID_W - 1, 0.0, 1.0))

    def proj(c):
        u = _dot(hb, win_ref[:, c * dr:(c + 1) * dr])
        cw = cw_ref[:, c * dr:(c + 1) * dr]
        up = pltpu.roll(u, 1, 0) * keep_prev
        un = pltpu.roll(u, tm - 1, 0) * keep_next
        return up * cw[0:1] + u * cw[1:2] + un * cw[2:3]

    r = proj(0)
    k = proj(1)
    v = proj(2)
    hy_ref[0] = proj(3)
    hy_ref[1] = proj(4)
    hy_ref[2] = proj(5)

    lo = _dot(hb, l1_ref[...])
    vec = vec_ref[...]
    k_k = vec[4:5]
    k_a = vec[5:6]
    hs = hs_ref[...]
    kkr = k * k_k
    kk = kkr * lax.rsqrt(jnp.maximum(_headsum(kkr * kkr, hs), 1e-24))
    sh_ref[0] = r
    sh_ref[1] = kk
    sh_ref[2] = v
    gb_ref[0] = _dot3(jax.nn.sigmoid(lo[:, 4 * LANES:5 * LANES]), g2_ref[...])
    bon = jnp.zeros_like(r)
    for dd in range(2):
        w = vec[dd:dd + 1] + _dot3(jnp.tanh(lo[:, dd * LANES:(dd + 1) * LANES]), w2_ref[dd])
        z = -w
        softplus = jnp.maximum(z, 0.0) + jnp.log1p(jnp.exp(-jnp.abs(z)))
        dr_ref[dd, 0] = jnp.exp(-jnp.exp(-softplus - 0.5))
        a = jax.nn.sigmoid(vec[2 + dd:3 + dd]
                           + _dot3(lo[:, (2 + dd) * LANES:(3 + dd) * LANES], a2_ref[dd]))
        kd = k * (1.0 + (a - 1.0) * k_a)
        dr_ref[dd, 1] = kd
        dr_ref[dd, 2] = kk * a
        bon = bon + r * kd * vec[6 + dd:7 + dd]
    gb_ref[1] = _headsum(bon, hs) * v


def _mix_pre(x, mods, n_ctx_tiles, tiles_per_lat, tm, norm_mix, w_in, conv_w, lora1, w2, a2, g2, vecs, hs):
    t, d = x.shape
    dr = hs.shape[0]
    n_tiles = t // tm

    def mod_idx(i):
        return (jnp.where(i < n_ctx_tiles, 0, 1 + (i - n_ctx_tiles) // tiles_per_lat), 0, 0)

    return pl.pallas_call(
        functools.partial(_mix_pre_kernel, n_ctx_tiles),
        grid=(n_tiles,),
        in_specs=[pl.BlockSpec((tm, d), lambda i: (i, 0)),
                  pl.BlockSpec((1, 1, mods.shape[-1]), mod_idx),
                  _full(norm_mix.shape), _full(w_in.shape), _full(conv_w.shape), _full(lora1.shape),
                  _full(w2.shape), _full(a2.shape), _full(g2.shape), _full(vecs.shape), _full(hs.shape)],
        out_specs=[pl.BlockSpec((3, tm, dr), lambda i: (0, i, 0)),
                   pl.BlockSpec((2, 3, tm, dr), lambda i: (0, 0, i, 0)),
                   pl.BlockSpec((2, tm, dr), lambda i: (0, i, 0)),
                   pl.BlockSpec((3, tm, dr), lambda i: (0, i, 0))],
        out_shape=[jax.ShapeDtypeStruct((3, t, dr), F32),
                   jax.ShapeDtypeStruct((2, 3, t, dr), F32),
                   jax.ShapeDtypeStruct((2, t, dr), F32),
                   jax.ShapeDtypeStruct((3, t, dr), F32)],
        compiler_params=_params("arbitrary"),
        name="mix_pre",
    )(x, mods, norm_mix, w_in, conv_w, lora1, w2, a2, g2, vecs, hs)


def _wkv_kernel(n_fwd, r_ref, kk_ref, v_ref, d_ref, k_ref, kka_ref, s0_ref, y_ref, sfin_ref, s_scr):
    tb = r_ref.shape[1]
    ni = s_scr.shape[0]
    rev = pl.program_id(0) >= n_fwd
    tblk = pl.program_id(1)

    @pl.when(tblk == 0)
    def _():
        s_scr[...] = s0_ref[0]

    def step(tau, carry):
        t = jnp.where(rev, tb - 1 - tau, tau)
        rt = r_ref[0, t]
        kkt = kk_ref[0, t]
        dt = d_ref[0, t]
        kt = k_ref[0, t]
        kkat = kka_ref[0, t]
        drt = dt * rt
        c1 = jnp.sum(kt * rt, axis=0, keepdims=True)
        c2 = jnp.sum(kkat * rt, axis=0, keepdims=True)

        def rows8(i8, c):
            base = pl.multiple_of(i8 * SUBLANES, SUBLANES)
            vblk = v_ref[0, t, pl.ds(base, SUBLANES), :]
            outs = []
            for jj in range(SUBLANES):
                s = s_scr[base + jj]
                vi = vblk[jj:jj + 1]
                sa = jnp.sum(s * kkt, axis=0, keepdims=True)
                q = jnp.sum(s * drt, axis=0, keepdims=True)
                s_scr[base + jj] = s * dt + (vi * kt - sa * kkat)
                outs.append(q + vi * c1 - sa * c2)
            y_ref[0, t, pl.ds(base, SUBLANES), :] = jnp.concatenate(outs, axis=0)
            return c

        lax.fori_loop(0, ni // SUBLANES, rows8, 0)
        return carry

    lax.fori_loop(0, tb, step, 0)

    @pl.when(tblk == pl.num_programs(1) - 1)
    def _():
        sfin_ref[0] = s_scr[...]


def _wkv(r, kk, v, d, k, kka, s0, tb):
    gh, t, _, _ = r.shape
    g = 2 * gh
    ni = v.shape[2]
    nt = t // tb

    def tsel(gi, ti):
        return jnp.where(gi >= gh, nt - 1 - ti, ti)

    shared = pl.BlockSpec((1, tb, HEAD, LANES), lambda gi, ti: (gi % gh, tsel(gi, ti), 0, 0))
    perdir = pl.BlockSpec((1, tb, HEAD, LANES), lambda gi, ti: (gi, tsel(gi, ti), 0, 0))
    state = pl.BlockSpec((1, ni, HEAD, LANES), lambda gi, ti: (gi, 0, 0, 0))
    return pl.pallas_call(
        functools.partial(_wkv_kernel, gh),
        grid=(g, nt),
        in_specs=[shared, shared,
                  pl.BlockSpec((1, tb, ni, LANES), lambda gi, ti: (gi % gh, tsel(gi, ti), 0, 0)),
                  perdir, perdir, perdir, state],
        out_specs=[pl.BlockSpec((1, tb, ni, LANES), lambda gi, ti: (gi, tsel(gi, ti), 0, 0)), state],
        out_shape=[jax.ShapeDtypeStruct((g, t, ni, LANES), F32),
                   jax.ShapeDtypeStruct((g, ni, HEAD, LANES), F32)],
        scratch_shapes=[pltpu.VMEM((ni, HEAD, LANES), F32)],
        compiler_params=_params("arbitrary", "arbitrary"),
        name="wkv",
    )(r, kk, v, d, k, kka, s0)


def _dft_mats(l):
    n = 2 * l
    nfp = l + SUBLANES
    kf = np.arange(nfp, dtype=np.int64)
    ang = (2.0 * np.pi / n) * ((kf[:, None] * np.arange(l, dtype=np.int64)[None, :]) % n)
    valid = (kf <= l)[:, None]
    c = np.where(valid, np.cos(ang), 0.0)
    s = np.where(valid, np.sin(ang), 0.0)
    wk = np.where((kf == 0) | (kf == l), 1.0 / n, 2.0 / n)[None, :]
    mats = (c, s, c.T * wk, -(s.T) * wk)
    return [m for mat in mats for m in _split(jnp.asarray(mat.astype(np.float32)))]


def _hy_filter_kernel(zf_ref, w1_ref, b1_ref, w2_ref, b2_ref, fq_ref, w3_ref, dl_ref, bias_ref,
                      ch_ref, cl_ref, sh_ref, sl_ref, g_ref):
    zf = zf_ref[...]
    hid = jnp.sin(fq_ref[0:1] * (_dot3(zf, w1_ref[...]) + b1_ref[...]))
    hid = jnp.sin(fq_ref[1:2] * (_dot3(hid, w2_ref[...]) + b2_ref[...]))
    win = jnp.exp(-zf[:, 0:1] * jnp.abs(dl_ref[...]))
    hh, hl = _split(hid)
    filt = []
    for o in range(4):
        wh, wl = _split(w3_ref[o])
        filt.append(_dot_split(hh, hl, wh, wl) * win)
    for od in range(2):
        hf, hbk = filt[2 * od], filt[2 * od + 1]
        ph, plo = _split(hf + hbk)
        mh, ml = _split(hbk - hf)
        g_ref[od, 0] = _dot_split(ch_ref[...], cl_ref[...], ph, plo) + bias_ref[od:od + 1]
        g_ref[od, 1] = _dot_split(sh_ref[...], sl_ref[...], mh, ml)


def _hy_filter(zf, w1, b1, w2, b2, fq, w3, deltas, bias, mats):
    l = zf.shape[0]
    nfp = mats[0].shape[0]
    c = w3.shape[-1]
    cb = LANES
    return pl.pallas_call(
        _hy_filter_kernel,
        grid=(c // cb,),
        in_specs=[_full(zf.shape), _full(w1.shape), _full(b1.shape), _full(w2.shape), _full(b2.shape),
                  _full(fq.shape),
                  pl.BlockSpec((4, w3.shape[1], cb), lambda j: (0, 0, j)),
                  pl.BlockSpec((1, cb), lambda j: (0, j)),
                  pl.BlockSpec((2, cb), lambda j: (0, j)),
                  _const((nfp, l)), _const((nfp, l)), _const((nfp, l)), _const((nfp, l))],
        out_specs=pl.BlockSpec((2, 2, nfp, cb), lambda j: (0, 0, 0, j)),
        out_shape=jax.ShapeDtypeStruct((2, 2, nfp, c), F32),
        compiler_params=_params("arbitrary"),
        name="hy_filt",
    )(zf, w1, b1, w2, b2, fq, w3, deltas, bias, *mats[:4])


def _hy_conv_kernel(hy_ref, g_ref, ch_ref, cl_ref, sh_ref, sl_ref, cih_ref, cil_ref, sih_ref, sil_ref, o_ref):
    def conv(u, od):
        uh, ul = _split(u)
        re = _dot_split(ch_ref[...], cl_ref[...], uh, ul)
        im = -_dot_split(sh_ref[...], sl_ref[...], uh, ul)
        gre = g_ref[od, 0]
        gim = g_ref[od, 1]
        ah, al = _split(re * gre - im * gim)
        bh, bl = _split(re * gim + im * gre)
        return (_dot_split(cih_ref[...], cil_ref[...], ah, al)
                + _dot_split(sih_ref[...], sil_ref[...], bh, bl))

    z = hy_ref[1] * conv(hy_ref[0], 0)
    o_ref[...] = hy_ref[2] * conv(z, 1)


def _hy_conv(hy, g, mats, l, first_seq, n_seq):
    t, c = hy.shape[1], hy.shape[2]
    nfp = mats[0].shape[0]
    cb = 2 * LANES
    return pl.pallas_call(
        _hy_conv_kernel,
        grid=(n_seq, c // cb),
        in_specs=[pl.BlockSpec((3, l, cb), lambda b, j: (0, first_seq + b, j)),
                  pl.BlockSpec((2, 2, nfp, cb), lambda b, j: (0, 0, 0, j)),
                  _const((nfp, l)), _const((nfp, l)), _const((nfp, l)), _const((nfp, l)),
                  _const((l, nfp)), _const((l, nfp)), _const((l, nfp)), _const((l, nfp))],
        out_specs=pl.BlockSpec((l, cb), lambda b, j: (b, j)),
        out_shape=jax.ShapeDtypeStruct((n_seq * l, c), F32),
        compiler_params=_params("arbitrary", "arbitrary"),
        name="hy_conv",
    )(hy, g, *mats)


def _mix_post_kernel(y_ref, gb_ref, z_ref, x_ref, mod_ref, vec_ref, wout_ref, nf_ref, hs_ref, x1_ref, h2_ref):
    d = x_ref.shape[1]
    dr = hs_ref.shape[0]
    hs = hs_ref[...]
    vec = vec_ref[...]
    mod = mod_ref[0]
    gate1 = mod[:, 2 * d:3 * d]
    shift2 = mod[:, 3 * d:4 * d]
    scale2 = mod[:, 4 * d:5 * d]
    y = y_ref[...]
    mu = _headsum(y, hs) * (1.0 / HEAD)
    yc = y - mu
    var = _headsum(yc * yc, hs) * (1.0 / HEAD)
    yn = yc * lax.rsqrt(var + GN_EPS) * vec[0:1] + vec[1:2]
    ya = (yn + gb_ref[1]) * gb_ref[0]
    z = z_ref[...]
    yb = z * lax.rsqrt(jnp.mean(z * z, axis=-1, keepdims=True) + EPS) * vec[2:3]
    m = _dot(ya.astype(BF16), wout_ref[0:dr]) + _dot(yb.astype(BF16), wout_ref[dr:])
    x1 = x_ref[...] + gate1 * m
    x1_ref[...] = x1
    xn = x1 * lax.rsqrt(jnp.mean(x1 * x1, axis=-1, keepdims=True) + EPS) * nf_ref[...]
    h2_ref[...] = xn * (1.0 + scale2) + shift2


def _mix_post(ysum, gb, z, x, mods, n_ctx_tiles, tiles_per_lat, tm, vecs, w_out, norm_ffn, hs):
    t, d = x.shape
    dr = hs.shape[0]

    def mod_idx(i):
        return (jnp.where(i < n_ctx_tiles, 0, 1 + (i - n_ctx_tiles) // tiles_per_lat), 0, 0)

    tok = pl.BlockSpec((tm, dr), lambda i: (i, 0))
    tokd = pl.BlockSpec((tm, d), lambda i: (i, 0))
    return pl.pallas_call(
        _mix_post_kernel,
        grid=(t // tm,),
        in_specs=[tok, pl.BlockSpec((2, tm, dr), lambda i: (0, i, 0)), tok, tokd,
                  pl.BlockSpec((1, 1, mods.shape[-1]), mod_idx),
                  _full(vecs.shape), _full(w_out.shape), _full(norm_ffn.shape), _full(hs.shape)],
        out_specs=[tokd, tokd],
        out_shape=[jax.ShapeDtypeStruct((t, d), F32), jax.ShapeDtypeStruct((t, d), F32)],
        compiler_params=_params("arbitrary"),
        name="mix_post",
    )(ysum, gb, z, x, mods, vecs, w_out, norm_ffn, hs)


def _second_level_pairs():
    return [(a, b) for a in range(TOPK) for b in range(TOPK) if (a + 1) * (b + 1) <= TOPK]


def _route_kernel(h2_ref, wq_ref, keys_ref, r2_ref, b1_ref, e1_ref, e2_ref, sv_scr, s_scr, rk_scr):
    nh = keys_ref.shape[1]
    nk = keys_ref.shape[2]
    half = keys_ref.shape[3]
    tt = h2_ref.shape[0]
    q = _dot(h2_ref[...].astype(BF16), wq_ref[...])
    key_id = lax.broadcasted_iota(jnp.int32, (nk, tt), 0).astype(F32)
    neg = jnp.float32(-jnp.inf)

    for h in range(nh):
        for p in range(2):
            c0 = (h * 2 + p) * half
            qh, ql = _split(q[:, c0:c0 + half])
            kh, kl = _split(keys_ref[p, h])
            dims = (((1,), (1,)), ((), ()))
            s = (lax.dot_general(kh, qh, dims, preferred_element_type=F32)
                 + (lax.dot_general(kl, qh, dims, preferred_element_type=F32)
                    + lax.dot_general(kh, ql, dims, preferred_element_type=F32)))
            s_scr[p, h] = s

            def extract(r, carry, p=p, h=h):
                x, rank = carry
                m = jnp.max(x, axis=0, keepdims=True)
                first = jnp.min(jnp.where(x == m, key_id, float(nk)), axis=0, keepdims=True)
                sel = key_id == first
                sv_scr[p, r, h:h + 1, :] = m
                return jnp.where(sel, neg, x), jnp.where(sel, r.astype(F32), rank)

            _, rank = lax.fori_loop(0, TOPK, extract, (s, jnp.full((nk, tt), float(TOPK), F32)))
            rk_scr[p, h] = rank

    v1 = [sv_scr[0, a] for a in range(TOPK)]
    v2 = [sv_scr[1, b] for b in range(TOPK)]
    pairs = _second_level_pairs()
    top = v1[0] + v2[0]

    def select(_, carry):
        cand, cnt, zsum = list(carry[0]), list(carry[1]), carry[2]
        m = cand[0]
        for cv in cand[1:]:
            m = jnp.maximum(m, cv)
        zsum = zsum + jnp.exp(m - top)
        found = jnp.zeros((nh, tt), F32)
        for ci, (a, _b) in enumerate(pairs):
            eq = jnp.where(cand[ci] == m, 1.0, 0.0)
            hit = eq * (1.0 - found)
            found = jnp.maximum(found, eq)
            cand[ci] = jnp.where(hit > 0.0, neg, cand[ci])
            cnt[a] = cnt[a] + hit
        return tuple(cand), tuple(cnt), zsum

    zero = jnp.zeros((nh, tt), F32)
    _, cnt, zsum = lax.fori_loop(0, TOPK, select,
                                 (tuple(v1[a] + v2[b] for a, b in pairs), (zero,) * TOPK, zero))
    zinv = 1.0 / zsum

    for h in range(nh):
        rank1 = rk_scr[0, h]
        b1 = jnp.zeros((nk, tt), F32)
        for a in range(TOPK):
            b1 = b1 + jnp.where(rank1 == float(a), cnt[a][h:h + 1], 0.0)
        b1_ref[h] = b1
        r2_ref[h] = rk_scr[1, h].astype(BF16)
        e1_ref[h] = jnp.exp(s_scr[0, h] - v1[0][h:h + 1]) * zinv[h:h + 1]
        e2_ref[h] = jnp.exp(s_scr[1, h] - v2[0][h:h + 1]).astype(BF16)


def _route(h2, wq, keys, tt):
    t, d = h2.shape
    _, nh, nk, _ = keys.shape
    out = jax.ShapeDtypeStruct((nh, nk, t), F32)
    out16 = jax.ShapeDtypeStruct((nh, nk, t), BF16)
    ospec = pl.BlockSpec((nh, nk, tt), lambda i: (0, 0, i))
    return pl.pallas_call(
        _route_kernel,
        grid=(t // tt,),
        in_specs=[pl.BlockSpec((tt, d), lambda i: (i, 0)), _full(wq.shape), _full(keys.shape)],
        out_specs=[ospec, ospec, ospec, ospec],
        out_shape=[out16, out, out, out16],
        scratch_shapes=[pltpu.VMEM((2, TOPK, nh, tt), F32),
                        pltpu.VMEM((2, nh, nk, tt), F32),
                        pltpu.VMEM((2, nh, nk, tt), F32)],
        compiler_params=_params("arbitrary"),
        name="route",
    )(h2, wq, keys)


def _peer_kernel(h2t_ref, u_ref, vt_ref, r2_ref, e2_ref, b1_ref, e1_ref, x1_ref, mod_ref, fn_ref,
                 o_ref, acc_scr, a_scr):
    d = x1_ref.shape[1]
    nh, nk, tt = r2_ref.shape
    n1 = b1_ref.shape[1]
    eb = pl.program_id(1)

    @pl.when(eb == 0)
    def _():
        acc_scr[...] = jnp.zeros_like(acc_scr)

    hid = _dot(u_ref[...], h2t_ref[...])
    pk = 2 * SUBLANES
    for i in range(n1):
        w = jnp.zeros((nk // pk, pk, tt), BF16)
        for h in range(nh):
            b1r = jnp.broadcast_to(b1_ref[h, i:i + 1, :], (pk, tt)).astype(BF16)[None]
            e1r = jnp.broadcast_to(e1_ref[h, i:i + 1, :], (pk, tt)).astype(BF16)[None]
            r2 = r2_ref[h].reshape(nk // pk, pk, tt)
            e2 = e2_ref[h].reshape(nk // pk, pk, tt)
            w = w + jnp.where(r2 < b1r, e1r * e2, jnp.zeros((), BF16))
        hi = hid[i * nk:(i + 1) * nk]
        act = 0.5 * hi * (1.0 + lax.erf(hi * (1.0 / math.sqrt(2.0))))
        a_scr[i * nk:(i + 1) * nk, :] = (act.astype(BF16).reshape(nk // pk, pk, tt) * w).reshape(nk, tt)
    acc_scr[...] += _dot(vt_ref[...], a_scr[...])

    @pl.when(eb == pl.num_programs(1) - 1)
    def _():
        gate2 = mod_ref[0][:, 5 * d:6 * d]
        xo = x1_ref[...] + gate2 * acc_scr[...].T
        o_ref[...] = xo * lax.rsqrt(jnp.mean(xo * xo, axis=-1, keepdims=True) + EPS) * fn_ref[...]


def _peer(h2t, u, vt, r2, e2, b1, e1, x1, mods, final_norm, n_ctx_tiles, tiles_per_lat, tt, n1):
    d, t = h2t.shape
    ne = u.shape[0]
    nh, nk, _ = r2.shape
    eb = n1 * nk

    def mod_idx(i, e):
        return (jnp.where(i < n_ctx_tiles, 0, 1 + (i - n_ctx_tiles) // tiles_per_lat), 0, 0)

    dense = pl.BlockSpec((nh, nk, tt), lambda i, e: (0, 0, i))
    rows = pl.BlockSpec((nh, n1, tt), lambda i, e: (0, e, i))
    return pl.pallas_call(
        _peer_kernel,
        grid=(t // tt, ne // eb),
        in_specs=[pl.BlockSpec((d, tt), lambda i, e: (0, i)),
                  pl.BlockSpec((eb, d), lambda i, e: (e, 0)),
                  pl.BlockSpec((d, eb), lambda i, e: (0, e)),
                  dense, dense, rows, rows,
                  pl.BlockSpec((tt, d), lambda i, e: (i, 0)),
                  pl.BlockSpec((1, 1, mods.shape[-1]), mod_idx),
                  _full(final_norm.shape)],
        out_specs=pl.BlockSpec((tt, d), lambda i, e: (i, 0)),
        out_shape=jax.ShapeDtypeStruct((t, d), F32),
        scratch_shapes=[pltpu.VMEM((d, tt), F32), pltpu.VMEM((eb, tt), BF16)],
        compiler_params=_params("arbitrary", "arbitrary"),
        name="peer",
    )(h2t, u, vt, r2, e2, b1, e1, x1, mods, final_norm)


def _hyena_features(l, bands):
    t = jnp.linspace(0.0, 1.0, l, dtype=F32)[:, None]
    wpos = 2.0 * math.pi * jnp.arange(l, dtype=F32)[:, None] / l
    f = jnp.linspace(1e-4, bands - 1, bands, dtype=F32)[None, :]
    return jnp.concatenate([t, jnp.cos(f * wpos), -jnp.sin(f * wpos)], axis=-1)


def kernel(x_prompt, x_sample, state_rwkv, c, c_ctx, w_ada, b_ada, norm_mix, norm_ffn, w_in, conv_w, w_out, rwkv_w0, rwkv_w1, rwkv_w2, rwkv_a0, rwkv_a1, rwkv_a2, rwkv_g1, rwkv_g2, rwkv_k_k, rwkv_k_a, rwkv_r_k, rwkv_ln_w, rwkv_ln_b, hy_f_w1, hy_f_b1, hy_f_w2, hy_f_b2, hy_freq, hy_f_w3, hy_bias, hy_norm, peer_wq, peer_keys, peer_u, peer_v, final_norm):
    bc, lc, d = x_prompt.shape
    bl, ll, _ = x_sample.shape
    depth = w_ada.shape[0]
    assert depth == 1
    dr = rwkv_w0.shape[-1]
    dh = hy_norm.shape[-1]
    nh = dr // HEAD
    assert dr == dh and w_in.shape[-1] == 3 * dr + 3 * dh
    tc, tl = bc * lc, bl * ll
    t = tc + tl
    tm = lc
    assert ll % tm == 0 and tm % GRID_W == 0 and tc % ll == 0
    n_ctx_tiles = tc // tm
    tiles_per_lat = ll // tm
    ghc = (bc * nh) // LANES
    assert ghc * LANES == bc * nh
    parts = LANES // (bl * nh)
    assert parts * bl * nh == LANES and HEAD % (parts * SUBLANES) == 0
    nil = HEAD // parts
    l = 0

    rows = jnp.concatenate([c_ctx[None, :], c, jnp.zeros((SUBLANES - 1 - bl, d), F32)], axis=0)
    mods = _ada(rows, w_ada[l], b_ada[l]).reshape(SUBLANES, 1, 6 * d)

    x = jnp.concatenate([x_prompt.reshape(tc, d), x_sample.reshape(tl, d)], axis=0)

    def pad_cols(w):
        return jnp.pad(w, ((0, 0), (0, LANES - w.shape[1])))

    def pad_rows(w):
        return jnp.pad(w, ((0, LANES - w.shape[0]), (0, 0)))

    lora1 = jnp.concatenate([pad_cols(rwkv_w1[l, 0]), pad_cols(rwkv_w1[l, 1]), pad_cols(rwkv_a1[l, 0]),
                             pad_cols(rwkv_a1[l, 1]), pad_cols(rwkv_g1[l])], axis=1).astype(BF16)
    w2 = jnp.stack([pad_rows(rwkv_w2[l, 0]), pad_rows(rwkv_w2[l, 1])])
    a2 = jnp.stack([pad_rows(rwkv_a2[l, 0]), pad_rows(rwkv_a2[l, 1])])
    vecs_pre = jnp.stack([rwkv_w0[l, 0], rwkv_w0[l, 1], rwkv_a0[l, 0], rwkv_a0[l, 1], rwkv_k_k[l], rwkv_k_a[l],
                          rwkv_r_k[l, 0].reshape(dr), rwkv_r_k[l, 1].reshape(dr)])
    head_id = jnp.arange(dr, dtype=jnp.int32) // HEAD
    hs = (head_id[:, None] == head_id[None, :]).astype(BF16)
    shared, perdir, gb, hy = _mix_pre(x, mods, n_ctx_tiles, tiles_per_lat, tm, norm_mix[l][None, :],
                                      w_in[l].astype(BF16), conv_w[l], lora1, w2, a2, rwkv_g2[l], vecs_pre, hs)

    def ctx_lanes(a):
        a = a.reshape(ghc, bc // ghc, lc, nh, HEAD)
        return a.transpose(0, 2, 4, 1, 3).reshape(ghc, lc, HEAD, LANES)

    def lat_lanes(a):
        a = a.reshape(bl, ll, nh, HEAD).transpose(1, 3, 0, 2)
        return jnp.broadcast_to(a[..., None], (ll, HEAD, bl, nh, parts)).reshape(1, ll, HEAD, LANES)

    def lat_rows(a):
        a = a.reshape(bl, ll, nh, parts, nil).transpose(1, 4, 0, 2, 3)
        return a.reshape(1, ll, nil, LANES)

    sh_c, sh_l = shared[:, :tc], shared[:, tc:]
    pd_c, pd_l = perdir[:, :, :tc], perdir[:, :, tc:]
    ctx_dir = [jnp.concatenate([ctx_lanes(pd_c[0, j]), ctx_lanes(pd_c[1, j])], axis=0) for j in range(3)]
    y_ctx, s_ctx = _wkv(ctx_lanes(sh_c[0]), ctx_lanes(sh_c[1]), ctx_lanes(sh_c[2]), *ctx_dir,
                        jnp.zeros((2 * ghc, HEAD, HEAD, LANES), F32), tb=32)
    lat_dir = [jnp.concatenate([lat_lanes(pd_l[0, j]), lat_lanes(pd_l[1, j])], axis=0) for j in range(3)]
    s0 = state_rwkv[:, l].reshape(bl, 2, nh, parts, nil, HEAD).transpose(1, 4, 5, 0, 2, 3)
    y_lat, _ = _wkv(lat_lanes(sh_l[0]), lat_lanes(sh_l[1]), lat_rows(sh_l[2]), *lat_dir,
                    s0.reshape(2, nil, HEAD, LANES), tb=32)
    ys_ctx = (y_ctx[:ghc] + y_ctx[ghc:]).reshape(ghc, lc, HEAD, bc // ghc, nh)
    ys_ctx = ys_ctx.transpose(0, 3, 1, 4, 2).reshape(tc, dr)
    ys_lat = (y_lat[0] + y_lat[1]).reshape(ll, nil, bl, nh, parts).transpose(2, 0, 3, 4, 1).reshape(tl, dr)
    ysum = jnp.concatenate([ys_ctx, ys_lat], axis=0)
    new_state = s_ctx.reshape(2, ghc, HEAD, HEAD, bc // ghc, nh).transpose(1, 4, 0, 5, 2, 3)
    new_state = new_state.reshape(bc, 1, 2, nh, HEAD, HEAD)

    bands = (hy_f_w1.shape[1] - 1) // 2
    femb = hy_f_w1.shape[1]
    fpad = 32
    w1p = jnp.pad(hy_f_w1[l], ((0, fpad - femb), (0, 0)))
    hidden = hy_f_w2.shape[-1]
    w3 = hy_f_w3[l].reshape(hidden, 4, dh).transpose(1, 0, 2)
    max_decay = math.log(1e-2) / 0.3
    min_decay = math.log(1e-2) / 1.5
    deltas = jnp.linspace(min_decay, max_decay, dh, dtype=F32)[None, :]
    z_parts = []
    for seq_len, first_seq, n_seq in ((lc, 0, bc), (ll, tc // ll, bl)):
        mats = _dft_mats(seq_len)
        zf = jnp.pad(_hyena_features(seq_len, bands), ((0, 0), (0, fpad - femb)))
        g = _hy_filter(zf, w1p, hy_f_b1[l][None, :], hy_f_w2[l], hy_f_b2[l][None, :], hy_freq[l], w3,
                       deltas, hy_bias[l], mats)
        z_parts.append(_hy_conv(hy, g, mats, seq_len, first_seq, n_seq))
    zhy = jnp.concatenate(z_parts, axis=0)

    vecs_post = jnp.concatenate([jnp.stack([rwkv_ln_w[l], rwkv_ln_b[l], hy_norm[l]]),
                                 jnp.zeros((SUBLANES - 3, dr), F32)], axis=0)
    x1, h2 = _mix_post(ysum, gb, zhy, x, mods, n_ctx_tiles, tiles_per_lat, tm, vecs_post,
                       w_out[l].astype(BF16), norm_ffn[l][None, :], hs)

    r2, b1, e1, e2 = _route(h2, peer_wq[l].astype(BF16), peer_keys[l], tt=256)
    tt = 512
    assert ll % tt == 0 and tc % tt == 0
    out = _peer(h2.astype(BF16).T, peer_u[l].astype(BF16), peer_v[l].astype(BF16).T, r2, e2, b1, e1, x1, mods,
                final_norm[None, :], tc // tt, ll // tt, tt, n1=SUBLANES)
    return out[:tc].reshape(bc, lc, d), out[tc:].reshape(bl, ll, d), new_state
```

```python
import functools
import math

import jax
import jax.numpy as jnp
import numpy as np
from jax import lax
from jax.experimental import pallas as pl
from jax.experimental.pallas import tpu as pltpu

F32 = jnp.float32
BF16 = jnp.bfloat16

GRID_W = 64
HEAD = 64
TOPK = 16
EPS = 1e-6
GN_EPS = 64e-5
LANES = 128
SUBLANES = 8
VMEM_LIMIT = 56 * 1024 * 1024


def _params(*sem):
    return pltpu.CompilerParams(dimension_semantics=sem, vmem_limit_bytes=VMEM_LIMIT)


def _split(x):
    hi = x.astype(BF16)
    lo = (x - hi.astype(F32)).astype(BF16)
    return hi, lo


def _dot(a, b):
    return jnp.dot(a, b, preferred_element_type=F32)


def _dot_split(ah, al, bh, bl):
    return _dot(ah, bh) + (_dot(al, bh) + _dot(ah, bl))


def _dot3(a, b):
    ah, al = _split(a)
    bh, bl = _split(b)
    return _dot_split(ah, al, bh, bl)


def _full(shape):
    n = len(shape)
    return pl.BlockSpec(shape, lambda *_: (0,) * n)


def _const(shape):
    n = len(shape)
    return pl.BlockSpec(shape, lambda *_: (0,) * n, pipeline_mode=pl.Buffered(1))


def _ada_kernel(c_ref, w_ref, b_ref, o_ref):
    c = c_ref[...]
    s = c * jax.nn.sigmoid(c)
    o_ref[...] = _dot3(s, w_ref[...]) + b_ref[...]


def _ada(c_rows, w, b):
    rows, d = c_rows.shape
    n = w.shape[1]
    bn = n // 4
    return pl.pallas_call(
        _ada_kernel,
        grid=(4,),
        in_specs=[_full((rows, d)),
                  pl.BlockSpec((d, bn), lambda j: (0, j)),
                  pl.BlockSpec((1, bn), lambda j: (0, j))],
        out_specs=pl.BlockSpec((rows, bn), lambda j: (0, j)),
        out_shape=jax.ShapeDtypeStruct((rows, n), F32),
        compiler_params=_params("arbitrary"),
        name="ada",
    )(c_rows, w, b.reshape(1, n))


def _headsum(z, hs):
    zh, zl = _split(z)
    return _dot(zh, hs) + _dot(zl, hs)


def _mix_pre_kernel(n_ctx_tiles, x_ref, mod_ref, nm_ref, win_ref, cw_ref, l1_ref, w2_ref, a2_ref,
                    g2_ref, vec_ref, hs_ref, sh_ref, dr_ref, gb_ref, hy_ref):
    tm, d = x_ref.shape
    dr = hs_ref.shape[0]
    x = x_ref[...]
    mod = mod_ref[0]
    shift1 = mod[:, 0:d]
    scale1 = mod[:, d:2 * d]
    xn = x * lax.rsqrt(jnp.mean(x * x, axis=-1, keepdims=True) + EPS) * nm_ref[...]
    hb = (xn * (1.0 + scale1) + shift1).astype(BF16)

    pos = lax.broadcasted_iota(jnp.int32, (tm, dr), 0)
    is_ctx = pl.program_id(0) < n_ctx_tiles
    row = pos % GRID_W
    keep_prev = jnp.where(is_ctx, jnp.where(pos == 0, 0.0, 1.0), jnp.where(row == 0, 0.0, 1.0))
    keep_next = jnp.where(is_ctx, jnp.where(pos == tm - 1, 0.0, 1.0),
                          jnp.where(row == GRID_W - 1, 0.0, 1.0))

    def proj(c):
        u = _dot(hb, win_ref[:, c * dr:(c + 1) * dr])
        cw = cw_ref[:, c * dr:(c + 1) * dr]
        up = pltpu.roll(u, 1, 0) * keep_prev
        un = pltpu.roll(u, tm - 1, 0) * keep_next
        return up * cw[0:1] + u * cw[1:2] + un * cw[2:3]

    r = proj(0)
    k = proj(1)
    v = proj(2)
    hy_ref[0] = proj(3)
    hy_ref[1] = proj(4)
    hy_ref[2] = proj(5)

    lo = _dot(hb, l1_ref[...])
    vec = vec_ref[...]
    k_k = vec[4:5]
    k_a = vec[5:6]
    hs = hs_ref[...]
    kkr = k * k_k
    kk = kkr * lax.rsqrt(jnp.maximum(_headsum(kkr * kkr, hs), 1e-24))
    sh_ref[0] = r
    sh_ref[1] = kk
    sh_ref[2] = v
    gb_ref[0] = _dot3(jax.nn.sigmoid(lo[:, 4 * LANES:5 * LANES]), g2_ref[...])
    bon = jnp.zeros_like(r)
    for dd in range(2):
        w = vec[dd:dd + 1] + _dot3(jnp.tanh(lo[:, dd * LANES:(dd + 1) * LANES]), w2_ref[dd])
        z = -w
        softplus = jnp.maximum(z, 0.0) + jnp.log1p(jnp.exp(-jnp.abs(z)))
        dr_ref[dd, 0] = jnp.exp(-jnp.exp(-softplus - 0.5))
        a = jax.nn.sigmoid(vec[2 + dd:3 + dd]
                           + _dot3(lo[:, (2 + dd) * LANES:(3 + dd) * LANES], a2_ref[dd]))
        kd = k * (1.0 + (a - 1.0) * k_a)
        dr_ref[dd, 1] = kd
        dr_ref[dd, 2] = kk * a
        bon = bon + r * kd * vec[6 + dd:7 + dd]
    gb_ref[1] = _headsum(bon, hs) * v


def _mix_pre(x, mods, n_ctx_tiles, tiles_per_lat, tm, norm_mix, w_in, conv_w, lora1, w2, a2, g2, vecs, hs):
    t, d = x.shape
    dr = hs.shape[0]
    n_tiles = t // tm

    def mod_idx(i):
        return (jnp.where(i < n_ctx_tiles, 0, 1 + (i - n_ctx_tiles) // tiles_per_lat), 0, 0)

    return pl.pallas_call(
        functools.partial(_mix_pre_kernel, n_ctx_tiles),
        grid=(n_tiles,),
        in_specs=[pl.BlockSpec((tm, d), lambda i: (i, 0)),
                  pl.BlockSpec((1, 1, mods.shape[-1]), mod_idx),
                  _full(norm_mix.shape), _full(w_in.shape), _full(conv_w.shape), _full(lora1.shape),
                  _full(w2.shape), _full(a2.shape), _full(g2.shape), _full(vecs.shape), _full(hs.shape)],
        out_specs=[pl.BlockSpec((3, tm, dr), lambda i: (0, i, 0)),
                   pl.BlockSpec((2, 3, tm, dr), lambda i: (0, 0, i, 0)),
                   pl.BlockSpec((2, tm, dr), lambda i: (0, i, 0)),
                   pl.BlockSpec((3, tm, dr), lambda i: (0, i, 0))],
        out_shape=[jax.ShapeDtypeStruct((3, t, dr), F32),
                   jax.ShapeDtypeStruct((2, 3, t, dr), F32),
                   jax.ShapeDtypeStruct((2, t, dr), F32),
                   jax.ShapeDtypeStruct((3, t, dr), F32)],
        compiler_params=_params("arbitrary"),
        name="mix_pre",
    )(x, mods, norm_mix, w_in, conv_w, lora1, w2, a2, g2, vecs, hs)


def _wkv_kernel(n_fwd, r_ref, kk_ref, v_ref, d_ref, k_ref, kka_ref, s0_ref, y_ref, sfin_ref, s_scr):
    tb = r_ref.shape[1]
    ni = s_scr.shape[0]
    rev = pl.program_id(0) >= n_fwd
    tblk = pl.program_id(1)

    @pl.when(tblk == 0)
    def _():
        s_scr[...] = s0_ref[0]

    def step(tau, carry):
        t = jnp.where(rev, tb - 1 - tau, tau)
        rt = r_ref[0, t]
        kkt = kk_ref[0, t]
        dt = d_ref[0, t]
        kt = k_ref[0, t]
        kkat = kka_ref[0, t]
        drt = dt * rt
        c1 = jnp.sum(kt * rt, axis=0, keepdims=True)
        c2 = jnp.sum(kkat * rt, axis=0, keepdims=True)

        def rows8(i8, c):
            base = pl.multiple_of(i8 * SUBLANES, SUBLANES)
            vblk = v_ref[0, t, pl.ds(base, SUBLANES), :]
            outs = []
            for jj in range(SUBLANES):
                s = s_scr[base + jj]
                vi = vblk[jj:jj + 1]
                sa = jnp.sum(s * kkt, axis=0, keepdims=True)
                q = jnp.sum(s * drt, axis=0, keepdims=True)
                s_scr[base + jj] = s * dt + (vi * kt - sa * kkat)
                outs.append(q + vi * c1 - sa * c2)
            y_ref[0, t, pl.ds(base, SUBLANES), :] = jnp.concatenate(outs, axis=0)
            return c

        lax.fori_loop(0, ni // SUBLANES, rows8, 0)
        return carry

    lax.fori_loop(0, tb, step, 0)

    @pl.when(tblk == pl.num_programs(1) - 1)
    def _():
        sfin_ref[0] = s_scr[...]


def _wkv(r, kk, v, d, k, kka, s0, tb):
    gh, t, _, _ = r.shape
    g = 2 * gh
    ni = v.shape[2]
    nt = t // tb

    def tsel(gi, ti):
        return jnp.where(gi >= gh, nt - 1 - ti, ti)

    shared = pl.BlockSpec((1, tb, HEAD, LANES), lambda gi, ti: (gi % gh, tsel(gi, ti), 0, 0))
    perdir = pl.BlockSpec((1, tb, HEAD, LANES), lambda gi, ti: (gi, tsel(gi, ti), 0, 0))
    state = pl.BlockSpec((1, ni, HEAD, LANES), lambda gi, ti: (gi, 0, 0, 0))
    return pl.pallas_call(
        functools.partial(_wkv_kernel, gh),
        grid=(g, nt),
        in_specs=[shared, shared,
                  pl.BlockSpec((1, tb, ni, LANES), lambda gi, ti: (gi % gh, tsel(gi, ti), 0, 0)),
                  perdir, perdir, perdir, state],
        out_specs=[pl.BlockSpec((1, tb, ni, LANES), lambda gi, ti: (gi, tsel(gi, ti), 0, 0)), state],
        out_shape=[jax.ShapeDtypeStruct((g, t, ni, LANES), F32),
                   jax.ShapeDtypeStruct((g, ni, HEAD, LANES), F32)],
        scratch_shapes=[pltpu.VMEM((ni, HEAD, LANES), F32)],
        compiler_params=_params("arbitrary", "arbitrary"),
        name="wkv",
    )(r, kk, v, d, k, kka, s0)


def _wkv_ctx_kernel(reverse, sh_ref, pd_ref, y_ref, sfin_ref, s_scr, tr_scr, yt_scr):
    nb, tb = sh_ref.shape[1], sh_ref.shape[2]
    nhp = sh_ref.shape[3] // LANES
    tblk = pl.program_id(0)

    @pl.when(tblk == 0)
    def _():
        s_scr[...] = jnp.zeros_like(s_scr)

    def to_lanes(ref, lead, t):
        blocks = [ref[lead + (slice(None), t, slice(hp * LANES, (hp + 1) * LANES))] for hp in range(nhp)]
        return jnp.concatenate(blocks, axis=0).T

    for tau in range(tb):
        t = tb - 1 - tau if reverse else tau
        r_t = to_lanes(sh_ref, (0,), t)
        d_t = to_lanes(pd_ref, (0, 0), t)
        tr_scr[0] = d_t * r_t
        tr_scr[1] = to_lanes(sh_ref, (1,), t)
        tr_scr[2] = to_lanes(sh_ref, (2,), t)
        tr_scr[3] = d_t
        k_t = to_lanes(pd_ref, (0, 1), t)
        tr_scr[4] = k_t
        kka_t = to_lanes(pd_ref, (0, 2), t)
        tr_scr[5] = kka_t
        for h2 in range(2):
            rows = slice(h2 * HEAD, (h2 + 1) * HEAD)
            c1 = jnp.sum(k_t[rows] * r_t[rows], axis=0, keepdims=True)
            c2 = jnp.sum(kka_t[rows] * r_t[rows], axis=0, keepdims=True)

            def rows8(i8, c, h2=h2, rows=rows, c1=c1, c2=c2):
                base = pl.multiple_of(h2 * HEAD + i8 * SUBLANES, SUBLANES)
                vblk = tr_scr[2, pl.ds(base, SUBLANES), :]
                outs = []
                for jj in range(SUBLANES):
                    s = s_scr[base + jj]
                    vi = vblk[jj:jj + 1]
                    sa = jnp.sum(s * tr_scr[1, rows, :], axis=0, keepdims=True)
                    q = jnp.sum(s * tr_scr[0, rows, :], axis=0, keepdims=True)
                    s_scr[base + jj] = s * tr_scr[3, rows, :] + (vi * tr_scr[4, rows, :] - sa * tr_scr[5, rows, :])
                    outs.append(q + vi * c1 - sa * c2)
                yt_scr[pl.ds(base, SUBLANES), :] = jnp.concatenate(outs, axis=0)
                return c

            lax.fori_loop(0, HEAD // SUBLANES, rows8, 0)
        y = yt_scr[...].T
        for hp in range(nhp):
            y_ref[:, t, hp * LANES:(hp + 1) * LANES] = y[hp * nb:(hp + 1) * nb]

    @pl.when(tblk == pl.num_programs(0) - 1)
    def _():
        sfin_ref[...] = s_scr[...]


def _wkv_ctx(shared, perdir, nb, tb, reverse):
    _, nseq, lc, dr = shared.shape
    assert nb * (dr // LANES) == LANES and 2 * HEAD == LANES
    nt = lc // tb
    di = 1 if reverse else 0

    def tsel(ti):
        return nt - 1 - ti if reverse else ti

    return pl.pallas_call(
        functools.partial(_wkv_ctx_kernel, reverse),
        grid=(nt,),
        in_specs=[pl.BlockSpec((3, nb, tb, dr), lambda ti: (0, 0, tsel(ti), 0)),
                  pl.BlockSpec((1, 3, nb, tb, dr), lambda ti: (di, 0, 0, tsel(ti), 0))],
        out_specs=[pl.BlockSpec((nb, tb, dr), lambda ti: (0, tsel(ti), 0)),
                   pl.BlockSpec((LANES, HEAD, LANES), lambda ti: (0, 0, 0))],
        out_shape=[jax.ShapeDtypeStruct((nb, lc, dr), F32),
                   jax.ShapeDtypeStruct((LANES, HEAD, LANES), F32)],
        scratch_shapes=[pltpu.VMEM((LANES, HEAD, LANES), F32),
                        pltpu.VMEM((6, LANES, LANES), F32),
                        pltpu.VMEM((LANES, LANES), F32)],
        compiler_params=_params("arbitrary"),
        name="wkv_ctx",
    )(shared, perdir)


def _dft_mats(l):
    n = 2 * l
    nfp = l + SUBLANES
    kf = np.arange(nfp, dtype=np.int64)
    ang = (2.0 * np.pi / n) * ((kf[:, None] * np.arange(l, dtype=np.int64)[None, :]) % n)
    valid = (kf <= l)[:, None]
    c = np.where(valid, np.cos(ang), 0.0)
    s = np.where(valid, np.sin(ang), 0.0)
    wk = np.where((kf == 0) | (kf == l), 1.0 / n, 2.0 / n)[None, :]
    mats = (c, s, c.T * wk, -(s.T) * wk)
    return [m for mat in mats for m in _split(jnp.asarray(mat.astype(np.float32)))]


def _hy_filter_kernel(zf_ref, w1_ref, b1_ref, w2_ref, b2_ref, fq_ref, w3_ref, dl_ref, bias_ref,
                      ch_ref, cl_ref, sh_ref, sl_ref, g_ref):
    zf = zf_ref[...]
    hid = jnp.sin(fq_ref[0:1] * (_dot3(zf, w1_ref[...]) + b1_ref[...]))
    hid = jnp.sin(fq_ref[1:2] * (_dot3(hid, w2_ref[...]) + b2_ref[...]))
    win = jnp.exp(-zf[:, 0:1] * jnp.abs(dl_ref[...]))
    hh, hl = _split(hid)
    filt = []
    for o in range(4):
        wh, wl = _split(w3_ref[o])
        filt.append(_dot_split(hh, hl, wh, wl) * win)
    for od in range(2):
        hf, hbk = filt[2 * od], filt[2 * od + 1]
        ph, plo = _split(hf + hbk)
        mh, ml = _split(hbk - hf)
        g_ref[od, 0] = _dot_split(ch_ref[...], cl_ref[...], ph, plo) + bias_ref[od:od + 1]
        g_ref[od, 1] = _dot_split(sh_ref[...], sl_ref[...], mh, ml)


def _hy_filter(zf, w1, b1, w2, b2, fq, w3, deltas, bias, mats):
    l = zf.shape[0]
    nfp = mats[0].shape[0]
    c = w3.shape[-1]
    cb = LANES
    return pl.pallas_call(
        _hy_filter_kernel,
        grid=(c // cb,),
        in_specs=[_full(zf.shape), _full(w1.shape), _full(b1.shape), _full(w2.shape), _full(b2.shape),
                  _full(fq.shape),
                  pl.BlockSpec((4, w3.shape[1], cb), lambda j: (0, 0, j)),
                  pl.BlockSpec((1, cb), lambda j: (0, j)),
                  pl.BlockSpec((2, cb), lambda j: (0, j)),
                  _const((nfp, l)), _const((nfp, l)), _const((nfp, l)), _const((nfp, l))],
        out_specs=pl.BlockSpec((2, 2, nfp, cb), lambda j: (0, 0, 0, j)),
        out_shape=jax.ShapeDtypeStruct((2, 2, nfp, c), F32),
        compiler_params=_params("arbitrary"),
        name="hy_filt",
    )(zf, w1, b1, w2, b2, fq, w3, deltas, bias, *mats[:4])


def _hy_conv_kernel(hy_ref, g_ref, ch_ref, cl_ref, sh_ref, sl_ref, cih_ref, cil_ref, sih_ref, sil_ref, o_ref):
    def conv(u, od):
        uh, ul = _split(u)
        re = _dot_split(ch_ref[...], cl_ref[...], uh, ul)
        im = -_dot_split(sh_ref[...], sl_ref[...], uh, ul)
        gre = g_ref[od, 0]
        gim = g_ref[od, 1]
        ah, al = _split(re * gre - im * gim)
        bh, bl = _split(re * gim + im * gre)
        return (_dot_split(cih_ref[...], cil_ref[...], ah, al)
                + _dot_split(sih_ref[...], sil_ref[...], bh, bl))

    z = hy_ref[1] * conv(hy_ref[0], 0)
    o_ref[...] = hy_ref[2] * conv(z, 1)


def _hy_conv(hy, g, mats, l, first_seq, n_seq):
    t, c = hy.shape[1], hy.shape[2]
    nfp = mats[0].shape[0]
    cb = 2 * LANES
    return pl.pallas_call(
        _hy_conv_kernel,
        grid=(n_seq, c // cb),
        in_specs=[pl.BlockSpec((3, l, cb), lambda b, j: (0, first_seq + b, j)),
                  pl.BlockSpec((2, 2, nfp, cb), lambda b, j: (0, 0, 0, j)),
                  _const((nfp, l)), _const((nfp, l)), _const((nfp, l)), _const((nfp, l)),
                  _const((l, nfp)), _const((l, nfp)), _const((l, nfp)), _const((l, nfp))],
        out_specs=pl.BlockSpec((l, cb), lambda b, j: (b, j)),
        out_shape=jax.ShapeDtypeStruct((n_seq * l, c), F32),
        compiler_params=_params("arbitrary", "arbitrary"),
        name="hy_conv",
    )(hy, g, *mats)


def _mix_post_kernel(y_ref, gb_ref, z_ref, x_ref, mod_ref, vec_ref, wout_ref, nf_ref, hs_ref, x1_ref, h2_ref):
    d = x_ref.shape[1]
    dr = hs_ref.shape[0]
    hs = hs_ref[...]
    vec = vec_ref[...]
    mod = mod_ref[0]
    gate1 = mod[:, 2 * d:3 * d]
    shift2 = mod[:, 3 * d:4 * d]
    scale2 = mod[:, 4 * d:5 * d]
    y = y_ref[...]
    mu = _headsum(y, hs) * (1.0 / HEAD)
    yc = y - mu
    var = _headsum(yc * yc, hs) * (1.0 / HEAD)
    yn = yc * lax.rsqrt(var + GN_EPS) * vec[0:1] + vec[1:2]
    ya = (yn + gb_ref[1]) * gb_ref[0]
    z = z_ref[...]
    yb = z * lax.rsqrt(jnp.mean(z * z, axis=-1, keepdims=True) + EPS) * vec[2:3]
    m = _dot(ya.astype(BF16), wout_ref[0:dr]) + _dot(yb.astype(BF16), wout_ref[dr:])
    x1 = x_ref[...] + gate1 * m
    x1_ref[...] = x1
    xn = x1 * lax.rsqrt(jnp.mean(x1 * x1, axis=-1, keepdims=True) + EPS) * nf_ref[...]
    h2_ref[...] = xn * (1.0 + scale2) + shift2


def _mix_post(ysum, gb, z, x, mods, n_ctx_tiles, tiles_per_lat, tm, vecs, w_out, norm_ffn, hs):
    t, d = x.shape
    dr = hs.shape[0]

    def mod_idx(i):
        return (jnp.where(i < n_ctx_tiles, 0, 1 + (i - n_ctx_tiles) // tiles_per_lat), 0, 0)

    tok = pl.BlockSpec((tm, dr), lambda i: (i, 0))
    tokd = pl.BlockSpec((tm, d), lambda i: (i, 0))
    return pl.pallas_call(
        _mix_post_kernel,
        grid=(t // tm,),
        in_specs=[tok, pl.BlockSpec((2, tm, dr), lambda i: (0, i, 0)), tok, tokd,
                  pl.BlockSpec((1, 1, mods.shape[-1]), mod_idx),
                  _full(vecs.shape), _full(w_out.shape), _full(norm_ffn.shape), _full(hs.shape)],
        out_specs=[tokd, tokd],
        out_shape=[jax.ShapeDtypeStruct((t, d), F32), jax.ShapeDtypeStruct((t, d), F32)],
        compiler_params=_params("arbitrary"),
        name="mix_post",
    )(ysum, gb, z, x, mods, vecs, w_out, norm_ffn, hs)


def _second_level_pairs():
    return [(a, b) for a in range(TOPK) for b in range(TOPK) if (a + 1) * (b + 1) <= TOPK]


def _route_kernel(h2_ref, wq_ref, keys_ref, r2_ref, b1_ref, e1_ref, e2_ref, sv_scr, s_scr, rk_scr):
    nh = keys_ref.shape[1]
    nk = keys_ref.shape[2]
    half = keys_ref.shape[3]
    tt = h2_ref.shape[0]
    q = _dot(h2_ref[...].astype(BF16), wq_ref[...])
    key_id = lax.broadcasted_iota(jnp.int32, (nk, tt), 0).astype(F32)
    neg = jnp.float32(-jnp.inf)

    for h in range(nh):
        for p in range(2):
            c0 = (h * 2 + p) * half
            qh, ql = _split(q[:, c0:c0 + half])
            kh, kl = _split(keys_ref[p, h])
            dims = (((1,), (1,)), ((), ()))
            s = (lax.dot_general(kh, qh, dims, preferred_element_type=F32)
                 + (lax.dot_general(kl, qh, dims, preferred_element_type=F32)
                    + lax.dot_general(kh, ql, dims, preferred_element_type=F32)))
            s_scr[p, h] = s

            def extract(r, carry, p=p, h=h):
                x, rank = carry
                m = jnp.max(x, axis=0, keepdims=True)
                first = jnp.min(jnp.where(x == m, key_id, float(nk)), axis=0, keepdims=True)
                sel = key_id == first
                sv_scr[p, r, h:h + 1, :] = m
                return jnp.where(sel, neg, x), jnp.where(sel, lax.convert_element_type(r, F32), rank)

            _, rank = lax.fori_loop(0, TOPK, extract, (s, jnp.full((nk, tt), float(TOPK), F32)))
            rk_scr[p, h] = rank

    v1 = [sv_scr[0, a] for a in range(TOPK)]
    v2 = [sv_scr[1, b] for b in range(TOPK)]
    pairs = _second_level_pairs()
    top = v1[0] + v2[0]

    def select(_, carry):
        cand, cnt, zsum = list(carry[0]), list(carry[1]), carry[2]
        m = cand[0]
        for cv in cand[1:]:
            m = jnp.maximum(m, cv)
        zsum = zsum + jnp.exp(m - top)
        found = jnp.zeros((nh, tt), F32)
        for ci, (a, _b) in enumerate(pairs):
            eq = jnp.where(cand[ci] == m, 1.0, 0.0)
            hit = eq * (1.0 - found)
            found = jnp.maximum(found, eq)
            cand[ci] = jnp.where(hit > 0.0, neg, cand[ci])
            cnt[a] = cnt[a] + hit
        return tuple(cand), tuple(cnt), zsum

    zero = jnp.zeros((nh, tt), F32)
    _, cnt, zsum = lax.fori_loop(0, TOPK, select,
                                 (tuple(v1[a] + v2[b] for a, b in pairs), (zero,) * TOPK, zero))
    zinv = 1.0 / zsum

    for h in range(nh):
        rank1 = rk_scr[0, h]
        b1 = jnp.zeros((nk, tt), F32)
        for a in range(TOPK):
            b1 = b1 + jnp.where(rank1 == float(a), cnt[a][h:h + 1], 0.0)
        b1_ref[h] = b1
        r2_ref[h] = rk_scr[1, h].astype(BF16)
        e1_ref[h] = jnp.exp(s_scr[0, h] - v1[0][h:h + 1]) * zinv[h:h + 1]
        e2_ref[h] = jnp.exp(s_scr[1, h] - v2[0][h:h + 1]).astype(BF16)


def _route(h2, wq, keys, tt):
    t, d = h2.shape
    _, nh, nk, _ = keys.shape
    out = jax.ShapeDtypeStruct((nh, nk, t), F32)
    out16 = jax.ShapeDtypeStruct((nh, nk, t), BF16)
    ospec = pl.BlockSpec((nh, nk, tt), lambda i: (0, 0, i))
    return pl.pallas_call(
        _route_kernel,
        grid=(t // tt,),
        in_specs=[pl.BlockSpec((tt, d), lambda i: (i, 0)), _full(wq.shape), _full(keys.shape)],
        out_specs=[ospec, ospec, ospec, ospec],
        out_shape=[out16, out, out, out16],
        scratch_shapes=[pltpu.VMEM((2, TOPK, nh, tt), F32),
                        pltpu.VMEM((2, nh, nk, tt), F32),
                        pltpu.VMEM((2, nh, nk, tt), F32)],
        compiler_params=_params("arbitrary"),
        name="route",
    )(h2, wq, keys)


def _peer_kernel(h2t_ref, u_ref, vt_ref, r2_ref, e2_ref, b1_ref, e1_ref, x1_ref, mod_ref, fn_ref,
                 o_ref, acc_scr, a_scr):
    d = x1_ref.shape[1]
    nh, nk, tt = r2_ref.shape
    n1 = b1_ref.shape[1]
    eb = pl.program_id(1)

    @pl.when(eb == 0)
    def _():
        acc_scr[...] = jnp.zeros_like(acc_scr)

    hid = _dot(u_ref[...], h2t_ref[...])
    pk = 2 * SUBLANES
    for i in range(n1):
        w = jnp.zeros((nk // pk, pk, tt), BF16)
        for h in range(nh):
            b1r = jnp.broadcast_to(b1_ref[h, i:i + 1, :], (pk, tt)).astype(BF16)[None]
            e1r = jnp.broadcast_to(e1_ref[h, i:i + 1, :], (pk, tt)).astype(BF16)[None]
            r2 = r2_ref[h].reshape(nk // pk, pk, tt)
            e2 = e2_ref[h].reshape(nk // pk, pk, tt)
            w = w + jnp.where(r2 < b1r, e1r * e2, jnp.zeros((), BF16))
        hi = hid[i * nk:(i + 1) * nk]
        act = 0.5 * hi * (1.0 + lax.erf(hi * (1.0 / math.sqrt(2.0))))
        a_scr[i * nk:(i + 1) * nk, :] = (act.astype(BF16).reshape(nk // pk, pk, tt) * w).reshape(nk, tt)
    acc_scr[...] += _dot(vt_ref[...], a_scr[...])

    @pl.when(eb == pl.num_programs(1) - 1)
    def _():
        gate2 = mod_ref[0][:, 5 * d:6 * d]
        xo = x1_ref[...] + gate2 * acc_scr[...].T
        o_ref[...] = xo * lax.rsqrt(jnp.mean(xo * xo, axis=-1, keepdims=True) + EPS) * fn_ref[...]


def _peer(h2t, u, vt, r2, e2, b1, e1, x1, mods, final_norm, n_ctx_tiles, tiles_per_lat, tt, n1):
    d, t = h2t.shape
    ne = u.shape[0]
    nh, nk, _ = r2.shape
    eb = n1 * nk

    def mod_idx(i, e):
        return (jnp.where(i < n_ctx_tiles, 0, 1 + (i - n_ctx_tiles) // tiles_per_lat), 0, 0)

    dense = pl.BlockSpec((nh, nk, tt), lambda i, e: (0, 0, i))
    rows = pl.BlockSpec((nh, n1, tt), lambda i, e: (0, e, i))
    return pl.pallas_call(
        _peer_kernel,
        grid=(t // tt, ne // eb),
        in_specs=[pl.BlockSpec((d, tt), lambda i, e: (0, i)),
                  pl.BlockSpec((eb, d), lambda i, e: (e, 0)),
                  pl.BlockSpec((d, eb), lambda i, e: (0, e)),
                  dense, dense, rows, rows,
                  pl.BlockSpec((tt, d), lambda i, e: (i, 0)),
                  pl.BlockSpec((1, 1, mods.shape[-1]), mod_idx),
                  _full(final_norm.shape)],
        out_specs=pl.BlockSpec((tt, d), lambda i, e: (i, 0)),
        out_shape=jax.ShapeDtypeStruct((t, d), F32),
        scratch_shapes=[pltpu.VMEM((d, tt), F32), pltpu.VMEM((eb, tt), BF16)],
        compiler_params=_params("arbitrary", "arbitrary"),
        name="peer",
    )(h2t, u, vt, r2, e2, b1, e1, x1, mods, final_norm)


def _hyena_features(l, bands):
    t = jnp.linspace(0.0, 1.0, l, dtype=F32)[:, None]
    wpos = 2.0 * math.pi * jnp.arange(l, dtype=F32)[:, None] / l
    f = jnp.linspace(1e-4, bands - 1, bands, dtype=F32)[None, :]
    return jnp.concatenate([t, jnp.cos(f * wpos), -jnp.sin(f * wpos)], axis=-1)


def kernel(x_prompt, x_sample, state_rwkv, c, c_ctx, w_ada, b_ada, norm_mix, norm_ffn, w_in, conv_w, w_out, rwkv_w0, rwkv_w1, rwkv_w2, rwkv_a0, rwkv_a1, rwkv_a2, rwkv_g1, rwkv_g2, rwkv_k_k, rwkv_k_a, rwkv_r_k, rwkv_ln_w, rwkv_ln_b, hy_f_w1, hy_f_b1, hy_f_w2, hy_f_b2, hy_freq, hy_f_w3, hy_bias, hy_norm, peer_wq, peer_keys, peer_u, peer_v, final_norm):
    bc, lc, d = x_prompt.shape
    bl, ll, _ = x_sample.shape
    depth = w_ada.shape[0]
    assert depth == 1
    dr = rwkv_w0.shape[-1]
    dh = hy_norm.shape[-1]
    nh = dr // HEAD
    assert dr == dh and w_in.shape[-1] == 3 * dr + 3 * dh
    tc, tl = bc * lc, bl * ll
    t = tc + tl
    tm = lc
    assert ll % tm == 0 and tm % GRID_W == 0 and tc % ll == 0
    n_ctx_tiles = tc // tm
    tiles_per_lat = ll // tm
    parts = LANES // (bl * nh)
    assert parts * bl * nh == LANES and HEAD % (parts * SUBLANES) == 0
    nil = HEAD // parts
    l = 0

    rows = jnp.concatenate([c_ctx[None, :], c, jnp.zeros((SUBLANES - 1 - bl, d), F32)], axis=0)
    mods = _ada(rows, w_ada[l], b_ada[l]).reshape(SUBLANES, 1, 6 * d)

    x = jnp.concatenate([x_prompt.reshape(tc, d), x_sample.reshape(tl, d)], axis=0)

    def pad_cols(w):
        return jnp.pad(w, ((0, 0), (0, LANES - w.shape[1])))

    def pad_rows(w):
        return jnp.pad(w, ((0, LANES - w.shape[0]), (0, 0)))

    lora1 = jnp.concatenate([pad_cols(rwkv_w1[l, 0]), pad_cols(rwkv_w1[l, 1]), pad_cols(rwkv_a1[l, 0]),
                             pad_cols(rwkv_a1[l, 1]), pad_cols(rwkv_g1[l])], axis=1).astype(BF16)
    w2 = jnp.stack([pad_rows(rwkv_w2[l, 0]), pad_rows(rwkv_w2[l, 1])])
    a2 = jnp.stack([pad_rows(rwkv_a2[l, 0]), pad_rows(rwkv_a2[l, 1])])
    vecs_pre = jnp.stack([rwkv_w0[l, 0], rwkv_w0[l, 1], rwkv_a0[l, 0], rwkv_a0[l, 1], rwkv_k_k[l], rwkv_k_a[l],
                          rwkv_r_k[l, 0].reshape(dr), rwkv_r_k[l, 1].reshape(dr)])
    head_id = jnp.arange(dr, dtype=jnp.int32) // HEAD
    hs = (head_id[:, None] == head_id[None, :]).astype(BF16)
    shared, perdir, gb, hy = _mix_pre(x, mods, n_ctx_tiles, tiles_per_lat, tm, norm_mix[l][None, :],
                                      w_in[l].astype(BF16), conv_w[l], lora1, w2, a2, rwkv_g2[l], vecs_pre, hs)

    def lat_lanes(a):
        a = a.reshape(bl, ll, nh, HEAD).transpose(1, 3, 0, 2)
        return jnp.broadcast_to(a[..., None], (ll, HEAD, bl, nh, parts)).reshape(1, ll, HEAD, LANES)

    def lat_rows(a):
        a = a.reshape(bl, ll, nh, parts, nil).transpose(1, 4, 0, 2, 3)
        return a.reshape(1, ll, nil, LANES)

    sh_l = shared[:, tc:]
    pd_l = perdir[:, :, tc:]
    shared4 = shared.reshape(3, t // lc, lc, dr)
    perdir4 = perdir.reshape(2, 3, t // lc, lc, dr)
    y_cf, s_cf = _wkv_ctx(shared4, perdir4, bc, 16, False)
    y_cb, s_cb = _wkv_ctx(shared4, perdir4, bc, 16, True)
    lat_dir = [jnp.concatenate([lat_lanes(pd_l[0, j]), lat_lanes(pd_l[1, j])], axis=0) for j in range(3)]
    s0 = state_rwkv[:, l].reshape(bl, 2, nh, parts, nil, HEAD).transpose(1, 4, 5, 0, 2, 3)
    y_lat, _ = _wkv(lat_lanes(sh_l[0]), lat_lanes(sh_l[1]), lat_rows(sh_l[2]), *lat_dir,
                    s0.reshape(2, nil, HEAD, LANES), tb=32)
    ys_ctx = (y_cf + y_cb).reshape(tc, dr)
    ys_lat = (y_lat[0] + y_lat[1]).reshape(ll, nil, bl, nh, parts).transpose(2, 0, 3, 4, 1).reshape(tl, dr)
    ysum = jnp.concatenate([ys_ctx, ys_lat], axis=0)
    new_state = jnp.stack([s_cf, s_cb]).reshape(2, 2, HEAD, HEAD, nh // 2, bc).transpose(5, 0, 4, 1, 2, 3)
    new_state = new_state.reshape(bc, 1, 2, nh, HEAD, HEAD)

    bands = (hy_f_w1.shape[1] - 1) // 2
    femb = hy_f_w1.shape[1]
    fpad = 32
    w1p = jnp.pad(hy_f_w1[l], ((0, fpad - femb), (0, 0)))
    hidden = hy_f_w2.shape[-1]
    w3 = hy_f_w3[l].reshape(hidden, 4, dh).transpose(1, 0, 2)
    max_decay = math.log(1e-2) / 0.3
    min_decay = math.log(1e-2) / 1.5
    deltas = jnp.linspace(min_decay, max_decay, dh, dtype=F32)[None, :]
    z_parts = []
    for seq_len, first_seq, n_seq in ((lc, 0, bc), (ll, tc // ll, bl)):
        mats = _dft_mats(seq_len)
        zf = jnp.pad(_hyena_features(seq_len, bands), ((0, 0), (0, fpad - femb)))
        g = _hy_filter(zf, w1p, hy_f_b1[l][None, :], hy_f_w2[l], hy_f_b2[l][None, :], hy_freq[l], w3,
                       deltas, hy_bias[l], mats)
        z_parts.append(_hy_conv(hy, g, mats, seq_len, first_seq, n_seq))
    zhy = jnp.concatenate(z_parts, axis=0)

    vecs_post = jnp.concatenate([jnp.stack([rwkv_ln_w[l], rwkv_ln_b[l], hy_norm[l]]),
                                 jnp.zeros((SUBLANES - 3, dr), F32)], axis=0)
    x1, h2 = _mix_post(ysum, gb, zhy, x, mods, n_ctx_tiles, tiles_per_lat, tm, vecs_post,
                       w_out[l].astype(BF16), norm_ffn[l][None, :], hs)

    r2, b1, e1, e2 = _route(h2, peer_wq[l].astype(BF16), peer_keys[l], tt=256)
    tt = 512
    assert ll % tt == 0 and tc % tt == 0
    out = _peer(h2.astype(BF16).T, peer_u[l].astype(BF16), peer_v[l].astype(BF16).T, r2, e2, b1, e1, x1, mods,
                final_norm[None, :], tc // tt, ll // tt, tt, n1=SUBLANES)
    return out[:tc].reshape(bc, lc, d), out[tc:].reshape(bl, ll, d), new_state
```

```python
import functools
import math

import jax
import jax.numpy as jnp
import numpy as np
from jax import lax
from jax.experimental import pallas as pl
from jax.experimental.pallas import tpu as pltpu

F32 = jnp.float32
BF16 = jnp.bfloat16

GRID_W = 64
HEAD = 64
TOPK = 16
EPS = 1e-6
GN_EPS = 64e-5
LANES = 128
SUBLANES = 8
VMEM_LIMIT = 56 * 1024 * 1024


def _params(*sem):
    return pltpu.CompilerParams(dimension_semantics=sem, vmem_limit_bytes=VMEM_LIMIT)


def _split(x):
    hi = x.astype(BF16)
    lo = (x - hi.astype(F32)).astype(BF16)
    return hi, lo


def _dot(a, b):
    return jnp.dot(a, b, preferred_element_type=F32)


def _dot_split(ah, al, bh, bl):
    return _dot(ah, bh) + (_dot(al, bh) + _dot(ah, bl))


def _dot3(a, b):
    ah, al = _split(a)
    bh, bl = _split(b)
    return _dot_split(ah, al, bh, bl)


def _full(shape):
    n = len(shape)
    return pl.BlockSpec(shape, lambda *_: (0,) * n)


def _const(shape):
    n = len(shape)
    return pl.BlockSpec(shape, lambda *_: (0,) * n, pipeline_mode=pl.Buffered(1))


def _ada_kernel(c_ref, w_ref, b_ref, o_ref):
    c = c_ref[...]
    s = c * jax.nn.sigmoid(c)
    o_ref[...] = _dot3(s, w_ref[...]) + b_ref[...]


def _ada(c_rows, w, b):
    rows, d = c_rows.shape
    n = w.shape[1]
    bn = n // 4
    return pl.pallas_call(
        _ada_kernel,
        grid=(4,),
        in_specs=[_full((rows, d)),
                  pl.BlockSpec((d, bn), lambda j: (0, j)),
                  pl.BlockSpec((1, bn), lambda j: (0, j))],
        out_specs=pl.BlockSpec((rows, bn), lambda j: (0, j)),
        out_shape=jax.ShapeDtypeStruct((rows, n), F32),
        compiler_params=_params("arbitrary"),
        name="ada",
    )(c_rows, w, b.reshape(1, n))


def _headsum(z, hs):
    zh, zl = _split(z)
    return _dot(zh, hs) + _dot(zl, hs)


def _mix_pre_kernel(n_ctx_tiles, x_ref, mod_ref, nm_ref, win_ref, cw_ref, l1_ref, w2_ref, a2_ref,
                    g2_ref, vec_ref, hs_ref, sh_ref, dr_ref, gb_ref, hy_ref):
    tm, d = x_ref.shape
    dr = hs_ref.shape[0]
    x = x_ref[...]
    mod = mod_ref[0]
    shift1 = mod[:, 0:d]
    scale1 = mod[:, d:2 * d]
    xn = x * lax.rsqrt(jnp.mean(x * x, axis=-1, keepdims=True) + EPS) * nm_ref[...]
    hb = (xn * (1.0 + scale1) + shift1).astype(BF16)

    pos = lax.broadcasted_iota(jnp.int32, (tm, dr), 0)
    is_ctx = pl.program_id(0) < n_ctx_tiles
    row = pos % GRID_W
    keep_prev = jnp.where(is_ctx, jnp.where(pos == 0, 0.0, 1.0), jnp.where(row == 0, 0.0, 1.0))
    keep_next = jnp.where(is_ctx, jnp.where(pos == tm - 1, 0.0, 1.0),
                          jnp.where(row == GRID_W - 1, 0.0, 1.0))

    def proj(c):
        u = _dot(hb, win_ref[:, c * dr:(c + 1) * dr])
        cw = cw_ref[:, c * dr:(c + 1) * dr]
        up = pltpu.roll(u, 1, 0) * keep_prev
        un = pltpu.roll(u, tm - 1, 0) * keep_next
        return up * cw[0:1] + u * cw[1:2] + un * cw[2:3]

    r = proj(0)
    k = proj(1)
    v = proj(2)
    hy_ref[0] = proj(3)
    hy_ref[1] = proj(4)
    hy_ref[2] = proj(5)

    lo = _dot(hb, l1_ref[...])
    vec = vec_ref[...]
    k_k = vec[4:5]
    k_a = vec[5:6]
    hs = hs_ref[...]
    kkr = k * k_k
    kk = kkr * lax.rsqrt(jnp.maximum(_headsum(kkr * kkr, hs), 1e-24))
    sh_ref[0] = r
    sh_ref[1] = kk
    sh_ref[2] = v
    gb_ref[0] = _dot3(jax.nn.sigmoid(lo[:, 4 * LANES:5 * LANES]), g2_ref[...])
    bon = jnp.zeros_like(r)
    for dd in range(2):
        w = vec[dd:dd + 1] + _dot3(jnp.tanh(lo[:, dd * LANES:(dd + 1) * LANES]), w2_ref[dd])
        z = -w
        softplus = jnp.maximum(z, 0.0) + jnp.log1p(jnp.exp(-jnp.abs(z)))
        dr_ref[dd, 0] = jnp.exp(-jnp.exp(-softplus - 0.5))
        a = jax.nn.sigmoid(vec[2 + dd:3 + dd]
                           + _dot3(lo[:, (2 + dd) * LANES:(3 + dd) * LANES], a2_ref[dd]))
        kd = k * (1.0 + (a - 1.0) * k_a)
        dr_ref[dd, 1] = kd
        dr_ref[dd, 2] = kk * a
        bon = bon + r * kd * vec[6 + dd:7 + dd]
    gb_ref[1] = _headsum(bon, hs) * v


def _mix_pre(x, mods, n_ctx_tiles, tiles_per_lat, tm, norm_mix, w_in, conv_w, lora1, w2, a2, g2, vecs, hs):
    t, d = x.shape
    dr = hs.shape[0]
    n_tiles = t // tm

    def mod_idx(i):
        return (jnp.where(i < n_ctx_tiles, 0, 1 + (i - n_ctx_tiles) // tiles_per_lat), 0, 0)

    return pl.pallas_call(
        functools.partial(_mix_pre_kernel, n_ctx_tiles),
        grid=(n_tiles,),
        in_specs=[pl.BlockSpec((tm, d), lambda i: (i, 0)),
                  pl.BlockSpec((1, 1, mods.shape[-1]), mod_idx),
                  _full(norm_mix.shape), _full(w_in.shape), _full(conv_w.shape), _full(lora1.shape),
                  _full(w2.shape), _full(a2.shape), _full(g2.shape), _full(vecs.shape), _full(hs.shape)],
        out_specs=[pl.BlockSpec((3, tm, dr), lambda i: (0, i, 0)),
                   pl.BlockSpec((2, 3, tm, dr), lambda i: (0, 0, i, 0)),
                   pl.BlockSpec((2, tm, dr), lambda i: (0, i, 0)),
                   pl.BlockSpec((3, tm, dr), lambda i: (0, i, 0))],
        out_shape=[jax.ShapeDtypeStruct((3, t, dr), F32),
                   jax.ShapeDtypeStruct((2, 3, t, dr), F32),
                   jax.ShapeDtypeStruct((2, t, dr), F32),
                   jax.ShapeDtypeStruct((3, t, dr), F32)],
        compiler_params=_params("arbitrary"),
        name="mix_pre",
    )(x, mods, norm_mix, w_in, conv_w, lora1, w2, a2, g2, vecs, hs)


def _wkv_kernel(n_fwd, r_ref, kk_ref, v_ref, d_ref, k_ref, kka_ref, s0_ref, y_ref, sfin_ref, s_scr):
    tb = r_ref.shape[1]
    ni = s_scr.shape[0]
    rev = pl.program_id(0) >= n_fwd
    tblk = pl.program_id(1)

    @pl.when(tblk == 0)
    def _():
        s_scr[...] = s0_ref[0]

    def step(tau, carry):
        t = jnp.where(rev, tb - 1 - tau, tau)
        rt = r_ref[0, t]
        kkt = kk_ref[0, t]
        dt = d_ref[0, t]
        kt = k_ref[0, t]
        kkat = kka_ref[0, t]
        drt = dt * rt
        c1 = jnp.sum(kt * rt, axis=0, keepdims=True)
        c2 = jnp.sum(kkat * rt, axis=0, keepdims=True)

        def rows8(i8, c):
            base = pl.multiple_of(i8 * SUBLANES, SUBLANES)
            vblk = v_ref[0, t, pl.ds(base, SUBLANES), :]
            outs = []
            for jj in range(SUBLANES):
                s = s_scr[base + jj]
                vi = vblk[jj:jj + 1]
                sa = jnp.sum(s * kkt, axis=0, keepdims=True)
                q = jnp.sum(s * drt, axis=0, keepdims=True)
                s_scr[base + jj] = s * dt + (vi * kt - sa * kkat)
                outs.append(q + vi * c1 - sa * c2)
            y_ref[0, t, pl.ds(base, SUBLANES), :] = jnp.concatenate(outs, axis=0)
            return c

        lax.fori_loop(0, ni // SUBLANES, rows8, 0)
        return carry

    lax.fori_loop(0, tb, step, 0)

    @pl.when(tblk == pl.num_programs(1) - 1)
    def _():
        sfin_ref[0] = s_scr[...]


def _wkv(r, kk, v, d, k, kka, s0, tb):
    gh, t, _, _ = r.shape
    g = 2 * gh
    ni = v.shape[2]
    nt = t // tb

    def tsel(gi, ti):
        return jnp.where(gi >= gh, nt - 1 - ti, ti)

    shared = pl.BlockSpec((1, tb, HEAD, LANES), lambda gi, ti: (gi % gh, tsel(gi, ti), 0, 0))
    perdir = pl.BlockSpec((1, tb, HEAD, LANES), lambda gi, ti: (gi, tsel(gi, ti), 0, 0))
    state = pl.BlockSpec((1, ni, HEAD, LANES), lambda gi, ti: (gi, 0, 0, 0))
    return pl.pallas_call(
        functools.partial(_wkv_kernel, gh),
        grid=(g, nt),
        in_specs=[shared, shared,
                  pl.BlockSpec((1, tb, ni, LANES), lambda gi, ti: (gi % gh, tsel(gi, ti), 0, 0)),
                  perdir, perdir, perdir, state],
        out_specs=[pl.BlockSpec((1, tb, ni, LANES), lambda gi, ti: (gi, tsel(gi, ti), 0, 0)), state],
        out_shape=[jax.ShapeDtypeStruct((g, t, ni, LANES), F32),
                   jax.ShapeDtypeStruct((g, ni, HEAD, LANES), F32)],
        scratch_shapes=[pltpu.VMEM((ni, HEAD, LANES), F32)],
        compiler_params=_params("arbitrary", "arbitrary"),
        name="wkv",
    )(r, kk, v, d, k, kka, s0)


def _wkv_ctx_kernel(reverse, sh_ref, pd_ref, y_ref, sfin_ref, s_scr, tr_scr, yt_scr):
    nb, tb = sh_ref.shape[1], sh_ref.shape[2]
    nhp = sh_ref.shape[3] // LANES
    tblk = pl.program_id(0)

    @pl.when(tblk == 0)
    def _():
        s_scr[...] = jnp.zeros_like(s_scr)

    def to_lanes(ref, lead, t):
        blocks = [ref[lead + (slice(None), t, slice(hp * LANES, (hp + 1) * LANES))] for hp in range(nhp)]
        return jnp.concatenate(blocks, axis=0).T

    for tau in range(tb):
        t = tb - 1 - tau if reverse else tau
        r_t = to_lanes(sh_ref, (0,), t)
        d_t = to_lanes(pd_ref, (0, 0), t)
        tr_scr[0] = d_t * r_t
        tr_scr[1] = to_lanes(sh_ref, (1,), t)
        tr_scr[2] = to_lanes(sh_ref, (2,), t)
        tr_scr[3] = d_t
        k_t = to_lanes(pd_ref, (0, 1), t)
        tr_scr[4] = k_t
        kka_t = to_lanes(pd_ref, (0, 2), t)
        tr_scr[5] = kka_t
        for h2 in range(2):
            rows = slice(h2 * HEAD, (h2 + 1) * HEAD)
            c1 = jnp.sum(k_t[rows] * r_t[rows], axis=0, keepdims=True)
            c2 = jnp.sum(kka_t[rows] * r_t[rows], axis=0, keepdims=True)

            def rows8(i8, c, h2=h2, rows=rows, c1=c1, c2=c2):
                base = pl.multiple_of(h2 * HEAD + i8 * SUBLANES, SUBLANES)
                vblk = tr_scr[2, pl.ds(base, SUBLANES), :]
                outs = []
                for jj in range(SUBLANES):
                    s = s_scr[base + jj]
                    vi = vblk[jj:jj + 1]
                    sa = jnp.sum(s * tr_scr[1, rows, :], axis=0, keepdims=True)
                    q = jnp.sum(s * tr_scr[0, rows, :], axis=0, keepdims=True)
                    s_scr[base + jj] = s * tr_scr[3, rows, :] + (vi * tr_scr[4, rows, :] - sa * tr_scr[5, rows, :])
                    outs.append(q + vi * c1 - sa * c2)
                yt_scr[pl.ds(base, SUBLANES), :] = jnp.concatenate(outs, axis=0)
                return c

            lax.fori_loop(0, HEAD // SUBLANES, rows8, 0)
        y = yt_scr[...].T
        for hp in range(nhp):
            y_ref[:, t, hp * LANES:(hp + 1) * LANES] = y[hp * nb:(hp + 1) * nb]

    @pl.when(tblk == pl.num_programs(0) - 1)
    def _():
        sfin_ref[...] = s_scr[...]


def _wkv_ctx(shared, perdir, nb, tb, reverse):
    _, nseq, lc, dr = shared.shape
    assert nb * (dr // LANES) == LANES and 2 * HEAD == LANES
    nt = lc // tb
    di = 1 if reverse else 0

    def tsel(ti):
        return nt - 1 - ti if reverse else ti

    return pl.pallas_call(
        functools.partial(_wkv_ctx_kernel, reverse),
        grid=(nt,),
        in_specs=[pl.BlockSpec((3, nb, tb, dr), lambda ti: (0, 0, tsel(ti), 0)),
                  pl.BlockSpec((1, 3, nb, tb, dr), lambda ti: (di, 0, 0, tsel(ti), 0))],
        out_specs=[pl.BlockSpec((nb, tb, dr), lambda ti: (0, tsel(ti), 0)),
                   pl.BlockSpec((LANES, HEAD, LANES), lambda ti: (0, 0, 0))],
        out_shape=[jax.ShapeDtypeStruct((nb, lc, dr), F32),
                   jax.ShapeDtypeStruct((LANES, HEAD, LANES), F32)],
        scratch_shapes=[pltpu.VMEM((LANES, HEAD, LANES), F32),
                        pltpu.VMEM((6, LANES, LANES), F32),
                        pltpu.VMEM((LANES, LANES), F32)],
        compiler_params=_params("arbitrary"),
        name="wkv_ctx",
    )(shared, perdir)


def _dft_mats(l):
    n = 2 * l
    nfp = l + SUBLANES
    kf = np.arange(nfp, dtype=np.int64)
    ang = (2.0 * np.pi / n) * ((kf[:, None] * np.arange(l, dtype=np.int64)[None, :]) % n)
    valid = (kf <= l)[:, None]
    c = np.where(valid, np.cos(ang), 0.0)
    s = np.where(valid, np.sin(ang), 0.0)
    wk = np.where((kf == 0) | (kf == l), 1.0 / n, 2.0 / n)[None, :]
    mats = (c, s, c.T * wk, -(s.T) * wk)
    return [m for mat in mats for m in _split(jnp.asarray(mat.astype(np.float32)))]


def _hy_filter_kernel(zf_ref, w1_ref, b1_ref, w2_ref, b2_ref, fq_ref, w3_ref, dl_ref, bias_ref,
                      ch_ref, cl_ref, sh_ref, sl_ref, g_ref):
    zf = zf_ref[...]
    hid = jnp.sin(fq_ref[0:1] * (_dot3(zf, w1_ref[...]) + b1_ref[...]))
    hid = jnp.sin(fq_ref[1:2] * (_dot3(hid, w2_ref[...]) + b2_ref[...]))
    win = jnp.exp(-zf[:, 0:1] * jnp.abs(dl_ref[...]))
    hh, hl = _split(hid)
    filt = []
    for o in range(4):
        wh, wl = _split(w3_ref[o])
        filt.append(_dot_split(hh, hl, wh, wl) * win)
    for od in range(2):
        hf, hbk = filt[2 * od], filt[2 * od + 1]
        ph, plo = _split(hf + hbk)
        mh, ml = _split(hbk - hf)
        g_ref[od, 0] = _dot_split(ch_ref[...], cl_ref[...], ph, plo) + bias_ref[od:od + 1]
        g_ref[od, 1] = _dot_split(sh_ref[...], sl_ref[...], mh, ml)


def _hy_filter(zf, w1, b1, w2, b2, fq, w3, deltas, bias, mats):
    l = zf.shape[0]
    nfp = mats[0].shape[0]
    c = w3.shape[-1]
    cb = LANES
    return pl.pallas_call(
        _hy_filter_kernel,
        grid=(c // cb,),
        in_specs=[_full(zf.shape), _full(w1.shape), _full(b1.shape), _full(w2.shape), _full(b2.shape),
                  _full(fq.shape),
                  pl.BlockSpec((4, w3.shape[1], cb), lambda j: (0, 0, j)),
                  pl.BlockSpec((1, cb), lambda j: (0, j)),
                  pl.BlockSpec((2, cb), lambda j: (0, j)),
                  _const((nfp, l)), _const((nfp, l)), _const((nfp, l)), _const((nfp, l))],
        out_specs=pl.BlockSpec((2, 2, nfp, cb), lambda j: (0, 0, 0, j)),
        out_shape=jax.ShapeDtypeStruct((2, 2, nfp, c), F32),
        compiler_params=_params("arbitrary"),
        name="hy_filt",
    )(zf, w1, b1, w2, b2, fq, w3, deltas, bias, *mats[:4])


def _hy_conv_kernel(hy_ref, g_ref, ch_ref, cl_ref, sh_ref, sl_ref, cih_ref, cil_ref, sih_ref, sil_ref, o_ref):
    def conv(u, od):
        uh, ul = _split(u)
        re = _dot_split(ch_ref[...], cl_ref[...], uh, ul)
        im = -_dot_split(sh_ref[...], sl_ref[...], uh, ul)
        gre = g_ref[od, 0]
        gim = g_ref[od, 1]
        ah, al = _split(re * gre - im * gim)
        bh, bl = _split(re * gim + im * gre)
        return (_dot_split(cih_ref[...], cil_ref[...], ah, al)
                + _dot_split(sih_ref[...], sil_ref[...], bh, bl))

    z = hy_ref[1] * conv(hy_ref[0], 0)
    o_ref[...] = hy_ref[2] * conv(z, 1)


def _hy_conv(hy, g, mats, l, first_seq, n_seq):
    t, c = hy.shape[1], hy.shape[2]
    nfp = mats[0].shape[0]
    cb = 2 * LANES
    return pl.pallas_call(
        _hy_conv_kernel,
        grid=(n_seq, c // cb),
        in_specs=[pl.BlockSpec((3, l, cb), lambda b, j: (0, first_seq + b, j)),
                  pl.BlockSpec((2, 2, nfp, cb), lambda b, j: (0, 0, 0, j)),
                  _const((nfp, l)), _const((nfp, l)), _const((nfp, l)), _const((nfp, l)),
                  _const((l, nfp)), _const((l, nfp)), _const((l, nfp)), _const((l, nfp))],
        out_specs=pl.BlockSpec((l, cb), lambda b, j: (b, j)),
        out_shape=jax.ShapeDtypeStruct((n_seq * l, c), F32),
        compiler_params=_params("arbitrary", "arbitrary"),
        name="hy_conv",
    )(hy, g, *mats)


def _mix_post_kernel(y_ref, gb_ref, z_ref, x_ref, mod_ref, vec_ref, wout_ref, nf_ref, hs_ref, x1_ref, h2_ref):
    d = x_ref.shape[1]
    dr = hs_ref.shape[0]
    hs = hs_ref[...]
    vec = vec_ref[...]
    mod = mod_ref[0]
    gate1 = mod[:, 2 * d:3 * d]
    shift2 = mod[:, 3 * d:4 * d]
    scale2 = mod[:, 4 * d:5 * d]
    y = y_ref[...]
    mu = _headsum(y, hs) * (1.0 / HEAD)
    yc = y - mu
    var = _headsum(yc * yc, hs) * (1.0 / HEAD)
    yn = yc * lax.rsqrt(var + GN_EPS) * vec[0:1] + vec[1:2]
    ya = (yn + gb_ref[1]) * gb_ref[0]
    z = z_ref[...]
    yb = z * lax.rsqrt(jnp.mean(z * z, axis=-1, keepdims=True) + EPS) * vec[2:3]
    m = _dot(ya.astype(BF16), wout_ref[0:dr]) + _dot(yb.astype(BF16), wout_ref[dr:])
    x1 = x_ref[...] + gate1 * m
    x1_ref[...] = x1
    xn = x1 * lax.rsqrt(jnp.mean(x1 * x1, axis=-1, keepdims=True) + EPS) * nf_ref[...]
    h2_ref[...] = xn * (1.0 + scale2) + shift2


def _mix_post(ysum, gb, z, x, mods, n_ctx_tiles, tiles_per_lat, tm, vecs, w_out, norm_ffn, hs):
    t, d = x.shape
    dr = hs.shape[0]

    def mod_idx(i):
        return (jnp.where(i < n_ctx_tiles, 0, 1 + (i - n_ctx_tiles) // tiles_per_lat), 0, 0)

    tok = pl.BlockSpec((tm, dr), lambda i: (i, 0))
    tokd = pl.BlockSpec((tm, d), lambda i: (i, 0))
    return pl.pallas_call(
        _mix_post_kernel,
        grid=(t // tm,),
        in_specs=[tok, pl.BlockSpec((2, tm, dr), lambda i: (0, i, 0)), tok, tokd,
                  pl.BlockSpec((1, 1, mods.shape[-1]), mod_idx),
                  _full(vecs.shape), _full(w_out.shape), _full(norm_ffn.shape), _full(hs.shape)],
        out_specs=[tokd, tokd],
        out_shape=[jax.ShapeDtypeStruct((t, d), F32), jax.ShapeDtypeStruct((t, d), F32)],
        compiler_params=_params("arbitrary"),
        name="mix_post",
    )(ysum, gb, z, x, mods, vecs, w_out, norm_ffn, hs)


def _second_level_pairs():
    return [(a, b) for a in range(TOPK) for b in range(TOPK) if (a + 1) * (b + 1) <= TOPK]


def _route_kernel(h2_ref, wq_ref, keys_ref, r2_ref, b1_ref, e1_ref, e2_ref, sv_scr, s_scr, rk_scr, cnt_scr):
    nh = keys_ref.shape[1]
    nk = keys_ref.shape[2]
    half = keys_ref.shape[3]
    tt = h2_ref.shape[0]
    q = _dot(h2_ref[...].astype(BF16), wq_ref[...])
    key_id = lax.broadcasted_iota(jnp.int32, (nk, LANES), 0).astype(F32)
    neg = jnp.float32(-jnp.inf)

    for h in range(nh):
        for p in range(2):
            c0 = (h * 2 + p) * half
            qh, ql = _split(q[:, c0:c0 + half])
            kh, kl = _split(keys_ref[p, h])
            dims = (((1,), (1,)), ((), ()))
            s = (lax.dot_general(kh, qh, dims, preferred_element_type=F32)
                 + (lax.dot_general(kl, qh, dims, preferred_element_type=F32)
                    + lax.dot_general(kh, ql, dims, preferred_element_type=F32)))
            s_scr[p, h] = s

    nchunk = tt // LANES

    def first_level_quick():
        def chunk(c, ranked):
            col = pl.ds(pl.multiple_of(c * LANES, LANES), LANES)
            for h in range(nh):
                def extract(r, carry, h=h):
                    out = []
                    for p in range(2):
                        m_prev, above = carry[p]
                        s = s_scr[p, h, :, col]
                        below = s < m_prev
                        m = jnp.max(jnp.where(below, s, neg), axis=0, keepdims=True)
                        sv_scr[p, r, h:h + 1, col] = m
                        out.append((m, above + jnp.where(below, 1.0, 0.0)))
                    return tuple(out)

                start = (jnp.full((1, LANES), jnp.inf, F32), jnp.zeros((nk, LANES), F32))
                done = lax.fori_loop(0, TOPK, extract, (start, start))
                for p in range(2):
                    m_last, above = done[p]
                    rank = above - 1.0 + jnp.where(s_scr[p, h, :, col] < m_last, 1.0, 0.0)
                    rk_scr[p, h, :, col] = rank
                    ranked = jnp.maximum(ranked, jnp.sum(jnp.where(rank < float(TOPK), 1.0, 0.0),
                                                         axis=0, keepdims=True))
            return ranked

        return lax.fori_loop(0, nchunk, chunk, jnp.zeros((1, LANES), F32))

    def first_level_exact():
        def chunk(c, carry):
            col = pl.ds(pl.multiple_of(c * LANES, LANES), LANES)
            for h in range(nh):
                for p in range(2):
                    def extract(r, xr, p=p, h=h):
                        x, rank = xr
                        m = jnp.max(x, axis=0, keepdims=True)
                        first = jnp.min(jnp.where(x == m, key_id, float(nk)), axis=0, keepdims=True)
                        sel = key_id == first
                        sv_scr[p, r, h:h + 1, col] = m
                        return jnp.where(sel, neg, x), jnp.where(sel, lax.convert_element_type(r, F32), rank)

                    _, rank = lax.fori_loop(0, TOPK, extract,
                                            (s_scr[p, h, :, col], jnp.full((nk, LANES), float(TOPK), F32)))
                    rk_scr[p, h, :, col] = rank
            return carry

        lax.fori_loop(0, nchunk, chunk, 0)

    ranked = first_level_quick()

    @pl.when(jnp.max(ranked) > float(TOPK))
    def _():
        first_level_exact()

    pairs = _second_level_pairs()

    def second_level(c, carry):
        col = pl.ds(pl.multiple_of(c * LANES, LANES), LANES)
        v1 = [sv_scr[0, a, :, col] for a in range(TOPK)]
        v2 = [sv_scr[1, b, :, col] for b in range(TOPK)]
        top = v1[0] + v2[0]

        def select(_, carry):
            cand, cnt, zsum = list(carry[0]), list(carry[1]), carry[2]
            m = cand[0]
            for cv in cand[1:]:
                m = jnp.maximum(m, cv)
            zsum = zsum + jnp.exp(m - top)
            found = jnp.zeros((nh, LANES), F32)
            for ci, (a, _b) in enumerate(pairs):
                eq = jnp.where(cand[ci] == m, 1.0, 0.0)
                hit = eq * (1.0 - found)
                found = jnp.maximum(found, eq)
                cand[ci] = jnp.where(hit > 0.0, neg, cand[ci])
                cnt[a] = cnt[a] + hit
            return tuple(cand), tuple(cnt), zsum

        zero = jnp.zeros((nh, LANES), F32)
        _, cnt, zsum = lax.fori_loop(0, TOPK, select,
                                     (tuple(v1[a] + v2[b] for a, b in pairs), (zero,) * TOPK, zero))
        for a in range(TOPK):
            cnt_scr[a, :, col] = cnt[a]
        cnt_scr[TOPK, :, col] = 1.0 / zsum
        return carry

    lax.fori_loop(0, nchunk, second_level, 0)

    for h in range(nh):
        rank1 = rk_scr[0, h]
        b1 = jnp.zeros((nk, tt), F32)
        for a in range(TOPK):
            b1 = b1 + jnp.where(rank1 == float(a), cnt_scr[a, h:h + 1, :], 0.0)
        b1_ref[h] = b1
        r2_ref[h] = rk_scr[1, h].astype(BF16)
        e1_ref[h] = jnp.exp(s_scr[0, h] - sv_scr[0, 0, h:h + 1, :]) * cnt_scr[TOPK, h:h + 1, :]
        e2_ref[h] = jnp.exp(s_scr[1, h] - sv_scr[1, 0, h:h + 1, :]).astype(BF16)


def _route(h2, wq, keys, tt):
    t, d = h2.shape
    _, nh, nk, _ = keys.shape
    out = jax.ShapeDtypeStruct((nh, nk, t), F32)
    out16 = jax.ShapeDtypeStruct((nh, nk, t), BF16)
    ospec = pl.BlockSpec((nh, nk, tt), lambda i: (0, 0, i))
    return pl.pallas_call(
        _route_kernel,
        grid=(t // tt,),
        in_specs=[pl.BlockSpec((tt, d), lambda i: (i, 0)), _full(wq.shape), _full(keys.shape)],
        out_specs=[ospec, ospec, ospec, ospec],
        out_shape=[out16, out, out, out16],
        scratch_shapes=[pltpu.VMEM((2, TOPK, nh, tt), F32),
                        pltpu.VMEM((2, nh, nk, tt), F32),
                        pltpu.VMEM((2, nh, nk, tt), F32),
                        pltpu.VMEM((TOPK + 1, nh, tt), F32)],
        compiler_params=_params("arbitrary"),
        name="route",
    )(h2, wq, keys)


def _peer_kernel(h2t_ref, u_ref, vt_ref, r2_ref, e2_ref, b1_ref, e1_ref, x1_ref, mod_ref, fn_ref,
                 o_ref, acc_scr, a_scr):
    d = x1_ref.shape[1]
    nh, nk, tt = r2_ref.shape
    n1 = b1_ref.shape[1]
    eb = pl.program_id(1)

    @pl.when(eb == 0)
    def _():
        acc_scr[...] = jnp.zeros_like(acc_scr)

    hid = _dot(u_ref[...], h2t_ref[...])
    pk = 2 * SUBLANES
    for i in range(n1):
        w = jnp.zeros((nk // pk, pk, tt), BF16)
        for h in range(nh):
            b1r = jnp.broadcast_to(b1_ref[h, i:i + 1, :], (pk, tt)).astype(BF16)[None]
            e1r = jnp.broadcast_to(e1_ref[h, i:i + 1, :], (pk, tt)).astype(BF16)[None]
            r2 = r2_ref[h].reshape(nk // pk, pk, tt)
            e2 = e2_ref[h].reshape(nk // pk, pk, tt)
            w = w + jnp.where(r2 < b1r, e1r * e2, jnp.zeros((), BF16))
        hi = hid[i * nk:(i + 1) * nk]
        act = 0.5 * hi * (1.0 + lax.erf(hi * (1.0 / math.sqrt(2.0))))
        a_scr[i * nk:(i + 1) * nk, :] = (act.astype(BF16).reshape(nk // pk, pk, tt) * w).reshape(nk, tt)
    acc_scr[...] += _dot(vt_ref[...], a_scr[...])

    @pl.when(eb == pl.num_programs(1) - 1)
    def _():
        gate2 = mod_ref[0][:, 5 * d:6 * d]
        xo = x1_ref[...] + gate2 * acc_scr[...].T
        o_ref[...] = xo * lax.rsqrt(jnp.mean(xo * xo, axis=-1, keepdims=True) + EPS) * fn_ref[...]


def _peer(h2t, u, vt, r2, e2, b1, e1, x1, mods, final_norm, n_ctx_tiles, tiles_per_lat, tt, n1):
    d, t = h2t.shape
    ne = u.shape[0]
    nh, nk, _ = r2.shape
    eb = n1 * nk

    def mod_idx(i, e):
        return (jnp.where(i < n_ctx_tiles, 0, 1 + (i - n_ctx_tiles) // tiles_per_lat), 0, 0)

    dense = pl.BlockSpec((nh, nk, tt), lambda i, e: (0, 0, i))
    rows = pl.BlockSpec((nh, n1, tt), lambda i, e: (0, e, i))
    return pl.pallas_call(
        _peer_kernel,
        grid=(t // tt, ne // eb),
        in_specs=[pl.BlockSpec((d, tt), lambda i, e: (0, i)),
                  pl.BlockSpec((eb, d), lambda i, e: (e, 0)),
                  pl.BlockSpec((d, eb), lambda i, e: (0, e)),
                  dense, dense, rows, rows,
                  pl.BlockSpec((tt, d), lambda i, e: (i, 0)),
                  pl.BlockSpec((1, 1, mods.shape[-1]), mod_idx),
                  _full(final_norm.shape)],
        out_specs=pl.BlockSpec((tt, d), lambda i, e: (i, 0)),
        out_shape=jax.ShapeDtypeStruct((t, d), F32),
        scratch_shapes=[pltpu.VMEM((d, tt), F32), pltpu.VMEM((eb, tt), BF16)],
        compiler_params=_params("arbitrary", "arbitrary"),
        name="peer",
    )(h2t, u, vt, r2, e2, b1, e1, x1, mods, final_norm)


def _hyena_features(l, bands):
    t = jnp.linspace(0.0, 1.0, l, dtype=F32)[:, None]
    wpos = 2.0 * math.pi * jnp.arange(l, dtype=F32)[:, None] / l
    f = jnp.linspace(1e-4, bands - 1, bands, dtype=F32)[None, :]
    return jnp.concatenate([t, jnp.cos(f * wpos), -jnp.sin(f * wpos)], axis=-1)


def kernel(x_prompt, x_sample, state_rwkv, c, c_ctx, w_ada, b_ada, norm_mix, norm_ffn, w_in, conv_w, w_out, rwkv_w0, rwkv_w1, rwkv_w2, rwkv_a0, rwkv_a1, rwkv_a2, rwkv_g1, rwkv_g2, rwkv_k_k, rwkv_k_a, rwkv_r_k, rwkv_ln_w, rwkv_ln_b, hy_f_w1, hy_f_b1, hy_f_w2, hy_f_b2, hy_freq, hy_f_w3, hy_bias, hy_norm, peer_wq, peer_keys, peer_u, peer_v, final_norm):
    bc, lc, d = x_prompt.shape
    bl, ll, _ = x_sample.shape
    depth = w_ada.shape[0]
    assert depth == 1
    dr = rwkv_w0.shape[-1]
    dh = hy_norm.shape[-1]
    nh = dr // HEAD
    assert dr == dh and w_in.shape[-1] == 3 * dr + 3 * dh
    tc, tl = bc * lc, bl * ll
    t = tc + tl
    tm = lc
    assert ll % tm == 0 and tm % GRID_W == 0 and tc % ll == 0
    n_ctx_tiles = tc // tm
    tiles_per_lat = ll // tm
    parts = LANES // (bl * nh)
    assert parts * bl * nh == LANES and HEAD % (parts * SUBLANES) == 0
    nil = HEAD // parts
    l = 0

    rows = jnp.concatenate([c_ctx[None, :], c, jnp.zeros((SUBLANES - 1 - bl, d), F32)], axis=0)
    mods = _ada(rows, w_ada[l], b_ada[l]).reshape(SUBLANES, 1, 6 * d)

    x = jnp.concatenate([x_prompt.reshape(tc, d), x_sample.reshape(tl, d)], axis=0)

    def pad_cols(w):
        return jnp.pad(w, ((0, 0), (0, LANES - w.shape[1])))

    def pad_rows(w):
        return jnp.pad(w, ((0, LANES - w.shape[0]), (0, 0)))

    lora1 = jnp.concatenate([pad_cols(rwkv_w1[l, 0]), pad_cols(rwkv_w1[l, 1]), pad_cols(rwkv_a1[l, 0]),
                             pad_cols(rwkv_a1[l, 1]), pad_cols(rwkv_g1[l])], axis=1).astype(BF16)
    w2 = jnp.stack([pad_rows(rwkv_w2[l, 0]), pad_rows(rwkv_w2[l, 1])])
    a2 = jnp.stack([pad_rows(rwkv_a2[l, 0]), pad_rows(rwkv_a2[l, 1])])
    vecs_pre = jnp.stack([rwkv_w0[l, 0], rwkv_w0[l, 1], rwkv_a0[l, 0], rwkv_a0[l, 1], rwkv_k_k[l], rwkv_k_a[l],
                          rwkv_r_k[l, 0].reshape(dr), rwkv_r_k[l, 1].reshape(dr)])
    head_id = jnp.arange(dr, dtype=jnp.int32) // HEAD
    hs = (head_id[:, None] == head_id[None, :]).astype(BF16)
    shared, perdir, gb, hy = _mix_pre(x, mods, n_ctx_tiles, tiles_per_lat, tm, norm_mix[l][None, :],
                                      w_in[l].astype(BF16), conv_w[l], lora1, w2, a2, rwkv_g2[l], vecs_pre, hs)

    def lat_lanes(a):
        a = a.reshape(bl, ll, nh, HEAD).transpose(1, 3, 0, 2)
        return jnp.broadcast_to(a[..., None], (ll, HEAD, bl, nh, parts)).reshape(1, ll, HEAD, LANES)

    def lat_rows(a):
        a = a.reshape(bl, ll, nh, parts, nil).transpose(1, 4, 0, 2, 3)
        return a.reshape(1, ll, nil, LANES)

    sh_l = shared[:, tc:]
    pd_l = perdir[:, :, tc:]
    shared4 = shared.reshape(3, t // lc, lc, dr)
    perdir4 = perdir.reshape(2, 3, t // lc, lc, dr)
    y_cf, s_cf = _wkv_ctx(shared4, perdir4, bc, 16, False)
    y_cb, s_cb = _wkv_ctx(shared4, perdir4, bc, 16, True)
    lat_dir = [jnp.concatenate([lat_lanes(pd_l[0, j]), lat_lanes(pd_l[1, j])], axis=0) for j in range(3)]
    s0 = state_rwkv[:, l].reshape(bl, 2, nh, parts, nil, HEAD).transpose(1, 4, 5, 0, 2, 3)
    y_lat, _ = _wkv(lat_lanes(sh_l[0]), lat_lanes(sh_l[1]), lat_rows(sh_l[2]), *lat_dir,
                    s0.reshape(2, nil, HEAD, LANES), tb=32)
    ys_ctx = (y_cf + y_cb).reshape(tc, dr)
    ys_lat = (y_lat[0] + y_lat[1]).reshape(ll, nil, bl, nh, parts).transpose(2, 0, 3, 4, 1).reshape(tl, dr)
    ysum = jnp.concatenate([ys_ctx, ys_lat], axis=0)
    new_state = jnp.stack([s_cf, s_cb]).reshape(2, 2, HEAD, HEAD, nh // 2, bc).transpose(5, 0, 4, 1, 2, 3)
    new_state = new_state.reshape(bc, 1, 2, nh, HEAD, HEAD)

    bands = (hy_f_w1.shape[1] - 1) // 2
    femb = hy_f_w1.shape[1]
    fpad = 32
    w1p = jnp.pad(hy_f_w1[l], ((0, fpad - femb), (0, 0)))
    hidden = hy_f_w2.shape[-1]
    w3 = hy_f_w3[l].reshape(hidden, 4, dh).transpose(1, 0, 2)
    max_decay = math.log(1e-2) / 0.3
    min_decay = math.log(1e-2) / 1.5
    deltas = jnp.linspace(min_decay, max_decay, dh, dtype=F32)[None, :]
    z_parts = []
    for seq_len, first_seq, n_seq in ((lc, 0, bc), (ll, tc // ll, bl)):
        mats = _dft_mats(seq_len)
        zf = jnp.pad(_hyena_features(seq_len, bands), ((0, 0), (0, fpad - femb)))
        g = _hy_filter(zf, w1p, hy_f_b1[l][None, :], hy_f_w2[l], hy_f_b2[l][None, :], hy_freq[l], w3,
                       deltas, hy_bias[l], mats)
        z_parts.append(_hy_conv(hy, g, mats, seq_len, first_seq, n_seq))
    zhy = jnp.concatenate(z_parts, axis=0)

    vecs_post = jnp.concatenate([jnp.stack([rwkv_ln_w[l], rwkv_ln_b[l], hy_norm[l]]),
                                 jnp.zeros((SUBLANES - 3, dr), F32)], axis=0)
    x1, h2 = _mix_post(ysum, gb, zhy, x, mods, n_ctx_tiles, tiles_per_lat, tm, vecs_post,
                       w_out[l].astype(BF16), norm_ffn[l][None, :], hs)

    r2, b1, e1, e2 = _route(h2, peer_wq[l].astype(BF16), peer_keys[l], tt=512)
    tt = 512
    assert ll % tt == 0 and tc % tt == 0
    out = _peer(h2.astype(BF16).T, peer_u[l].astype(BF16), peer_v[l].astype(BF16).T, r2, e2, b1, e1, x1, mods,
                final_norm[None, :], tc // tt, ll // tt, tt, n1=SUBLANES)
    return out[:tc].reshape(bc, lc, d), out[tc:].reshape(bl, ll, d), new_state
```

```python
import functools
import math

import jax
import jax.numpy as jnp
import numpy as np
from jax import lax
from jax.experimental import pallas as pl
from jax.experimental.pallas import tpu as pltpu

F32 = jnp.float32
BF16 = jnp.bfloat16

GRID_W = 64
HEAD = 64
TOPK = 16
EPS = 1e-6
GN_EPS = 64e-5
LANES = 128
SUBLANES = 8
VMEM_LIMIT = 56 * 1024 * 1024


def _params(*sem):
    return pltpu.CompilerParams(dimension_semantics=sem, vmem_limit_bytes=VMEM_LIMIT)


def _split(x):
    hi = x.astype(BF16)
    lo = (x - hi.astype(F32)).astype(BF16)
    return hi, lo


def _dot(a, b):
    return jnp.dot(a, b, preferred_element_type=F32)


def _dot_split(ah, al, bh, bl):
    return _dot(ah, bh) + (_dot(al, bh) + _dot(ah, bl))


def _dot3(a, b):
    ah, al = _split(a)
    bh, bl = _split(b)
    return _dot_split(ah, al, bh, bl)


def _full(shape):
    n = len(shape)
    return pl.BlockSpec(shape, lambda *_: (0,) * n)


def _const(shape):
    n = len(shape)
    return pl.BlockSpec(shape, lambda *_: (0,) * n, pipeline_mode=pl.Buffered(1))


def _ada_kernel(c_ref, w_ref, b_ref, o_ref):
    c = c_ref[...]
    s = c * jax.nn.sigmoid(c)
    o_ref[...] = _dot3(s, w_ref[...]) + b_ref[...]


def _ada(c_rows, w, b):
    rows, d = c_rows.shape
    n = w.shape[1]
    bn = n // 4
    return pl.pallas_call(
        _ada_kernel,
        grid=(4,),
        in_specs=[_full((rows, d)),
                  pl.BlockSpec((d, bn), lambda j: (0, j)),
                  pl.BlockSpec((1, bn), lambda j: (0, j))],
        out_specs=pl.BlockSpec((rows, bn), lambda j: (0, j)),
        out_shape=jax.ShapeDtypeStruct((rows, n), F32),
        compiler_params=_params("arbitrary"),
        name="ada",
    )(c_rows, w, b.reshape(1, n))


def _headsum(z, hs):
    zh, zl = _split(z)
    return _dot(zh, hs) + _dot(zl, hs)


def _mix_pre_kernel(n_ctx_tiles, x_ref, mod_ref, nm_ref, win_ref, cw_ref, l1_ref, w2_ref, a2_ref,
                    g2_ref, vec_ref, hs_ref, sh_ref, dr_ref, gb_ref, hy_ref):
    tm, d = x_ref.shape
    dr = hs_ref.shape[0]
    x = x_ref[...]
    mod = mod_ref[0]
    shift1 = mod[:, 0:d]
    scale1 = mod[:, d:2 * d]
    xn = x * lax.rsqrt(jnp.mean(x * x, axis=-1, keepdims=True) + EPS) * nm_ref[...]
    hb = (xn * (1.0 + scale1) + shift1).astype(BF16)

    pos = lax.broadcasted_iota(jnp.int32, (tm, dr), 0)
    is_ctx = pl.program_id(0) < n_ctx_tiles
    row = pos % GRID_W
    keep_prev = jnp.where(is_ctx, jnp.where(pos == 0, 0.0, 1.0), jnp.where(row == 0, 0.0, 1.0))
    keep_next = jnp.where(is_ctx, jnp.where(pos == tm - 1, 0.0, 1.0),
                          jnp.where(row == GRID_W - 1, 0.0, 1.0))

    def proj(c):
        u = _dot(hb, win_ref[:, c * dr:(c + 1) * dr])
        cw = cw_ref[:, c * dr:(c + 1) * dr]
        up = pltpu.roll(u, 1, 0) * keep_prev
        un = pltpu.roll(u, tm - 1, 0) * keep_next
        return up * cw[0:1] + u * cw[1:2] + un * cw[2:3]

    r = proj(0)
    k = proj(1)
    v = proj(2)
    hy_ref[0] = proj(3)
    hy_ref[1] = proj(4)
    hy_ref[2] = proj(5)

    lo = _dot(hb, l1_ref[...])
    vec = vec_ref[...]
    k_k = vec[4:5]
    k_a = vec[5:6]
    hs = hs_ref[...]
    kkr = k * k_k
    kk = kkr * lax.rsqrt(jnp.maximum(_headsum(kkr * kkr, hs), 1e-24))
    sh_ref[0] = r
    sh_ref[1] = kk
    sh_ref[2] = v
    gb_ref[0] = _dot3(jax.nn.sigmoid(lo[:, 4 * LANES:5 * LANES]), g2_ref[...])
    bon = jnp.zeros_like(r)
    for dd in range(2):
        w = vec[dd:dd + 1] + _dot3(jnp.tanh(lo[:, dd * LANES:(dd + 1) * LANES]), w2_ref[dd])
        z = -w
        softplus = jnp.maximum(z, 0.0) + jnp.log1p(jnp.exp(-jnp.abs(z)))
        dr_ref[dd, 0] = jnp.exp(-jnp.exp(-softplus - 0.5))
        a = jax.nn.sigmoid(vec[2 + dd:3 + dd]
                           + _dot3(lo[:, (2 + dd) * LANES:(3 + dd) * LANES], a2_ref[dd]))
        kd = k * (1.0 + (a - 1.0) * k_a)
        dr_ref[dd, 1] = kd
        dr_ref[dd, 2] = kk * a
        bon = bon + r * kd * vec[6 + dd:7 + dd]
    gb_ref[1] = _headsum(bon, hs) * v


def _mix_pre(x, mods, n_ctx_tiles, tiles_per_lat, tm, norm_mix, w_in, conv_w, lora1, w2, a2, g2, vecs, hs):
    t, d = x.shape
    dr = hs.shape[0]
    n_tiles = t // tm

    def mod_idx(i):
        return (jnp.where(i < n_ctx_tiles, 0, 1 + (i - n_ctx_tiles) // tiles_per_lat), 0, 0)

    return pl.pallas_call(
        functools.partial(_mix_pre_kernel, n_ctx_tiles),
        grid=(n_tiles,),
        in_specs=[pl.BlockSpec((tm, d), lambda i: (i, 0)),
                  pl.BlockSpec((1, 1, mods.shape[-1]), mod_idx),
                  _full(norm_mix.shape), _full(w_in.shape), _full(conv_w.shape), _full(lora1.shape),
                  _full(w2.shape), _full(a2.shape), _full(g2.shape), _full(vecs.shape), _full(hs.shape)],
        out_specs=[pl.BlockSpec((3, tm, dr), lambda i: (0, i, 0)),
                   pl.BlockSpec((2, 3, tm, dr), lambda i: (0, 0, i, 0)),
                   pl.BlockSpec((2, tm, dr), lambda i: (0, i, 0)),
                   pl.BlockSpec((3, tm, dr), lambda i: (0, i, 0))],
        out_shape=[jax.ShapeDtypeStruct((3, t, dr), F32),
                   jax.ShapeDtypeStruct((2, 3, t, dr), F32),
                   jax.ShapeDtypeStruct((2, t, dr), F32),
                   jax.ShapeDtypeStruct((3, t, dr), F32)],
        compiler_params=_params("arbitrary"),
        name="mix_pre",
    )(x, mods, norm_mix, w_in, conv_w, lora1, w2, a2, g2, vecs, hs)


def _wkv_ctx_kernel(reverse, sh_ref, pd_ref, y_ref, sfin_ref, s_scr, tr_scr, yt_scr):
    nb, tb = sh_ref.shape[1], sh_ref.shape[2]
    nhp = sh_ref.shape[3] // LANES
    tblk = pl.program_id(0)

    @pl.when(tblk == 0)
    def _():
        s_scr[...] = jnp.zeros_like(s_scr)

    def to_lanes(ref, lead, t):
        blocks = [ref[lead + (slice(None), t, slice(hp * LANES, (hp + 1) * LANES))] for hp in range(nhp)]
        return jnp.concatenate(blocks, axis=0).T

    for tau in range(tb):
        t = tb - 1 - tau if reverse else tau
        r_t = to_lanes(sh_ref, (0,), t)
        d_t = to_lanes(pd_ref, (0, 0), t)
        tr_scr[0] = d_t * r_t
        tr_scr[1] = to_lanes(sh_ref, (1,), t)
        tr_scr[2] = to_lanes(sh_ref, (2,), t)
        tr_scr[3] = d_t
        k_t = to_lanes(pd_ref, (0, 1), t)
        tr_scr[4] = k_t
        kka_t = to_lanes(pd_ref, (0, 2), t)
        tr_scr[5] = kka_t
        for h2 in range(2):
            rows = slice(h2 * HEAD, (h2 + 1) * HEAD)
            c1 = jnp.sum(k_t[rows] * r_t[rows], axis=0, keepdims=True)
            c2 = jnp.sum(kka_t[rows] * r_t[rows], axis=0, keepdims=True)

            def rows8(i8, c, h2=h2, rows=rows, c1=c1, c2=c2):
                base = pl.multiple_of(h2 * HEAD + i8 * SUBLANES, SUBLANES)
                vblk = tr_scr[2, pl.ds(base, SUBLANES), :]
                outs = []
                for jj in range(SUBLANES):
                    s = s_scr[base + jj]
                    vi = vblk[jj:jj + 1]
                    sa = jnp.sum(s * tr_scr[1, rows, :], axis=0, keepdims=True)
                    q = jnp.sum(s * tr_scr[0, rows, :], axis=0, keepdims=True)
                    s_scr[base + jj] = s * tr_scr[3, rows, :] + (vi * tr_scr[4, rows, :] - sa * tr_scr[5, rows, :])
                    outs.append(q + vi * c1 - sa * c2)
                yt_scr[pl.ds(base, SUBLANES), :] = jnp.concatenate(outs, axis=0)
                return c

            lax.fori_loop(0, HEAD // SUBLANES, rows8, 0)
        y = yt_scr[...].T
        for hp in range(nhp):
            y_ref[:, t, hp * LANES:(hp + 1) * LANES] = y[hp * nb:(hp + 1) * nb]

    @pl.when(tblk == pl.num_programs(0) - 1)
    def _():
        sfin_ref[...] = s_scr[...]


def _wkv_ctx(shared, perdir, nb, tb, reverse):
    _, nseq, lc, dr = shared.shape
    assert nb * (dr // LANES) == LANES and 2 * HEAD == LANES
    nt = lc // tb
    di = 1 if reverse else 0

    def tsel(ti):
        return nt - 1 - ti if reverse else ti

    return pl.pallas_call(
        functools.partial(_wkv_ctx_kernel, reverse),
        grid=(nt,),
        in_specs=[pl.BlockSpec((3, nb, tb, dr), lambda ti: (0, 0, tsel(ti), 0)),
                  pl.BlockSpec((1, 3, nb, tb, dr), lambda ti: (di, 0, 0, tsel(ti), 0))],
        out_specs=[pl.BlockSpec((nb, tb, dr), lambda ti: (0, tsel(ti), 0)),
                   pl.BlockSpec((LANES, HEAD, LANES), lambda ti: (0, 0, 0))],
        out_shape=[jax.ShapeDtypeStruct((nb, lc, dr), F32),
                   jax.ShapeDtypeStruct((LANES, HEAD, LANES), F32)],
        scratch_shapes=[pltpu.VMEM((LANES, HEAD, LANES), F32),
                        pltpu.VMEM((6, LANES, LANES), F32),
                        pltpu.VMEM((LANES, LANES), F32)],
        compiler_params=_params("arbitrary"),
        name="wkv_ctx",
    )(shared, perdir)


def _wkv_lat_kernel(reverse, sh_ref, pd_ref, s0_ref, y_ref, s_scr):
    nb, tb, dr = sh_ref.shape[1], sh_ref.shape[2], sh_ref.shape[3]
    nhp = dr // LANES
    parts = LANES // (nhp * nb)
    nil = HEAD // parts
    tblk = pl.program_id(0)
    lane_part = lax.broadcasted_iota(jnp.int32, (parts, LANES), 1) % parts
    diag = jnp.where(lane_part == lax.broadcasted_iota(jnp.int32, (parts, LANES), 0), 1.0, 0.0)

    @pl.when(tblk == 0)
    def _():
        s_scr[...] = s0_ref[...]

    def to_lanes(ref, lead, t):
        blocks = [jnp.broadcast_to(ref[lead + (b, slice(t, t + 1), slice(hp * LANES, (hp + 1) * LANES))],
                                   (parts, LANES))
                  for hp in range(nhp) for b in range(nb)]
        return jnp.concatenate(blocks, axis=0).T

    for tau in range(tb):
        t = tb - 1 - tau if reverse else tau
        r_t = to_lanes(sh_ref, (0,), t)
        kk_t = to_lanes(sh_ref, (1,), t)
        v_t = to_lanes(sh_ref, (2,), t)
        d_t = to_lanes(pd_ref, (0, 0), t)
        k_t = to_lanes(pd_ref, (0, 1), t)
        kka_t = to_lanes(pd_ref, (0, 2), t)
        placed = []
        for h2 in range(2):
            rows = slice(h2 * HEAD, (h2 + 1) * HEAD)
            rt, kkt, dt, kt, kkat = r_t[rows], kk_t[rows], d_t[rows], k_t[rows], kka_t[rows]
            drt = dt * rt
            c1 = jnp.sum(kt * rt, axis=0, keepdims=True)
            c2 = jnp.sum(kkat * rt, axis=0, keepdims=True)
            for il in range(nil):
                r0 = h2 * HEAD + il * parts
                vi = jnp.sum(v_t[r0:r0 + parts] * diag, axis=0, keepdims=True)
                s = s_scr[h2 * nil + il]
                sa = jnp.sum(s * kkt, axis=0, keepdims=True)
                q = jnp.sum(s * drt, axis=0, keepdims=True)
                s_scr[h2 * nil + il] = s * dt + (vi * kt - sa * kkat)
                placed.append((q + vi * c1 - sa * c2) * diag)
        y = jnp.concatenate(placed, axis=0).T
        for hp in range(nhp):
            for b in range(nb):
                r0 = (hp * nb + b) * parts
                y_ref[b, t:t + 1, hp * LANES:(hp + 1) * LANES] = jnp.sum(y[r0:r0 + parts], axis=0, keepdims=True)


def _wkv_lat(shared, perdir, s0, first_blk, nb, tb, reverse):
    _, nseq, ll, dr = shared.shape
    nt = ll // tb
    di = 1 if reverse else 0

    def tsel(ti):
        return nt - 1 - ti if reverse else ti

    return pl.pallas_call(
        functools.partial(_wkv_lat_kernel, reverse),
        grid=(nt,),
        in_specs=[pl.BlockSpec((3, nb, tb, dr), lambda ti: (0, first_blk, tsel(ti), 0)),
                  pl.BlockSpec((1, 3, nb, tb, dr), lambda ti: (di, 0, first_blk, tsel(ti), 0)),
                  _full(s0.shape)],
        out_specs=pl.BlockSpec((nb, tb, dr), lambda ti: (0, tsel(ti), 0)),
        out_shape=jax.ShapeDtypeStruct((nb, ll, dr), F32),
        scratch_shapes=[pltpu.VMEM(s0.shape, F32)],
        compiler_params=_params("arbitrary"),
        name="wkv_lat",
    )(shared, perdir, s0)


def _dft_mats(l):
    n = 2 * l
    nfp = l + SUBLANES
    kf = np.arange(nfp, dtype=np.int64)
    ang = (2.0 * np.pi / n) * ((kf[:, None] * np.arange(l, dtype=np.int64)[None, :]) % n)
    valid = (kf <= l)[:, None]
    c = np.where(valid, np.cos(ang), 0.0)
    s = np.where(valid, np.sin(ang), 0.0)
    wk = np.where((kf == 0) | (kf == l), 1.0 / n, 2.0 / n)[None, :]
    mats = (c, s, c.T * wk, -(s.T) * wk)
    return [m for mat in mats for m in _split(jnp.asarray(mat.astype(np.float32)))]


def _hy_filter_kernel(zf_ref, w1_ref, b1_ref, w2_ref, b2_ref, fq_ref, w3_ref, dl_ref, bias_ref,
                      ch_ref, cl_ref, sh_ref, sl_ref, g_ref):
    zf = zf_ref[...]
    hid = jnp.sin(fq_ref[0:1] * (_dot3(zf, w1_ref[...]) + b1_ref[...]))
    hid = jnp.sin(fq_ref[1:2] * (_dot3(hid, w2_ref[...]) + b2_ref[...]))
    win = jnp.exp(-zf[:, 0:1] * jnp.abs(dl_ref[...]))
    hh, hl = _split(hid)
    filt = []
    for o in range(4):
        wh, wl = _split(w3_ref[o])
        filt.append(_dot_split(hh, hl, wh, wl) * win)
    for od in range(2):
        hf, hbk = filt[2 * od], filt[2 * od + 1]
        ph, plo = _split(hf + hbk)
        mh, ml = _split(hbk - hf)
        g_ref[od, 0] = _dot_split(ch_ref[...], cl_ref[...], ph, plo) + bias_ref[od:od + 1]
        g_ref[od, 1] = _dot_split(sh_ref[...], sl_ref[...], mh, ml)


def _hy_filter(zf, w1, b1, w2, b2, fq, w3, deltas, bias, mats):
    l = zf.shape[0]
    nfp = mats[0].shape[0]
    c = w3.shape[-1]
    cb = LANES
    return pl.pallas_call(
        _hy_filter_kernel,
        grid=(c // cb,),
        in_specs=[_full(zf.shape), _full(w1.shape), _full(b1.shape), _full(w2.shape), _full(b2.shape),
                  _full(fq.shape),
                  pl.BlockSpec((4, w3.shape[1], cb), lambda j: (0, 0, j)),
                  pl.BlockSpec((1, cb), lambda j: (0, j)),
                  pl.BlockSpec((2, cb), lambda j: (0, j)),
                  _const((nfp, l)), _const((nfp, l)), _const((nfp, l)), _const((nfp, l))],
        out_specs=pl.BlockSpec((2, 2, nfp, cb), lambda j: (0, 0, 0, j)),
        out_shape=jax.ShapeDtypeStruct((2, 2, nfp, c), F32),
        compiler_params=_params("arbitrary"),
        name="hy_filt",
    )(zf, w1, b1, w2, b2, fq, w3, deltas, bias, *mats[:4])


def _hy_conv_kernel(hy_ref, g_ref, ch_ref, cl_ref, sh_ref, sl_ref, cih_ref, cil_ref, sih_ref, sil_ref, o_ref):
    def conv(u, od):
        uh, ul = _split(u)
        re = _dot_split(ch_ref[...], cl_ref[...], uh, ul)
        im = -_dot_split(sh_ref[...], sl_ref[...], uh, ul)
        gre = g_ref[od, 0]
        gim = g_ref[od, 1]
        ah, al = _split(re * gre - im * gim)
        bh, bl = _split(re * gim + im * gre)
        return (_dot_split(cih_ref[...], cil_ref[...], ah, al)
                + _dot_split(sih_ref[...], sil_ref[...], bh, bl))

    z = hy_ref[1] * conv(hy_ref[0], 0)
    o_ref[...] = hy_ref[2] * conv(z, 1)


def _hy_conv(hy, g, mats, l, first_seq, n_seq):
    t, c = hy.shape[1], hy.shape[2]
    nfp = mats[0].shape[0]
    cb = 2 * LANES
    return pl.pallas_call(
        _hy_conv_kernel,
        grid=(n_seq, c // cb),
        in_specs=[pl.BlockSpec((3, l, cb), lambda b, j: (0, first_seq + b, j)),
                  pl.BlockSpec((2, 2, nfp, cb), lambda b, j: (0, 0, 0, j)),
                  _const((nfp, l)), _const((nfp, l)), _const((nfp, l)), _const((nfp, l)),
                  _const((l, nfp)), _const((l, nfp)), _const((l, nfp)), _const((l, nfp))],
        out_specs=pl.BlockSpec((l, cb), lambda b, j: (b, j)),
        out_shape=jax.ShapeDtypeStruct((n_seq * l, c), F32),
        compiler_params=_params("arbitrary", "arbitrary"),
        name="hy_conv",
    )(hy, g, *mats)


def _mix_post_kernel(y_ref, gb_ref, z_ref, x_ref, mod_ref, vec_ref, wout_ref, nf_ref, hs_ref, x1_ref, h2_ref):
    d = x_ref.shape[1]
    dr = hs_ref.shape[0]
    hs = hs_ref[...]
    vec = vec_ref[...]
    mod = mod_ref[0]
    gate1 = mod[:, 2 * d:3 * d]
    shift2 = mod[:, 3 * d:4 * d]
    scale2 = mod[:, 4 * d:5 * d]
    y = y_ref[...]
    mu = _headsum(y, hs) * (1.0 / HEAD)
    yc = y - mu
    var = _headsum(yc * yc, hs) * (1.0 / HEAD)
    yn = yc * lax.rsqrt(var + GN_EPS) * vec[0:1] + vec[1:2]
    ya = (yn + gb_ref[1]) * gb_ref[0]
    z = z_ref[...]
    yb = z * lax.rsqrt(jnp.mean(z * z, axis=-1, keepdims=True) + EPS) * vec[2:3]
    m = _dot(ya.astype(BF16), wout_ref[0:dr]) + _dot(yb.astype(BF16), wout_ref[dr:])
    x1 = x_ref[...] + gate1 * m
    x1_ref[...] = x1
    xn = x1 * lax.rsqrt(jnp.mean(x1 * x1, axis=-1, keepdims=True) + EPS) * nf_ref[...]
    h2_ref[...] = xn * (1.0 + scale2) + shift2


def _mix_post(ysum, gb, z, x, mods, n_ctx_tiles, tiles_per_lat, tm, vecs, w_out, norm_ffn, hs):
    t, d = x.shape
    dr = hs.shape[0]

    def mod_idx(i):
        return (jnp.where(i < n_ctx_tiles, 0, 1 + (i - n_ctx_tiles) // tiles_per_lat), 0, 0)

    tok = pl.BlockSpec((tm, dr), lambda i: (i, 0))
    tokd = pl.BlockSpec((tm, d), lambda i: (i, 0))
    return pl.pallas_call(
        _mix_post_kernel,
        grid=(t // tm,),
        in_specs=[tok, pl.BlockSpec((2, tm, dr), lambda i: (0, i, 0)), tok, tokd,
                  pl.BlockSpec((1, 1, mods.shape[-1]), mod_idx),
                  _full(vecs.shape), _full(w_out.shape), _full(norm_ffn.shape), _full(hs.shape)],
        out_specs=[tokd, tokd],
        out_shape=[jax.ShapeDtypeStruct((t, d), F32), jax.ShapeDtypeStruct((t, d), F32)],
        compiler_params=_params("arbitrary"),
        name="mix_post",
    )(ysum, gb, z, x, mods, vecs, w_out, norm_ffn, hs)


def _second_level_pairs():
    return [(a, b) for a in range(TOPK) for b in range(TOPK) if (a + 1) * (b + 1) <= TOPK]


def _route_kernel(h2_ref, wq_ref, keys_ref, r2_ref, b1_ref, e1_ref, e2_ref, sv_scr, s_scr, rk_scr, cnt_scr):
    nh = keys_ref.shape[1]
    nk = keys_ref.shape[2]
    half = keys_ref.shape[3]
    tt = h2_ref.shape[0]
    q = _dot(h2_ref[...].astype(BF16), wq_ref[...])
    key_id = lax.broadcasted_iota(jnp.int32, (nk, LANES), 0).astype(F32)
    neg = jnp.float32(-jnp.inf)

    for h in range(nh):
        for p in range(2):
            c0 = (h * 2 + p) * half
            qh, ql = _split(q[:, c0:c0 + half])
            kh, kl = _split(keys_ref[p, h])
            dims = (((1,), (1,)), ((), ()))
            s = (lax.dot_general(kh, qh, dims, preferred_element_type=F32)
                 + (lax.dot_general(kl, qh, dims, preferred_element_type=F32)
                    + lax.dot_general(kh, ql, dims, preferred_element_type=F32)))
            s_scr[p, h] = s

    nchunk = tt // LANES

    def first_level_quick():
        def chunk(c, ranked):
            col = pl.ds(pl.multiple_of(c * LANES, LANES), LANES)
            for h in range(nh):
                def extract(r, carry, h=h):
                    out = []
                    for p in range(2):
                        m_prev, above = carry[p]
                        s = s_scr[p, h, :, col]
                        below = s < m_prev
                        m = jnp.max(jnp.where(below, s, neg), axis=0, keepdims=True)
                        sv_scr[p, r, h:h + 1, col] = m
                        out.append((m, above + jnp.where(below, 1.0, 0.0)))
                    return tuple(out)

                start = (jnp.full((1, LANES), jnp.inf, F32), jnp.zeros((nk, LANES), F32))
                done = lax.fori_loop(0, TOPK, extract, (start, start))
                for p in range(2):
                    m_last, above = done[p]
                    rank = above - 1.0 + jnp.where(s_scr[p, h, :, col] < m_last, 1.0, 0.0)
                    rk_scr[p, h, :, col] = rank
                    ranked = jnp.maximum(ranked, jnp.sum(jnp.where(rank < float(TOPK), 1.0, 0.0),
                                                         axis=0, keepdims=True))
            return ranked

        return lax.fori_loop(0, nchunk, chunk, jnp.zeros((1, LANES), F32))

    def first_level_exact():
        def chunk(c, carry):
            col = pl.ds(pl.multiple_of(c * LANES, LANES), LANES)
            for h in range(nh):
                for p in range(2):
                    def extract(r, xr, p=p, h=h):
                        x, rank = xr
                        m = jnp.max(x, axis=0, keepdims=True)
                        first = jnp.min(jnp.where(x == m, key_id, float(nk)), axis=0, keepdims=True)
                        sel = key_id == first
                        sv_scr[p, r, h:h + 1, col] = m
                        return jnp.where(sel, neg, x), jnp.where(sel, lax.convert_element_type(r, F32), rank)

                    _, rank = lax.fori_loop(0, TOPK, extract,
                                            (s_scr[p, h, :, col], jnp.full((nk, LANES), float(TOPK), F32)))
                    rk_scr[p, h, :, col] = rank
            return carry

        lax.fori_loop(0, nchunk, chunk, 0)

    ranked = first_level_quick()

    @pl.when(jnp.max(ranked) > float(TOPK))
    def _():
        first_level_exact()

    pairs = _second_level_pairs()

    def second_level(c, carry):
        col = pl.ds(pl.multiple_of(c * LANES, LANES), LANES)
        v1 = [sv_scr[0, a, :, col] for a in range(TOPK)]
        v2 = [sv_scr[1, b, :, col] for b in range(TOPK)]
        top = v1[0] + v2[0]

        def select(_, carry):
            cand, cnt, zsum = list(carry[0]), list(carry[1]), carry[2]
            m = cand[0]
            for cv in cand[1:]:
                m = jnp.maximum(m, cv)
            zsum = zsum + jnp.exp(m - top)
            found = jnp.zeros((nh, LANES), F32)
            for ci, (a, _b) in enumerate(pairs):
                eq = jnp.where(cand[ci] == m, 1.0, 0.0)
                hit = eq * (1.0 - found)
                found = jnp.maximum(found, eq)
                cand[ci] = jnp.where(hit > 0.0, neg, cand[ci])
                cnt[a] = cnt[a] + hit
            return tuple(cand), tuple(cnt), zsum

        zero = jnp.zeros((nh, LANES), F32)
        _, cnt, zsum = lax.fori_loop(0, TOPK, select,
                                     (tuple(v1[a] + v2[b] for a, b in pairs), (zero,) * TOPK, zero))
        for a in range(TOPK):
            cnt_scr[a, :, col] = cnt[a]
        cnt_scr[TOPK, :, col] = 1.0 / zsum
        return carry

    lax.fori_loop(0, nchunk, second_level, 0)

    for h in range(nh):
        rank1 = rk_scr[0, h]
        b1 = jnp.zeros((nk, tt), F32)
        for a in range(TOPK):
            b1 = b1 + jnp.where(rank1 == float(a), cnt_scr[a, h:h + 1, :], 0.0)
        b1_ref[h] = b1
        r2_ref[h] = rk_scr[1, h].astype(BF16)
        e1_ref[h] = jnp.exp(s_scr[0, h] - sv_scr[0, 0, h:h + 1, :]) * cnt_scr[TOPK, h:h + 1, :]
        e2_ref[h] = jnp.exp(s_scr[1, h] - sv_scr[1, 0, h:h + 1, :]).astype(BF16)


def _route(h2, wq, keys, tt):
    t, d = h2.shape
    _, nh, nk, _ = keys.shape
    out = jax.ShapeDtypeStruct((nh, nk, t), F32)
    out16 = jax.ShapeDtypeStruct((nh, nk, t), BF16)
    ospec = pl.BlockSpec((nh, nk, tt), lambda i: (0, 0, i))
    return pl.pallas_call(
        _route_kernel,
        grid=(t // tt,),
        in_specs=[pl.BlockSpec((tt, d), lambda i: (i, 0)), _full(wq.shape), _full(keys.shape)],
        out_specs=[ospec, ospec, ospec, ospec],
        out_shape=[out16, out, out, out16],
        scratch_shapes=[pltpu.VMEM((2, TOPK, nh, tt), F32),
                        pltpu.VMEM((2, nh, nk, tt), F32),
                        pltpu.VMEM((2, nh, nk, tt), F32),
                        pltpu.VMEM((TOPK + 1, nh, tt), F32)],
        compiler_params=_params("arbitrary"),
        name="route",
    )(h2, wq, keys)


def _peer_kernel(h2t_ref, u_ref, vt_ref, r2_ref, e2_ref, b1_ref, e1_ref, x1_ref, mod_ref, fn_ref,
                 o_ref, acc_scr, a_scr):
    d = x1_ref.shape[1]
    nh, nk, tt = r2_ref.shape
    n1 = b1_ref.shape[1]
    eb = pl.program_id(1)

    @pl.when(eb == 0)
    def _():
        acc_scr[...] = jnp.zeros_like(acc_scr)

    hid = _dot(u_ref[...], h2t_ref[...])
    pk = 2 * SUBLANES
    for i in range(n1):
        w = jnp.zeros((nk // pk, pk, tt), BF16)
        for h in range(nh):
            b1r = jnp.broadcast_to(b1_ref[h, i:i + 1, :], (pk, tt)).astype(BF16)[None]
            e1r = jnp.broadcast_to(e1_ref[h, i:i + 1, :], (pk, tt)).astype(BF16)[None]
            r2 = r2_ref[h].reshape(nk // pk, pk, tt)
            e2 = e2_ref[h].reshape(nk // pk, pk, tt)
            w = w + jnp.where(r2 < b1r, e1r * e2, jnp.zeros((), BF16))
        hi = hid[i * nk:(i + 1) * nk]
        act = 0.5 * hi * (1.0 + lax.erf(hi * (1.0 / math.sqrt(2.0))))
        a_scr[i * nk:(i + 1) * nk, :] = (act.astype(BF16).reshape(nk // pk, pk, tt) * w).reshape(nk, tt)
    acc_scr[...] += _dot(vt_ref[...], a_scr[...])

    @pl.when(eb == pl.num_programs(1) - 1)
    def _():
        gate2 = mod_ref[0][:, 5 * d:6 * d]
        xo = x1_ref[...] + gate2 * acc_scr[...].T
        o_ref[...] = xo * lax.rsqrt(jnp.mean(xo * xo, axis=-1, keepdims=True) + EPS) * fn_ref[...]


def _peer(h2t, u, vt, r2, e2, b1, e1, x1, mods, final_norm, n_ctx_tiles, tiles_per_lat, tt, n1):
    d, t = h2t.shape
    ne = u.shape[0]
    nh, nk, _ = r2.shape
    eb = n1 * nk

    def mod_idx(i, e):
        return (jnp.where(i < n_ctx_tiles, 0, 1 + (i - n_ctx_tiles) // tiles_per_lat), 0, 0)

    dense = pl.BlockSpec((nh, nk, tt), lambda i, e: (0, 0, i))
    rows = pl.BlockSpec((nh, n1, tt), lambda i, e: (0, e, i))
    return pl.pallas_call(
        _peer_kernel,
        grid=(t // tt, ne // eb),
        in_specs=[pl.BlockSpec((d, tt), lambda i, e: (0, i)),
                  pl.BlockSpec((eb, d), lambda i, e: (e, 0)),
                  pl.BlockSpec((d, eb), lambda i, e: (0, e)),
                  dense, dense, rows, rows,
                  pl.BlockSpec((tt, d), lambda i, e: (i, 0)),
                  pl.BlockSpec((1, 1, mods.shape[-1]), mod_idx),
                  _full(final_norm.shape)],
        out_specs=pl.BlockSpec((tt, d), lambda i, e: (i, 0)),
        out_shape=jax.ShapeDtypeStruct((t, d), F32),
        scratch_shapes=[pltpu.VMEM((d, tt), F32), pltpu.VMEM((eb, tt), BF16)],
        compiler_params=_params("arbitrary", "arbitrary"),
        name="peer",
    )(h2t, u, vt, r2, e2, b1, e1, x1, mods, final_norm)


def _hyena_features(l, bands):
    t = jnp.linspace(0.0, 1.0, l, dtype=F32)[:, None]
    wpos = 2.0 * math.pi * jnp.arange(l, dtype=F32)[:, None] / l
    f = jnp.linspace(1e-4, bands - 1, bands, dtype=F32)[None, :]
    return jnp.concatenate([t, jnp.cos(f * wpos), -jnp.sin(f * wpos)], axis=-1)


def kernel(x_prompt, x_sample, state_rwkv, c, c_ctx, w_ada, b_ada, norm_mix, norm_ffn, w_in, conv_w, w_out, rwkv_w0, rwkv_w1, rwkv_w2, rwkv_a0, rwkv_a1, rwkv_a2, rwkv_g1, rwkv_g2, rwkv_k_k, rwkv_k_a, rwkv_r_k, rwkv_ln_w, rwkv_ln_b, hy_f_w1, hy_f_b1, hy_f_w2, hy_f_b2, hy_freq, hy_f_w3, hy_bias, hy_norm, peer_wq, peer_keys, peer_u, peer_v, final_norm):
    bc, lc, d = x_prompt.shape
    bl, ll, _ = x_sample.shape
    depth = w_ada.shape[0]
    assert depth == 1
    dr = rwkv_w0.shape[-1]
    dh = hy_norm.shape[-1]
    nh = dr // HEAD
    assert dr == dh and w_in.shape[-1] == 3 * dr + 3 * dh
    tc, tl = bc * lc, bl * ll
    t = tc + tl
    tm = lc
    assert ll % tm == 0 and tm % GRID_W == 0 and tc % ll == 0
    n_ctx_tiles = tc // tm
    tiles_per_lat = ll // tm
    parts = LANES // (bl * (nh // 2))
    assert parts * bl * (nh // 2) == LANES and HEAD % parts == 0 and tc % tl == 0
    nil = HEAD // parts
    l = 0

    rows = jnp.concatenate([c_ctx[None, :], c, jnp.zeros((SUBLANES - 1 - bl, d), F32)], axis=0)
    mods = _ada(rows, w_ada[l], b_ada[l]).reshape(SUBLANES, 1, 6 * d)

    x = jnp.concatenate([x_prompt.reshape(tc, d), x_sample.reshape(tl, d)], axis=0)

    def pad_cols(w):
        return jnp.pad(w, ((0, 0), (0, LANES - w.shape[1])))

    def pad_rows(w):
        return jnp.pad(w, ((0, LANES - w.shape[0]), (0, 0)))

    lora1 = jnp.concatenate([pad_cols(rwkv_w1[l, 0]), pad_cols(rwkv_w1[l, 1]), pad_cols(rwkv_a1[l, 0]),
                             pad_cols(rwkv_a1[l, 1]), pad_cols(rwkv_g1[l])], axis=1).astype(BF16)
    w2 = jnp.stack([pad_rows(rwkv_w2[l, 0]), pad_rows(rwkv_w2[l, 1])])
    a2 = jnp.stack([pad_rows(rwkv_a2[l, 0]), pad_rows(rwkv_a2[l, 1])])
    vecs_pre = jnp.stack([rwkv_w0[l, 0], rwkv_w0[l, 1], rwkv_a0[l, 0], rwkv_a0[l, 1], rwkv_k_k[l], rwkv_k_a[l],
                          rwkv_r_k[l, 0].reshape(dr), rwkv_r_k[l, 1].reshape(dr)])
    head_id = jnp.arange(dr, dtype=jnp.int32) // HEAD
    hs = (head_id[:, None] == head_id[None, :]).astype(BF16)
    shared, perdir, gb, hy = _mix_pre(x, mods, n_ctx_tiles, tiles_per_lat, tm, norm_mix[l][None, :],
                                      w_in[l].astype(BF16), conv_w[l], lora1, w2, a2, rwkv_g2[l], vecs_pre, hs)

    shared4 = shared.reshape(3, t // lc, lc, dr)
    perdir4 = perdir.reshape(2, 3, t // lc, lc, dr)
    y_cf, s_cf = _wkv_ctx(shared4, perdir4, bc, 16, False)
    y_cb, s_cb = _wkv_ctx(shared4, perdir4, bc, 16, True)
    shared_l = shared.reshape(3, t // ll, ll, dr)
    perdir_l = perdir.reshape(2, 3, t // ll, ll, dr)
    s0 = state_rwkv[:, l].reshape(bl, 2, nh // 2, 2, nil, parts, HEAD).transpose(1, 3, 4, 6, 2, 0, 5)
    s0 = s0.reshape(2, 2 * nil, HEAD, LANES)
    y_lf = _wkv_lat(shared_l, perdir_l, s0[0], tc // tl, bl, 16, False)
    y_lb = _wkv_lat(shared_l, perdir_l, s0[1], tc // tl, bl, 16, True)
    ys_ctx = (y_cf + y_cb).reshape(tc, dr)
    ys_lat = (y_lf + y_lb).reshape(tl, dr)
    ysum = jnp.concatenate([ys_ctx, ys_lat], axis=0)
    new_state = jnp.stack([s_cf, s_cb]).reshape(2, 2, HEAD, HEAD, nh // 2, bc).transpose(5, 0, 4, 1, 2, 3)
    new_state = new_state.reshape(bc, 1, 2, nh, HEAD, HEAD)

    bands = (hy_f_w1.shape[1] - 1) // 2
    femb = hy_f_w1.shape[1]
    fpad = 32
    w1p = jnp.pad(hy_f_w1[l], ((0, fpad - femb), (0, 0)))
    hidden = hy_f_w2.shape[-1]
    w3 = hy_f_w3[l].reshape(hidden, 4, dh).transpose(1, 0, 2)
    max_decay = math.log(1e-2) / 0.3
    min_decay = math.log(1e-2) / 1.5
    deltas = jnp.linspace(min_decay, max_decay, dh, dtype=F32)[None, :]
    z_parts = []
    for seq_len, first_seq, n_seq in ((lc, 0, bc), (ll, tc // ll, bl)):
        mats = _dft_mats(seq_len)
        zf = jnp.pad(_hyena_features(seq_len, bands), ((0, 0), (0, fpad - femb)))
        g = _hy_filter(zf, w1p, hy_f_b1[l][None, :], hy_f_w2[l], hy_f_b2[l][None, :], hy_freq[l], w3,
                       deltas, hy_bias[l], mats)
        z_parts.append(_hy_conv(hy, g, mats, seq_len, first_seq, n_seq))
    zhy = jnp.concatenate(z_parts, axis=0)

    vecs_post = jnp.concatenate([jnp.stack([rwkv_ln_w[l], rwkv_ln_b[l], hy_norm[l]]),
                                 jnp.zeros((SUBLANES - 3, dr), F32)], axis=0)
    x1, h2 = _mix_post(ysum, gb, zhy, x, mods, n_ctx_tiles, tiles_per_lat, tm, vecs_post,
                       w_out[l].astype(BF16), norm_ffn[l][None, :], hs)

    r2, b1, e1, e2 = _route(h2, peer_wq[l].astype(BF16), peer_keys[l], tt=512)
    tt = 512
    assert ll % tt == 0 and tc % tt == 0
    out = _peer(h2.astype(BF16).T, peer_u[l].astype(BF16), peer_v[l].astype(BF16).T, r2, e2, b1, e1, x1, mods,
                final_norm[None, :], tc // tt, ll // tt, tt, n1=SUBLANES)
    return out[:tc].reshape(bc, lc, d), out[tc:].reshape(bl, ll, d), new_state
```

```python
import functools
import math

import jax
import jax.numpy as jnp
import numpy as np
from jax import lax
from jax.experimental import pallas as pl
from jax.experimental.pallas import tpu as pltpu

F32 = jnp.float32
BF16 = jnp.bfloat16

GRID_W = 64
HEAD = 64
TOPK = 16
EPS = 1e-6
GN_EPS = 64e-5
LANES = 128
SUBLANES = 8
VMEM_LIMIT = 56 * 1024 * 1024


def _params(*sem):
    return pltpu.CompilerParams(dimension_semantics=sem, vmem_limit_bytes=VMEM_LIMIT)


def _split(x):
    hi = x.astype(BF16)
    lo = (x - hi.astype(F32)).astype(BF16)
    return hi, lo


def _dot(a, b):
    return jnp.dot(a, b, preferred_element_type=F32)


def _dot_split(ah, al, bh, bl):
    return _dot(ah, bh) + (_dot(al, bh) + _dot(ah, bl))


def _dot3(a, b):
    ah, al = _split(a)
    bh, bl = _split(b)
    return _dot_split(ah, al, bh, bl)


def _full(shape):
    n = len(shape)
    return pl.BlockSpec(shape, lambda *_: (0,) * n)


def _two_part(block, n_first):
    return (pl.BlockSpec(block, lambda i, *_: (jnp.minimum(i, n_first - 1), 0)),
            pl.BlockSpec(block, lambda i, *_: (jnp.maximum(i - n_first, 0), 0)))


def _const(shape):
    n = len(shape)
    return pl.BlockSpec(shape, lambda *_: (0,) * n, pipeline_mode=pl.Buffered(1))


def _ada_kernel(c_ref, w_ref, b_ref, o_ref):
    c = c_ref[...]
    s = c * jax.nn.sigmoid(c)
    o_ref[...] = _dot3(s, w_ref[...]) + b_ref[...]


def _ada(c_rows, w, b):
    rows, d = c_rows.shape
    n = w.shape[1]
    bn = n // 4
    return pl.pallas_call(
        _ada_kernel,
        grid=(4,),
        in_specs=[_full((rows, d)),
                  pl.BlockSpec((d, bn), lambda j: (0, j)),
                  pl.BlockSpec((1, bn), lambda j: (0, j))],
        out_specs=pl.BlockSpec((rows, bn), lambda j: (0, j)),
        out_shape=jax.ShapeDtypeStruct((rows, n), F32),
        compiler_params=_params("arbitrary"),
        name="ada",
    )(c_rows, w, b.reshape(1, n))


def _headsum(z, hs):
    zh, zl = _split(z)
    return _dot(zh, hs) + _dot(zl, hs)


def _mix_pre_kernel(n_ctx_tiles, xp_ref, xs_ref, mod_ref, nm_ref, win_ref, cw_ref, l1_ref, w2_ref, a2_ref,
                    g2_ref, vec_ref, hs_ref, sh_ref, dr_ref, gb_ref, hy_ref):
    tm, d = xp_ref.shape
    dr = hs_ref.shape[0]
    is_ctx = pl.program_id(0) < n_ctx_tiles
    x = jnp.where(is_ctx, xp_ref[...], xs_ref[...])
    mod = mod_ref[0]
    shift1 = mod[:, 0:d]
    scale1 = mod[:, d:2 * d]
    xn = x * lax.rsqrt(jnp.mean(x * x, axis=-1, keepdims=True) + EPS) * nm_ref[...]
    hb = (xn * (1.0 + scale1) + shift1).astype(BF16)

    pos = lax.broadcasted_iota(jnp.int32, (tm, dr), 0)
    row = pos % GRID_W
    keep_prev = jnp.where(is_ctx, jnp.where(pos == 0, 0.0, 1.0), jnp.where(row == 0, 0.0, 1.0))
    keep_next = jnp.where(is_ctx, jnp.where(pos == tm - 1, 0.0, 1.0),
                          jnp.where(row == GRID_W - 1, 0.0, 1.0))

    def proj(c):
        u = _dot(hb, win_ref[:, c * dr:(c + 1) * dr])
        cw = cw_ref[:, c * dr:(c + 1) * dr]
        up = pltpu.roll(u, 1, 0) * keep_prev
        un = pltpu.roll(u, tm - 1, 0) * keep_next
        return up * cw[0:1] + u * cw[1:2] + un * cw[2:3]

    r = proj(0)
    k = proj(1)
    v = proj(2)
    hy_ref[0] = proj(3)
    hy_ref[1] = proj(4)
    hy_ref[2] = proj(5)

    lo = _dot(hb, l1_ref[...])
    vec = vec_ref[...]
    k_k = vec[4:5]
    k_a = vec[5:6]
    hs = hs_ref[...]
    kkr = k * k_k
    kk = kkr * lax.rsqrt(jnp.maximum(_headsum(kkr * kkr, hs), 1e-24))
    sh_ref[0] = r
    sh_ref[1] = kk
    sh_ref[2] = v
    gb_ref[0] = _dot3(jax.nn.sigmoid(lo[:, 4 * LANES:5 * LANES]), g2_ref[...])
    bon = jnp.zeros_like(r)
    for dd in range(2):
        w = vec[dd:dd + 1] + _dot3(jnp.tanh(lo[:, dd * LANES:(dd + 1) * LANES]), w2_ref[dd])
        z = -w
        softplus = jnp.maximum(z, 0.0) + jnp.log1p(jnp.exp(-jnp.abs(z)))
        dr_ref[dd, 0] = jnp.exp(-jnp.exp(-softplus - 0.5))
        a = jax.nn.sigmoid(vec[2 + dd:3 + dd]
                           + _dot3(lo[:, (2 + dd) * LANES:(3 + dd) * LANES], a2_ref[dd]))
        kd = k * (1.0 + (a - 1.0) * k_a)
        dr_ref[dd, 1] = kd
        dr_ref[dd, 2] = kk * a
        bon = bon + r * kd * vec[6 + dd:7 + dd]
    gb_ref[1] = _headsum(bon, hs) * v


def _mix_pre(xp, xs, mods, n_ctx_tiles, tiles_per_lat, tm, norm_mix, w_in, conv_w, lora1, w2, a2, g2, vecs, hs):
    d = xp.shape[1]
    t = xp.shape[0] + xs.shape[0]
    dr = hs.shape[0]
    n_tiles = t // tm

    def mod_idx(i):
        return (jnp.where(i < n_ctx_tiles, 0, 1 + (i - n_ctx_tiles) // tiles_per_lat), 0, 0)

    return pl.pallas_call(
        functools.partial(_mix_pre_kernel, n_ctx_tiles),
        grid=(n_tiles,),
        in_specs=[*_two_part((tm, d), n_ctx_tiles),
                  pl.BlockSpec((1, 1, mods.shape[-1]), mod_idx),
                  _full(norm_mix.shape), _full(w_in.shape), _full(conv_w.shape), _full(lora1.shape),
                  _full(w2.shape), _full(a2.shape), _full(g2.shape), _full(vecs.shape), _full(hs.shape)],
        out_specs=[pl.BlockSpec((3, tm, dr), lambda i: (0, i, 0)),
                   pl.BlockSpec((2, 3, tm, dr), lambda i: (0, 0, i, 0)),
                   pl.BlockSpec((2, tm, dr), lambda i: (0, i, 0)),
                   pl.BlockSpec((3, tm, dr), lambda i: (0, i, 0))],
        out_shape=[jax.ShapeDtypeStruct((3, t, dr), F32),
                   jax.ShapeDtypeStruct((2, 3, t, dr), F32),
                   jax.ShapeDtypeStruct((2, t, dr), F32),
                   jax.ShapeDtypeStruct((3, t, dr), F32)],
        compiler_params=_params("arbitrary"),
        name="mix_pre",
    )(xp, xs, mods, norm_mix, w_in, conv_w, lora1, w2, a2, g2, vecs, hs)


def _wkv_ctx_kernel(reverse, sh_ref, pd_ref, y_ref, sfin_ref, s_scr, tr_scr, yt_scr):
    nb, tb = sh_ref.shape[1], sh_ref.shape[2]
    nhp = sh_ref.shape[3] // LANES
    tblk = pl.program_id(0)

    @pl.when(tblk == 0)
    def _():
        s_scr[...] = jnp.zeros_like(s_scr)

    def to_lanes(ref, lead, t):
        blocks = [ref[lead + (slice(None), t, slice(hp * LANES, (hp + 1) * LANES))] for hp in range(nhp)]
        return jnp.concatenate(blocks, axis=0).T

    for tau in range(tb):
        t = tb - 1 - tau if reverse else tau
        r_t = to_lanes(sh_ref, (0,), t)
        d_t = to_lanes(pd_ref, (0, 0), t)
        tr_scr[0] = d_t * r_t
        tr_scr[1] = to_lanes(sh_ref, (1,), t)
        tr_scr[2] = to_lanes(sh_ref, (2,), t)
        tr_scr[3] = d_t
        k_t = to_lanes(pd_ref, (0, 1), t)
        tr_scr[4] = k_t
        kka_t = to_lanes(pd_ref, (0, 2), t)
        tr_scr[5] = kka_t
        for h2 in range(2):
            rows = slice(h2 * HEAD, (h2 + 1) * HEAD)
            c1 = jnp.sum(k_t[rows] * r_t[rows], axis=0, keepdims=True)
            c2 = jnp.sum(kka_t[rows] * r_t[rows], axis=0, keepdims=True)

            def rows8(i8, c, h2=h2, rows=rows, c1=c1, c2=c2):
                base = pl.multiple_of(h2 * HEAD + i8 * SUBLANES, SUBLANES)
                vblk = tr_scr[2, pl.ds(base, SUBLANES), :]
                outs = []
                for jj in range(SUBLANES):
                    s = s_scr[base + jj]
                    vi = vblk[jj:jj + 1]
                    sa = jnp.sum(s * tr_scr[1, rows, :], axis=0, keepdims=True)
                    q = jnp.sum(s * tr_scr[0, rows, :], axis=0, keepdims=True)
                    s_scr[base + jj] = s * tr_scr[3, rows, :] + (vi * tr_scr[4, rows, :] - sa * tr_scr[5, rows, :])
                    outs.append(q + vi * c1 - sa * c2)
                yt_scr[pl.ds(base, SUBLANES), :] = jnp.concatenate(outs, axis=0)
                return c

            lax.fori_loop(0, HEAD // SUBLANES, rows8, 0)
        y = yt_scr[...].T
        for hp in range(nhp):
            y_ref[:, t, hp * LANES:(hp + 1) * LANES] = y[hp * nb:(hp + 1) * nb]

    @pl.when(tblk == pl.num_programs(0) - 1)
    def _():
        sfin_ref[...] = s_scr[...]


def _wkv_ctx(shared, perdir, nb, tb, reverse):
    _, nseq, lc, dr = shared.shape
    assert nb * (dr // LANES) == LANES and 2 * HEAD == LANES
    nt = lc // tb
    di = 1 if reverse else 0

    def tsel(ti):
        return nt - 1 - ti if reverse else ti

    return pl.pallas_call(
        functools.partial(_wkv_ctx_kernel, reverse),
        grid=(nt,),
        in_specs=[pl.BlockSpec((3, nb, tb, dr), lambda ti: (0, 0, tsel(ti), 0)),
                  pl.BlockSpec((1, 3, nb, tb, dr), lambda ti: (di, 0, 0, tsel(ti), 0))],
        out_specs=[pl.BlockSpec((nb, tb, dr), lambda ti: (0, tsel(ti), 0)),
                   pl.BlockSpec((LANES, HEAD, LANES), lambda ti: (0, 0, 0))],
        out_shape=[jax.ShapeDtypeStruct((nb, lc, dr), F32),
                   jax.ShapeDtypeStruct((LANES, HEAD, LANES), F32)],
        scratch_shapes=[pltpu.VMEM((LANES, HEAD, LANES), F32),
                        pltpu.VMEM((6, LANES, LANES), F32),
                        pltpu.VMEM((LANES, LANES), F32)],
        compiler_params=_params("arbitrary"),
        name="wkv_ctx",
    )(shared, perdir)


def _wkv_lat_kernel(shf_ref, shb_ref, pdf_ref, pdb_ref, s0_ref, yf_ref, yb_ref, s_scr):
    nb, tb, dr = shf_ref.shape[1], shf_ref.shape[2], shf_ref.shape[3]
    nhp = dr // LANES
    parts = LANES // (nhp * nb)
    nil = HEAD // parts
    tblk = pl.program_id(0)
    lane_part = lax.broadcasted_iota(jnp.int32, (parts, LANES), 1) % parts
    diag = jnp.where(lane_part == lax.broadcasted_iota(jnp.int32, (parts, LANES), 0), 1.0, 0.0)

    @pl.when(tblk == 0)
    def _():
        s_scr[...] = s0_ref[...]

    def to_lanes(ref, lead, t):
        blocks = [jnp.broadcast_to(ref[lead + (b, slice(t, t + 1), slice(hp * LANES, (hp + 1) * LANES))],
                                   (parts, LANES))
                  for hp in range(nhp) for b in range(nb)]
        return jnp.concatenate(blocks, axis=0).T

    def step(di, sh_ref, pd_ref, y_ref, t):
        r_t = to_lanes(sh_ref, (0,), t)
        kk_t = to_lanes(sh_ref, (1,), t)
        v_t = to_lanes(sh_ref, (2,), t)
        d_t = to_lanes(pd_ref, (0, 0), t)
        k_t = to_lanes(pd_ref, (0, 1), t)
        kka_t = to_lanes(pd_ref, (0, 2), t)
        placed = []
        for h2 in range(2):
            rows = slice(h2 * HEAD, (h2 + 1) * HEAD)
            rt, kkt, dt, kt, kkat = r_t[rows], kk_t[rows], d_t[rows], k_t[rows], kka_t[rows]
            drt = dt * rt
            c1 = jnp.sum(kt * rt, axis=0, keepdims=True)
            c2 = jnp.sum(kkat * rt, axis=0, keepdims=True)
            for il in range(nil):
                r0 = h2 * HEAD + il * parts
                vi = jnp.sum(v_t[r0:r0 + parts] * diag, axis=0, keepdims=True)
                s = s_scr[di, h2 * nil + il]
                sa = jnp.sum(s * kkt, axis=0, keepdims=True)
                q = jnp.sum(s * drt, axis=0, keepdims=True)
                s_scr[di, h2 * nil + il] = s * dt + (vi * kt - sa * kkat)
                placed.append((q + vi * c1 - sa * c2) * diag)
        y = jnp.concatenate(placed, axis=0).T
        for hp in range(nhp):
            for b in range(nb):
                r0 = (hp * nb + b) * parts
                y_ref[b, t:t + 1, hp * LANES:(hp + 1) * LANES] = jnp.sum(y[r0:r0 + parts], axis=0, keepdims=True)

    for tau in range(tb):
        step(0, shf_ref, pdf_ref, yf_ref, tau)
        step(1, shb_ref, pdb_ref, yb_ref, tb - 1 - tau)


def _wkv_lat(shared, perdir, s0, first_blk, nb, tb):
    _, nseq, ll, dr = shared.shape
    nt = ll // tb
    out = jax.ShapeDtypeStruct((nb, ll, dr), F32)
    return pl.pallas_call(
        _wkv_lat_kernel,
        grid=(nt,),
        in_specs=[pl.BlockSpec((3, nb, tb, dr), lambda ti: (0, first_blk, ti, 0)),
                  pl.BlockSpec((3, nb, tb, dr), lambda ti: (0, first_blk, nt - 1 - ti, 0)),
                  pl.BlockSpec((1, 3, nb, tb, dr), lambda ti: (0, 0, first_blk, ti, 0)),
                  pl.BlockSpec((1, 3, nb, tb, dr), lambda ti: (1, 0, first_blk, nt - 1 - ti, 0)),
                  _full(s0.shape)],
        out_specs=[pl.BlockSpec((nb, tb, dr), lambda ti: (0, ti, 0)),
                   pl.BlockSpec((nb, tb, dr), lambda ti: (0, nt - 1 - ti, 0))],
        out_shape=[out, out],
        scratch_shapes=[pltpu.VMEM(s0.shape, F32)],
        compiler_params=_params("arbitrary"),
        name="wkv_lat",
    )(shared, shared, perdir, perdir, s0)


def _dft_mats(l):
    n = 2 * l
    nfp = l + SUBLANES
    kf = np.arange(nfp, dtype=np.int64)
    ang = (2.0 * np.pi / n) * ((kf[:, None] * np.arange(l, dtype=np.int64)[None, :]) % n)
    valid = (kf <= l)[:, None]
    c = np.where(valid, np.cos(ang), 0.0)
    s = np.where(valid, np.sin(ang), 0.0)
    wk = np.where((kf == 0) | (kf == l), 1.0 / n, 2.0 / n)[None, :]
    mats = (c, s, c.T * wk, -(s.T) * wk)
    return [m for mat in mats for m in _split(jnp.asarray(mat.astype(np.float32)))]


def _hy_filter_kernel(zf_ref, w1_ref, b1_ref, w2_ref, b2_ref, fq_ref, w3_ref, dl_ref, bias_ref,
                      ch_ref, cl_ref, sh_ref, sl_ref, g_ref):
    zf = zf_ref[...]
    hid = jnp.sin(fq_ref[0:1] * (_dot3(zf, w1_ref[...]) + b1_ref[...]))
    hid = jnp.sin(fq_ref[1:2] * (_dot3(hid, w2_ref[...]) + b2_ref[...]))
    win = jnp.exp(-zf[:, 0:1] * jnp.abs(dl_ref[...]))
    hh, hl = _split(hid)
    filt = []
    for o in range(4):
        wh, wl = _split(w3_ref[o])
        filt.append(_dot_split(hh, hl, wh, wl) * win)
    for od in range(2):
        hf, hbk = filt[2 * od], filt[2 * od + 1]
        ph, plo = _split(hf + hbk)
        mh, ml = _split(hbk - hf)
        g_ref[od, 0] = _dot_split(ch_ref[...], cl_ref[...], ph, plo) + bias_ref[od:od + 1]
        g_ref[od, 1] = _dot_split(sh_ref[...], sl_ref[...], mh, ml)


def _hy_filter(zf, w1, b1, w2, b2, fq, w3, deltas, bias, mats):
    l = zf.shape[0]
    nfp = mats[0].shape[0]
    c = w3.shape[-1]
    cb = LANES
    return pl.pallas_call(
        _hy_filter_kernel,
        grid=(c // cb,),
        in_specs=[_full(zf.shape), _full(w1.shape), _full(b1.shape), _full(w2.shape), _full(b2.shape),
                  _full(fq.shape),
                  pl.BlockSpec((4, w3.shape[1], cb), lambda j: (0, 0, j)),
                  pl.BlockSpec((1, cb), lambda j: (0, j)),
                  pl.BlockSpec((2, cb), lambda j: (0, j)),
                  _const((nfp, l)), _const((nfp, l)), _const((nfp, l)), _const((nfp, l))],
        out_specs=pl.BlockSpec((2, 2, nfp, cb), lambda j: (0, 0, 0, j)),
        out_shape=jax.ShapeDtypeStruct((2, 2, nfp, c), F32),
        compiler_params=_params("arbitrary"),
        name="hy_filt",
    )(zf, w1, b1, w2, b2, fq, w3, deltas, bias, *mats[:4])


def _hy_conv_kernel(hy_ref, g_ref, ch_ref, cl_ref, sh_ref, sl_ref, cih_ref, cil_ref, sih_ref, sil_ref, o_ref):
    def conv(u, od):
        uh, ul = _split(u)
        re = _dot_split(ch_ref[...], cl_ref[...], uh, ul)
        im = -_dot_split(sh_ref[...], sl_ref[...], uh, ul)
        gre = g_ref[od, 0]
        gim = g_ref[od, 1]
        ah, al = _split(re * gre - im * gim)
        bh, bl = _split(re * gim + im * gre)
        return (_dot_split(cih_ref[...], cil_ref[...], ah, al)
                + _dot_split(sih_ref[...], sil_ref[...], bh, bl))

    z = hy_ref[1] * conv(hy_ref[0], 0)
    o_ref[...] = hy_ref[2] * conv(z, 1)


def _hy_conv(hy, g, mats, l, first_seq, n_seq):
    t, c = hy.shape[1], hy.shape[2]
    nfp = mats[0].shape[0]
    cb = 2 * LANES
    return pl.pallas_call(
        _hy_conv_kernel,
        grid=(n_seq, c // cb),
        in_specs=[pl.BlockSpec((3, l, cb), lambda b, j: (0, first_seq + b, j)),
                  pl.BlockSpec((2, 2, nfp, cb), lambda b, j: (0, 0, 0, j)),
                  _const((nfp, l)), _const((nfp, l)), _const((nfp, l)), _const((nfp, l)),
                  _const((l, nfp)), _const((l, nfp)), _const((l, nfp)), _const((l, nfp))],
        out_specs=pl.BlockSpec((l, cb), lambda b, j: (b, j)),
        out_shape=jax.ShapeDtypeStruct((n_seq * l, c), F32),
        compiler_params=_params("arbitrary", "arbitrary"),
        name="hy_conv",
    )(hy, g, *mats)


def _mix_post_kernel(n_ctx_tiles, ycf_ref, ylf_ref, ycb_ref, ylb_ref, gb_ref, zc_ref, zl_ref, xp_ref, xs_ref,
                     mod_ref, vec_ref, wout_ref, nf_ref, hs_ref, x1_ref, h2_ref):
    d = xp_ref.shape[1]
    dr = hs_ref.shape[0]
    is_ctx = pl.program_id(0) < n_ctx_tiles
    hs = hs_ref[...]
    vec = vec_ref[...]
    mod = mod_ref[0]
    gate1 = mod[:, 2 * d:3 * d]
    shift2 = mod[:, 3 * d:4 * d]
    scale2 = mod[:, 4 * d:5 * d]
    y = jnp.where(is_ctx, ycf_ref[...] + ycb_ref[...], ylf_ref[...] + ylb_ref[...])
    mu = _headsum(y, hs) * (1.0 / HEAD)
    yc = y - mu
    var = _headsum(yc * yc, hs) * (1.0 / HEAD)
    yn = yc * lax.rsqrt(var + GN_EPS) * vec[0:1] + vec[1:2]
    ya = (yn + gb_ref[1]) * gb_ref[0]
    z = jnp.where(is_ctx, zc_ref[...], zl_ref[...])
    yb = z * lax.rsqrt(jnp.mean(z * z, axis=-1, keepdims=True) + EPS) * vec[2:3]
    m = _dot(ya.astype(BF16), wout_ref[0:dr]) + _dot(yb.astype(BF16), wout_ref[dr:])
    x1 = jnp.where(is_ctx, xp_ref[...], xs_ref[...]) + gate1 * m
    x1_ref[...] = x1
    xn = x1 * lax.rsqrt(jnp.mean(x1 * x1, axis=-1, keepdims=True) + EPS) * nf_ref[...]
    h2_ref[...] = xn * (1.0 + scale2) + shift2


def _mix_post(y_fwd, y_bwd, gb, z, x, mods, n_ctx_tiles, tiles_per_lat, tm, vecs, w_out, norm_ffn, hs):
    d = x[0].shape[1]
    t = x[0].shape[0] + x[1].shape[0]
    dr = hs.shape[0]

    def mod_idx(i):
        return (jnp.where(i < n_ctx_tiles, 0, 1 + (i - n_ctx_tiles) // tiles_per_lat), 0, 0)

    tok = _two_part((tm, dr), n_ctx_tiles)
    tokd = pl.BlockSpec((tm, d), lambda i: (i, 0))
    return pl.pallas_call(
        functools.partial(_mix_post_kernel, n_ctx_tiles),
        grid=(t // tm,),
        in_specs=[*tok, *tok, pl.BlockSpec((2, tm, dr), lambda i: (0, i, 0)), *tok,
                  *_two_part((tm, d), n_ctx_tiles),
                  pl.BlockSpec((1, 1, mods.shape[-1]), mod_idx),
                  _full(vecs.shape), _full(w_out.shape), _full(norm_ffn.shape), _full(hs.shape)],
        out_specs=[tokd, tokd],
        out_shape=[jax.ShapeDtypeStruct((t, d), F32), jax.ShapeDtypeStruct((t, d), F32)],
        compiler_params=_params("arbitrary"),
        name="mix_post",
    )(*y_fwd, *y_bwd, gb, *z, *x, mods, vecs, w_out, norm_ffn, hs)


def _second_level_pairs():
    return [(a, b) for a in range(TOPK) for b in range(TOPK) if (a + 1) * (b + 1) <= TOPK]


def _route_kernel(h2_ref, wq_ref, keys_ref, r2_ref, b1_ref, e1_ref, e2_ref, sv_scr, s_scr, rk_scr, cnt_scr):
    nh = keys_ref.shape[1]
    nk = keys_ref.shape[2]
    half = keys_ref.shape[3]
    tt = h2_ref.shape[0]
    q = _dot(h2_ref[...].astype(BF16), wq_ref[...])
    key_id = lax.broadcasted_iota(jnp.int32, (nk, LANES), 0).astype(F32)
    neg = jnp.float32(-jnp.inf)

    for h in range(nh):
        for p in range(2):
            c0 = (h * 2 + p) * half
            qh, ql = _split(q[:, c0:c0 + half])
            kh, kl = _split(keys_ref[p, h])
            dims = (((1,), (1,)), ((), ()))
            s = (lax.dot_general(kh, qh, dims, preferred_element_type=F32)
                 + (lax.dot_general(kl, qh, dims, preferred_element_type=F32)
                    + lax.dot_general(kh, ql, dims, preferred_element_type=F32)))
            s_scr[p, h] = s

    nchunk = tt // LANES

    def first_level_quick():
        def chunk(c, ranked):
            col = pl.ds(pl.multiple_of(c * LANES, LANES), LANES)
            for h in range(nh):
                def extract(r, carry, h=h):
                    out = []
                    for p in range(2):
                        m_prev, above = carry[p]
                        s = s_scr[p, h, :, col]
                        below = s < m_prev
                        m = jnp.max(jnp.where(below, s, neg), axis=0, keepdims=True)
                        sv_scr[p, r, h:h + 1, col] = m
                        out.append((m, above + jnp.where(below, 1.0, 0.0)))
                    return tuple(out)

                start = (jnp.full((1, LANES), jnp.inf, F32), jnp.zeros((nk, LANES), F32))
                done = lax.fori_loop(0, TOPK, extract, (start, start))
                for p in range(2):
                    m_last, above = done[p]
                    rank = above - 1.0 + jnp.where(s_scr[p, h, :, col] < m_last, 1.0, 0.0)
                    rk_scr[p, h, :, col] = rank
                    ranked = jnp.maximum(ranked, jnp.sum(jnp.where(rank < float(TOPK), 1.0, 0.0),
                                                         axis=0, keepdims=True))
            return ranked

        return lax.fori_loop(0, nchunk, chunk, jnp.zeros((1, LANES), F32))

    def first_level_exact():
        def chunk(c, carry):
            col = pl.ds(pl.multiple_of(c * LANES, LANES), LANES)
            for h in range(nh):
                for p in range(2):
                    def extract(r, xr, p=p, h=h):
                        x, rank = xr
                        m = jnp.max(x, axis=0, keepdims=True)
                        first = jnp.min(jnp.where(x == m, key_id, float(nk)), axis=0, keepdims=True)
                        sel = key_id == first
                        sv_scr[p, r, h:h + 1, col] = m
                        return jnp.where(sel, neg, x), jnp.where(sel, lax.convert_element_type(r, F32), rank)

                    _, rank = lax.fori_loop(0, TOPK, extract,
                                            (s_scr[p, h, :, col], jnp.full((nk, LANES), float(TOPK), F32)))
                    rk_scr[p, h, :, col] = rank
            return carry

        lax.fori_loop(0, nchunk, chunk, 0)

    ranked = first_level_quick()

    @pl.when(jnp.max(ranked) > float(TOPK))
    def _():
        first_level_exact()

    pairs = _second_level_pairs()

    def second_level(c, carry):
        col = pl.ds(pl.multiple_of(c * LANES, LANES), LANES)
        v1 = [sv_scr[0, a, :, col] for a in range(TOPK)]
        v2 = [sv_scr[1, b, :, col] for b in range(TOPK)]
        top = v1[0] + v2[0]

        def select(_, carry):
            cand, cnt, zsum = list(carry[0]), list(carry[1]), carry[2]
            m = cand[0]
            for cv in cand[1:]:
                m = jnp.maximum(m, cv)
            zsum = zsum + jnp.exp(m - top)
            found = jnp.zeros((nh, LANES), F32)
            for ci, (a, _b) in enumerate(pairs):
                eq = jnp.where(cand[ci] == m, 1.0, 0.0)
                hit = eq * (1.0 - found)
                found = jnp.maximum(found, eq)
                cand[ci] = jnp.where(hit > 0.0, neg, cand[ci])
                cnt[a] = cnt[a] + hit
            return tuple(cand), tuple(cnt), zsum

        zero = jnp.zeros((nh, LANES), F32)
        _, cnt, zsum = lax.fori_loop(0, TOPK, select,
                                     (tuple(v1[a] + v2[b] for a, b in pairs), (zero,) * TOPK, zero))
        for a in range(TOPK):
            cnt_scr[a, :, col] = cnt[a]
        cnt_scr[TOPK, :, col] = 1.0 / zsum
        return carry

    lax.fori_loop(0, nchunk, second_level, 0)

    for h in range(nh):
        rank1 = rk_scr[0, h]
        b1 = jnp.zeros((nk, tt), F32)
        for a in range(TOPK):
            b1 = b1 + jnp.where(rank1 == float(a), cnt_scr[a, h:h + 1, :], 0.0)
        b1_ref[h] = b1
        r2_ref[h] = rk_scr[1, h].astype(BF16)
        e1_ref[h] = jnp.exp(s_scr[0, h] - sv_scr[0, 0, h:h + 1, :]) * cnt_scr[TOPK, h:h + 1, :]
        e2_ref[h] = jnp.exp(s_scr[1, h] - sv_scr[1, 0, h:h + 1, :]).astype(BF16)


def _route(h2, wq, keys, tt):
    t, d = h2.shape
    _, nh, nk, _ = keys.shape
    out = jax.ShapeDtypeStruct((nh, nk, t), F32)
    out16 = jax.ShapeDtypeStruct((nh, nk, t), BF16)
    ospec = pl.BlockSpec((nh, nk, tt), lambda i: (0, 0, i))
    return pl.pallas_call(
        _route_kernel,
        grid=(t // tt,),
        in_specs=[pl.BlockSpec((tt, d), lambda i: (i, 0)), _full(wq.shape), _full(keys.shape)],
        out_specs=[ospec, ospec, ospec, ospec],
        out_shape=[out16, out, out, out16],
        scratch_shapes=[pltpu.VMEM((2, TOPK, nh, tt), F32),
                        pltpu.VMEM((2, nh, nk, tt), F32),
                        pltpu.VMEM((2, nh, nk, tt), F32),
                        pltpu.VMEM((TOPK + 1, nh, tt), F32)],
        compiler_params=_params("arbitrary"),
        name="route",
    )(h2, wq, keys)


def _peer_kernel(n_ctx_tiles, h2t_ref, u_ref, vt_ref, r2_ref, e2_ref, b1_ref, e1_ref, x1_ref, mod_ref, fn_ref,
                 op_ref, os_ref, acc_scr, a_scr):
    d = x1_ref.shape[1]
    nh, nk, tt = r2_ref.shape
    n1 = b1_ref.shape[1]
    eb = pl.program_id(1)

    @pl.when(eb == 0)
    def _():
        acc_scr[...] = jnp.zeros_like(acc_scr)

    hid = _dot(u_ref[...].astype(BF16), h2t_ref[...])
    pk = 2 * SUBLANES
    for i in range(n1):
        w = jnp.zeros((nk // pk, pk, tt), BF16)
        for h in range(nh):
            b1r = jnp.broadcast_to(b1_ref[h, i:i + 1, :], (pk, tt)).astype(BF16)[None]
            e1r = jnp.broadcast_to(e1_ref[h, i:i + 1, :], (pk, tt)).astype(BF16)[None]
            r2 = r2_ref[h].reshape(nk // pk, pk, tt)
            e2 = e2_ref[h].reshape(nk // pk, pk, tt)
            w = w + jnp.where(r2 < b1r, e1r * e2, jnp.zeros((), BF16))
        hi = hid[i * nk:(i + 1) * nk]
        act = 0.5 * hi * (1.0 + lax.erf(hi * (1.0 / math.sqrt(2.0))))
        a_scr[i * nk:(i + 1) * nk, :] = (act.astype(BF16).reshape(nk // pk, pk, tt) * w).reshape(nk, tt)
    acc_scr[...] += _dot(vt_ref[...], a_scr[...])

    @pl.when(eb == pl.num_programs(1) - 1)
    def _():
        gate2 = mod_ref[0][:, 5 * d:6 * d]
        xo = x1_ref[...] + gate2 * acc_scr[...].T
        res = xo * lax.rsqrt(jnp.mean(xo * xo, axis=-1, keepdims=True) + EPS) * fn_ref[...]
        is_ctx = pl.program_id(0) < n_ctx_tiles

        @pl.when(is_ctx)
        def _():
            op_ref[...] = res

        @pl.when(jnp.logical_not(is_ctx))
        def _():
            os_ref[...] = res


def _peer(h2t, u, vt, r2, e2, b1, e1, x1, mods, final_norm, n_ctx_tiles, tiles_per_lat, tt, n1):
    d, t = h2t.shape
    ne = u.shape[0]
    nh, nk, _ = r2.shape
    eb = n1 * nk

    def mod_idx(i, e):
        return (jnp.where(i < n_ctx_tiles, 0, 1 + (i - n_ctx_tiles) // tiles_per_lat), 0, 0)

    dense = pl.BlockSpec((nh, nk, tt), lambda i, e: (0, 0, i))
    rows = pl.BlockSpec((nh, n1, tt), lambda i, e: (0, e, i))
    tc = n_ctx_tiles * tt
    return pl.pallas_call(
        functools.partial(_peer_kernel, n_ctx_tiles),
        grid=(t // tt, ne // eb),
        in_specs=[pl.BlockSpec((d, tt), lambda i, e: (0, i)),
                  pl.BlockSpec((eb, d), lambda i, e: (e, 0)),
                  pl.BlockSpec((d, eb), lambda i, e: (0, e)),
                  dense, dense, rows, rows,
                  pl.BlockSpec((tt, d), lambda i, e: (i, 0)),
                  pl.BlockSpec((1, 1, mods.shape[-1]), mod_idx),
                  _full(final_norm.shape)],
        out_specs=list(_two_part((tt, d), n_ctx_tiles)),
        out_shape=[jax.ShapeDtypeStruct((tc, d), F32), jax.ShapeDtypeStruct((t - tc, d), F32)],
        scratch_shapes=[pltpu.VMEM((d, tt), F32), pltpu.VMEM((eb, tt), BF16)],
        compiler_params=_params("arbitrary", "arbitrary"),
        name="peer",
    )(h2t, u, vt, r2, e2, b1, e1, x1, mods, final_norm)


def _hyena_features(l, bands):
    t = jnp.linspace(0.0, 1.0, l, dtype=F32)[:, None]
    wpos = 2.0 * math.pi * jnp.arange(l, dtype=F32)[:, None] / l
    f = jnp.linspace(1e-4, bands - 1, bands, dtype=F32)[None, :]
    return jnp.concatenate([t, jnp.cos(f * wpos), -jnp.sin(f * wpos)], axis=-1)


def kernel(x_prompt, x_sample, state_rwkv, c, c_ctx, w_ada, b_ada, norm_mix, norm_ffn, w_in, conv_w, w_out, rwkv_w0, rwkv_w1, rwkv_w2, rwkv_a0, rwkv_a1, rwkv_a2, rwkv_g1, rwkv_g2, rwkv_k_k, rwkv_k_a, rwkv_r_k, rwkv_ln_w, rwkv_ln_b, hy_f_w1, hy_f_b1, hy_f_w2, hy_f_b2, hy_freq, hy_f_w3, hy_bias, hy_norm, peer_wq, peer_keys, peer_u, peer_v, final_norm):
    bc, lc, d = x_prompt.shape
    bl, ll, _ = x_sample.shape
    depth = w_ada.shape[0]
    assert depth == 1
    dr = rwkv_w0.shape[-1]
    dh = hy_norm.shape[-1]
    nh = dr // HEAD
    assert dr == dh and w_in.shape[-1] == 3 * dr + 3 * dh
    tc, tl = bc * lc, bl * ll
    t = tc + tl
    tm = lc
    assert ll % tm == 0 and tm % GRID_W == 0 and tc % ll == 0
    n_ctx_tiles = tc // tm
    tiles_per_lat = ll // tm
    parts = LANES // (bl * (nh // 2))
    assert parts * bl * (nh // 2) == LANES and HEAD % parts == 0 and tc % tl == 0
    nil = HEAD // parts
    l = 0

    rows = jnp.concatenate([c_ctx[None, :], c, jnp.zeros((SUBLANES - 1 - bl, d), F32)], axis=0)
    mods = _ada(rows, w_ada[l], b_ada[l]).reshape(SUBLANES, 1, 6 * d)

    xp = x_prompt.reshape(tc, d)
    xs = x_sample.reshape(tl, d)

    def pad_cols(w):
        return jnp.pad(w, ((0, 0), (0, LANES - w.shape[1])))

    def pad_rows(w):
        return jnp.pad(w, ((0, LANES - w.shape[0]), (0, 0)))

    lora1 = jnp.concatenate([pad_cols(rwkv_w1[l, 0]), pad_cols(rwkv_w1[l, 1]), pad_cols(rwkv_a1[l, 0]),
                             pad_cols(rwkv_a1[l, 1]), pad_cols(rwkv_g1[l])], axis=1).astype(BF16)
    w2 = jnp.stack([pad_rows(rwkv_w2[l, 0]), pad_rows(rwkv_w2[l, 1])])
    a2 = jnp.stack([pad_rows(rwkv_a2[l, 0]), pad_rows(rwkv_a2[l, 1])])
    vecs_pre = jnp.stack([rwkv_w0[l, 0], rwkv_w0[l, 1], rwkv_a0[l, 0], rwkv_a0[l, 1], rwkv_k_k[l], rwkv_k_a[l],
                          rwkv_r_k[l, 0].reshape(dr), rwkv_r_k[l, 1].reshape(dr)])
    head_id = jnp.arange(dr, dtype=jnp.int32) // HEAD
    hs = (head_id[:, None] == head_id[None, :]).astype(BF16)
    shared, perdir, gb, hy = _mix_pre(xp, xs, mods, n_ctx_tiles, tiles_per_lat, tm, norm_mix[l][None, :],
                                      w_in[l].astype(BF16), conv_w[l], lora1, w2, a2, rwkv_g2[l], vecs_pre, hs)

    shared4 = shared.reshape(3, t // lc, lc, dr)
    perdir4 = perdir.reshape(2, 3, t // lc, lc, dr)
    y_cf, s_cf = _wkv_ctx(shared4, perdir4, bc, 16, False)
    y_cb, s_cb = _wkv_ctx(shared4, perdir4, bc, 16, True)
    shared_l = shared.reshape(3, t // ll, ll, dr)
    perdir_l = perdir.reshape(2, 3, t // ll, ll, dr)
    s0 = state_rwkv[:, l].reshape(bl, 2, nh // 2, 2, nil, parts, HEAD).transpose(1, 3, 4, 6, 2, 0, 5)
    s0 = s0.reshape(2, 2 * nil, HEAD, LANES)
    y_lf, y_lb = _wkv_lat(shared_l, perdir_l, s0, tc // tl, bl, 16)
    new_state = jnp.stack([s_cf, s_cb]).reshape(2, 2, HEAD, HEAD, nh // 2, bc).transpose(5, 0, 4, 1, 2, 3)
    new_state = new_state.reshape(bc, 1, 2, nh, HEAD, HEAD)

    bands = (hy_f_w1.shape[1] - 1) // 2
    femb = hy_f_w1.shape[1]
    fpad = 32
    w1p = jnp.pad(hy_f_w1[l], ((0, fpad - femb), (0, 0)))
    hidden = hy_f_w2.shape[-1]
    w3 = hy_f_w3[l].reshape(hidden, 4, dh).transpose(1, 0, 2)
    max_decay = math.log(1e-2) / 0.3
    min_decay = math.log(1e-2) / 1.5
    deltas = jnp.linspace(min_decay, max_decay, dh, dtype=F32)[None, :]
    z_parts = []
    for seq_len, first_seq, n_seq in ((lc, 0, bc), (ll, tc // ll, bl)):
        mats = _dft_mats(seq_len)
        zf = jnp.pad(_hyena_features(seq_len, bands), ((0, 0), (0, fpad - femb)))
        g = _hy_filter(zf, w1p, hy_f_b1[l][None, :], hy_f_w2[l], hy_f_b2[l][None, :], hy_freq[l], w3,
                       deltas, hy_bias[l], mats)
        z_parts.append(_hy_conv(hy, g, mats, seq_len, first_seq, n_seq))

    vecs_post = jnp.concatenate([jnp.stack([rwkv_ln_w[l], rwkv_ln_b[l], hy_norm[l]]),
                                 jnp.zeros((SUBLANES - 3, dr), F32)], axis=0)
    x1, h2 = _mix_post((y_cf.reshape(tc, dr), y_lf.reshape(tl, dr)), (y_cb.reshape(tc, dr), y_lb.reshape(tl, dr)),
                       gb, z_parts, (xp, xs), mods, n_ctx_tiles, tiles_per_lat, tm, vecs_post,
                       w_out[l].astype(BF16), norm_ffn[l][None, :], hs)

    r2, b1, e1, e2 = _route(h2, peer_wq[l].astype(BF16), peer_keys[l], tt=512)
    tt = 512
    assert ll % tt == 0 and tc % tt == 0
    y_p, y_s = _peer(h2.astype(BF16).T, peer_u[l], peer_v[l].astype(BF16).T, r2, e2, b1, e1, x1, mods,
                final_norm[None, :], tc // tt, ll // tt, tt, n1=SUBLANES)
    return y_p.reshape(bc, lc, d), y_s.reshape(bl, ll, d), new_state
```

```python
import functools
import math

import jax
import jax.numpy as jnp
import numpy as np
from jax import lax
from jax.experimental import pallas as pl
from jax.experimental.pallas import tpu as pltpu

F32 = jnp.float32
BF16 = jnp.bfloat16

GRID_W = 64
HEAD = 64
TOPK = 16
EPS = 1e-6
GN_EPS = 64e-5
LANES = 128
SUBLANES = 8
VMEM_LIMIT = 56 * 1024 * 1024


def _params(*sem):
    return pltpu.CompilerParams(dimension_semantics=sem, vmem_limit_bytes=VMEM_LIMIT)


def _split(x):
    hi = x.astype(BF16)
    lo = (x - hi.astype(F32)).astype(BF16)
    return hi, lo


def _dot(a, b):
    return jnp.dot(a, b, preferred_element_type=F32)


def _dot_split(ah, al, bh, bl):
    return _dot(ah, bh) + (_dot(al, bh) + _dot(ah, bl))


def _dot3(a, b):
    ah, al = _split(a)
    bh, bl = _split(b)
    return _dot_split(ah, al, bh, bl)


def _full(shape):
    n = len(shape)
    return pl.BlockSpec(shape, lambda *_: (0,) * n)


def _two_part(block, n_first):
    return (pl.BlockSpec(block, lambda i, *_: (jnp.minimum(i, n_first - 1), 0)),
            pl.BlockSpec(block, lambda i, *_: (jnp.maximum(i - n_first, 0), 0)))


def _const(shape):
    n = len(shape)
    return pl.BlockSpec(shape, lambda *_: (0,) * n, pipeline_mode=pl.Buffered(1))


def _ada_kernel(c_ref, w_ref, b_ref, o_ref):
    c = c_ref[...]
    s = c * jax.nn.sigmoid(c)
    o_ref[...] = _dot3(s, w_ref[...]) + b_ref[...]


def _ada(c_rows, w, b):
    rows, d = c_rows.shape
    n = w.shape[1]
    bn = n // 4
    return pl.pallas_call(
        _ada_kernel,
        grid=(4,),
        in_specs=[_full((rows, d)),
                  pl.BlockSpec((d, bn), lambda j: (0, j)),
                  pl.BlockSpec((1, bn), lambda j: (0, j))],
        out_specs=pl.BlockSpec((rows, bn), lambda j: (0, j)),
        out_shape=jax.ShapeDtypeStruct((rows, n), F32),
        compiler_params=_params("arbitrary"),
        name="ada",
    )(c_rows, w, b.reshape(1, n))


def _headsum(z, hs):
    zh, zl = _split(z)
    return _dot(zh, hs) + _dot(zl, hs)


def _mix_pre_kernel(n_ctx_tiles, xp_ref, xs_ref, mod_ref, nm_ref, win_ref, cw_ref, l1_ref, w2_ref, a2_ref,
                    g2_ref, vec_ref, hs_ref, sh_ref, dr_ref, gb_ref, hy_ref):
    tm, d = xp_ref.shape
    dr = hs_ref.shape[0]
    is_ctx = pl.program_id(0) < n_ctx_tiles
    x = jnp.where(is_ctx, xp_ref[...], xs_ref[...])
    mod = mod_ref[0]
    shift1 = mod[:, 0:d]
    scale1 = mod[:, d:2 * d]
    xn = x * lax.rsqrt(jnp.mean(x * x, axis=-1, keepdims=True) + EPS) * nm_ref[...]
    hb = (xn * (1.0 + scale1) + shift1).astype(BF16)

    pos = lax.broadcasted_iota(jnp.int32, (tm, dr), 0)
    row = pos % GRID_W
    keep_prev = jnp.where(is_ctx, jnp.where(pos == 0, 0.0, 1.0), jnp.where(row == 0, 0.0, 1.0))
    keep_next = jnp.where(is_ctx, jnp.where(pos == tm - 1, 0.0, 1.0),
                          jnp.where(row == GRID_W - 1, 0.0, 1.0))

    def proj(c):
        u = _dot(hb, win_ref[:, c * dr:(c + 1) * dr])
        cw = cw_ref[:, c * dr:(c + 1) * dr]
        up = pltpu.roll(u, 1, 0) * keep_prev
        un = pltpu.roll(u, tm - 1, 0) * keep_next
        return up * cw[0:1] + u * cw[1:2] + un * cw[2:3]

    r = proj(0)
    k = proj(1)
    v = proj(2)
    hy_ref[0] = proj(3)
    hy_ref[1] = proj(4)
    hy_ref[2] = proj(5)

    lo = _dot(hb, l1_ref[...])
    vec = vec_ref[...]
    k_k = vec[4:5]
    k_a = vec[5:6]
    hs = hs_ref[...]
    kkr = k * k_k
    kk = kkr * lax.rsqrt(jnp.maximum(_headsum(kkr * kkr, hs), 1e-24))
    sh_ref[0] = r
    sh_ref[1] = kk
    sh_ref[2] = v
    gb_ref[0] = _dot3(jax.nn.sigmoid(lo[:, 4 * LANES:5 * LANES]), g2_ref[...])
    bon = jnp.zeros_like(r)
    for dd in range(2):
        w = vec[dd:dd + 1] + _dot3(jnp.tanh(lo[:, dd * LANES:(dd + 1) * LANES]), w2_ref[dd])
        z = -w
        softplus = jnp.maximum(z, 0.0) + jnp.log1p(jnp.exp(-jnp.abs(z)))
        dr_ref[dd, 0] = jnp.exp(-jnp.exp(-softplus - 0.5))
        a = jax.nn.sigmoid(vec[2 + dd:3 + dd]
                           + _dot3(lo[:, (2 + dd) * LANES:(3 + dd) * LANES], a2_ref[dd]))
        kd = k * (1.0 + (a - 1.0) * k_a)
        dr_ref[dd, 1] = kd
        dr_ref[dd, 2] = kk * a
        bon = bon + r * kd * vec[6 + dd:7 + dd]
    gb_ref[1] = _headsum(bon, hs) * v


def _mix_pre(xp, xs, mods, n_ctx_tiles, tiles_per_lat, tm, norm_mix, w_in, conv_w, lora1, w2, a2, g2, vecs, hs):
    d = xp.shape[1]
    t = xp.shape[0] + xs.shape[0]
    dr = hs.shape[0]
    n_tiles = t // tm

    def mod_idx(i):
        return (jnp.where(i < n_ctx_tiles, 0, 1 + (i - n_ctx_tiles) // tiles_per_lat), 0, 0)

    return pl.pallas_call(
        functools.partial(_mix_pre_kernel, n_ctx_tiles),
        grid=(n_tiles,),
        in_specs=[*_two_part((tm, d), n_ctx_tiles),
                  pl.BlockSpec((1, 1, mods.shape[-1]), mod_idx),
                  _full(norm_mix.shape), _full(w_in.shape), _full(conv_w.shape), _full(lora1.shape),
                  _full(w2.shape), _full(a2.shape), _full(g2.shape), _full(vecs.shape), _full(hs.shape)],
        out_specs=[pl.BlockSpec((3, tm, dr), lambda i: (0, i, 0)),
                   pl.BlockSpec((2, 3, tm, dr), lambda i: (0, 0, i, 0)),
                   pl.BlockSpec((2, tm, dr), lambda i: (0, i, 0)),
                   pl.BlockSpec((3, tm, dr), lambda i: (0, i, 0))],
        out_shape=[jax.ShapeDtypeStruct((3, t, dr), F32),
                   jax.ShapeDtypeStruct((2, 3, t, dr), F32),
                   jax.ShapeDtypeStruct((2, t, dr), F32),
                   jax.ShapeDtypeStruct((3, t, dr), F32)],
        compiler_params=_params("arbitrary"),
        name="mix_pre",
    )(xp, xs, mods, norm_mix, w_in, conv_w, lora1, w2, a2, g2, vecs, hs)


def _wkv_ctx_kernel(reverse, sh_ref, pd_ref, y_ref, sfin_ref, s_scr, tr_scr, c_scr, yt_scr):
    nb, tb = sh_ref.shape[1], sh_ref.shape[2]
    nhp = sh_ref.shape[3] // LANES
    tblk = pl.program_id(0)

    @pl.when(tblk == 0)
    def _():
        s_scr[...] = jnp.zeros_like(s_scr)

    def to_lanes(ref, lead, t):
        blocks = [ref[lead + (slice(None), t, slice(hp * LANES, (hp + 1) * LANES))] for hp in range(nhp)]
        return jnp.concatenate(blocks, axis=0).T

    def relayout(t, slot):
        r_t = to_lanes(sh_ref, (0,), t)
        d_t = to_lanes(pd_ref, (0, 0), t)
        k_t = to_lanes(pd_ref, (0, 1), t)
        kka_t = to_lanes(pd_ref, (0, 2), t)
        tr_scr[slot, 0] = d_t * r_t
        tr_scr[slot, 1] = to_lanes(sh_ref, (1,), t)
        tr_scr[slot, 2] = to_lanes(sh_ref, (2,), t)
        tr_scr[slot, 3] = d_t
        tr_scr[slot, 4] = k_t
        tr_scr[slot, 5] = kka_t
        for h2 in range(2):
            rows = slice(h2 * HEAD, (h2 + 1) * HEAD)
            c_scr[slot, 2 * h2] = jnp.broadcast_to(jnp.sum(k_t[rows] * r_t[rows], axis=0, keepdims=True),
                                                   (SUBLANES, LANES))
            c_scr[slot, 2 * h2 + 1] = jnp.broadcast_to(jnp.sum(kka_t[rows] * r_t[rows], axis=0, keepdims=True),
                                                       (SUBLANES, LANES))

    def time_of(tau):
        return tb - 1 - tau if reverse else tau

    relayout(time_of(0), 0)
    for tau in range(tb):
        slot = tau % 2
        if tau + 1 < tb:
            relayout(time_of(tau + 1), 1 - slot)
        for h2 in range(2):
            rows = slice(h2 * HEAD, (h2 + 1) * HEAD)
            c1 = c_scr[slot, 2 * h2, 0:1, :]
            c2 = c_scr[slot, 2 * h2 + 1, 0:1, :]
            for i8 in range(HEAD // SUBLANES):
                base = h2 * HEAD + i8 * SUBLANES
                vblk = tr_scr[slot, 2, base:base + SUBLANES, :]
                outs = []
                for jj in range(SUBLANES):
                    s = s_scr[base + jj]
                    vi = vblk[jj:jj + 1]
                    sa = jnp.sum(s * tr_scr[slot, 1, rows, :], axis=0, keepdims=True)
                    q = jnp.sum(s * tr_scr[slot, 0, rows, :], axis=0, keepdims=True)
                    s_scr[base + jj] = (s * tr_scr[slot, 3, rows, :]
                                        + (vi * tr_scr[slot, 4, rows, :] - sa * tr_scr[slot, 5, rows, :]))
                    outs.append(q + vi * c1 - sa * c2)
                yt_scr[base:base + SUBLANES, :] = jnp.concatenate(outs, axis=0)
        y = yt_scr[...].T
        t = time_of(tau)
        for hp in range(nhp):
            y_ref[:, t, hp * LANES:(hp + 1) * LANES] = y[hp * nb:(hp + 1) * nb]

    @pl.when(tblk == pl.num_programs(0) - 1)
    def _():
        sfin_ref[...] = s_scr[...]


def _wkv_ctx(shared, perdir, nb, tb, reverse):
    _, nseq, lc, dr = shared.shape
    assert nb * (dr // LANES) == LANES and 2 * HEAD == LANES
    nt = lc // tb
    di = 1 if reverse else 0

    def tsel(ti):
        return nt - 1 - ti if reverse else ti

    return pl.pallas_call(
        functools.partial(_wkv_ctx_kernel, reverse),
        grid=(nt,),
        in_specs=[pl.BlockSpec((3, nb, tb, dr), lambda ti: (0, 0, tsel(ti), 0)),
                  pl.BlockSpec((1, 3, nb, tb, dr), lambda ti: (di, 0, 0, tsel(ti), 0))],
        out_specs=[pl.BlockSpec((nb, tb, dr), lambda ti: (0, tsel(ti), 0)),
                   pl.BlockSpec((LANES, HEAD, LANES), lambda ti: (0, 0, 0))],
        out_shape=[jax.ShapeDtypeStruct((nb, lc, dr), F32),
                   jax.ShapeDtypeStruct((LANES, HEAD, LANES), F32)],
        scratch_shapes=[pltpu.VMEM((LANES, HEAD, LANES), F32),
                        pltpu.VMEM((2, 6, LANES, LANES), F32),
                        pltpu.VMEM((2, 4, SUBLANES, LANES), F32),
                        pltpu.VMEM((LANES, LANES), F32)],
        compiler_params=_params("arbitrary"),
        name="wkv_ctx",
    )(shared, perdir)


def _wkv_lat_kernel(shf_ref, shb_ref, pdf_ref, pdb_ref, s0_ref, yf_ref, yb_ref, s_scr):
    nb, tb, dr = shf_ref.shape[1], shf_ref.shape[2], shf_ref.shape[3]
    nhp = dr // LANES
    parts = LANES // (nhp * nb)
    nil = HEAD // parts
    tblk = pl.program_id(0)
    lane_part = lax.broadcasted_iota(jnp.int32, (parts, LANES), 1) % parts
    diag = jnp.where(lane_part == lax.broadcasted_iota(jnp.int32, (parts, LANES), 0), 1.0, 0.0)

    @pl.when(tblk == 0)
    def _():
        s_scr[...] = s0_ref[...]

    def to_lanes(ref, lead, t):
        blocks = [jnp.broadcast_to(ref[lead + (b, slice(t, t + 1), slice(hp * LANES, (hp + 1) * LANES))],
                                   (parts, LANES))
                  for hp in range(nhp) for b in range(nb)]
        return jnp.concatenate(blocks, axis=0).T

    def fetch(sh_ref, pd_ref, t):
        return (to_lanes(sh_ref, (0,), t), to_lanes(sh_ref, (1,), t), to_lanes(sh_ref, (2,), t),
                to_lanes(pd_ref, (0, 0), t), to_lanes(pd_ref, (0, 1), t), to_lanes(pd_ref, (0, 2), t))

    def step(di, fetched, y_ref, t):
        r_t, kk_t, v_t, d_t, k_t, kka_t = fetched
        placed = []
        for h2 in range(2):
            rows = slice(h2 * HEAD, (h2 + 1) * HEAD)
            rt, kkt, dt, kt, kkat = r_t[rows], kk_t[rows], d_t[rows], k_t[rows], kka_t[rows]
            drt = dt * rt
            c1 = jnp.sum(kt * rt, axis=0, keepdims=True)
            c2 = jnp.sum(kkat * rt, axis=0, keepdims=True)
            for il in range(nil):
                r0 = h2 * HEAD + il * parts
                vi = jnp.sum(v_t[r0:r0 + parts] * diag, axis=0, keepdims=True)
                s = s_scr[di, h2 * nil + il]
                sa = jnp.sum(s * kkt, axis=0, keepdims=True)
                q = jnp.sum(s * drt, axis=0, keepdims=True)
                s_scr[di, h2 * nil + il] = s * dt + (vi * kt - sa * kkat)
                placed.append((q + vi * c1 - sa * c2) * diag)
        y = jnp.concatenate(placed, axis=0).T
        for hp in range(nhp):
            for b in range(nb):
                r0 = (hp * nb + b) * parts
                y_ref[b, t:t + 1, hp * LANES:(hp + 1) * LANES] = jnp.sum(y[r0:r0 + parts], axis=0, keepdims=True)

    ahead = (fetch(shf_ref, pdf_ref, 0), fetch(shb_ref, pdb_ref, tb - 1))
    for tau in range(tb):
        now = ahead
        if tau + 1 < tb:
            ahead = (fetch(shf_ref, pdf_ref, tau + 1), fetch(shb_ref, pdb_ref, tb - 2 - tau))
        step(0, now[0], yf_ref, tau)
        step(1, now[1], yb_ref, tb - 1 - tau)


def _wkv_lat(shared, perdir, s0, first_blk, nb, tb):
    _, nseq, ll, dr = shared.shape
    nt = ll // tb
    out = jax.ShapeDtypeStruct((nb, ll, dr), F32)
    return pl.pallas_call(
        _wkv_lat_kernel,
        grid=(nt,),
        in_specs=[pl.BlockSpec((3, nb, tb, dr), lambda ti: (0, first_blk, ti, 0)),
                  pl.BlockSpec((3, nb, tb, dr), lambda ti: (0, first_blk, nt - 1 - ti, 0)),
                  pl.BlockSpec((1, 3, nb, tb, dr), lambda ti: (0, 0, first_blk, ti, 0)),
                  pl.BlockSpec((1, 3, nb, tb, dr), lambda ti: (1, 0, first_blk, nt - 1 - ti, 0)),
                  _full(s0.shape)],
        out_specs=[pl.BlockSpec((nb, tb, dr), lambda ti: (0, ti, 0)),
                   pl.BlockSpec((nb, tb, dr), lambda ti: (0, nt - 1 - ti, 0))],
        out_shape=[out, out],
        scratch_shapes=[pltpu.VMEM(s0.shape, F32)],
        compiler_params=_params("arbitrary"),
        name="wkv_lat",
    )(shared, shared, perdir, perdir, s0)


def _dft_mats(l):
    n = 2 * l
    nfp = l + SUBLANES
    kf = np.arange(nfp, dtype=np.int64)
    ang = (2.0 * np.pi / n) * ((kf[:, None] * np.arange(l, dtype=np.int64)[None, :]) % n)
    valid = (kf <= l)[:, None]
    c = np.where(valid, np.cos(ang), 0.0)
    s = np.where(valid, np.sin(ang), 0.0)
    wk = np.where((kf == 0) | (kf == l), 1.0 / n, 2.0 / n)[None, :]
    mats = (c, s, c.T * wk, -(s.T) * wk)
    return [m for mat in mats for m in _split(jnp.asarray(mat.astype(np.float32)))]


def _hy_filter_kernel(zf_ref, w1_ref, b1_ref, w2_ref, b2_ref, fq_ref, w3_ref, dl_ref, bias_ref,
                      ch_ref, cl_ref, sh_ref, sl_ref, g_ref):
    zf = zf_ref[...]
    hid = jnp.sin(fq_ref[0:1] * (_dot3(zf, w1_ref[...]) + b1_ref[...]))
    hid = jnp.sin(fq_ref[1:2] * (_dot3(hid, w2_ref[...]) + b2_ref[...]))
    win = jnp.exp(-zf[:, 0:1] * jnp.abs(dl_ref[...]))
    hh, hl = _split(hid)
    filt = []
    for o in range(4):
        wh, wl = _split(w3_ref[o])
        filt.append(_dot_split(hh, hl, wh, wl) * win)
    for od in range(2):
        hf, hbk = filt[2 * od], filt[2 * od + 1]
        ph, plo = _split(hf + hbk)
        mh, ml = _split(hbk - hf)
        g_ref[od, 0] = _dot_split(ch_ref[...], cl_ref[...], ph, plo) + bias_ref[od:od + 1]
        g_ref[od, 1] = _dot_split(sh_ref[...], sl_ref[...], mh, ml)


def _hy_filter(zf, w1, b1, w2, b2, fq, w3, deltas, bias, mats):
    l = zf.shape[0]
    nfp = mats[0].shape[0]
    c = w3.shape[-1]
    cb = LANES
    return pl.pallas_call(
        _hy_filter_kernel,
        grid=(c // cb,),
        in_specs=[_full(zf.shape), _full(w1.shape), _full(b1.shape), _full(w2.shape), _full(b2.shape),
                  _full(fq.shape),
                  pl.BlockSpec((4, w3.shape[1], cb), lambda j: (0, 0, j)),
                  pl.BlockSpec((1, cb), lambda j: (0, j)),
                  pl.BlockSpec((2, cb), lambda j: (0, j)),
                  _const((nfp, l)), _const((nfp, l)), _const((nfp, l)), _const((nfp, l))],
        out_specs=pl.BlockSpec((2, 2, nfp, cb), lambda j: (0, 0, 0, j)),
        out_shape=jax.ShapeDtypeStruct((2, 2, nfp, c), F32),
        compiler_params=_params("arbitrary"),
        name="hy_filt",
    )(zf, w1, b1, w2, b2, fq, w3, deltas, bias, *mats[:4])


def _hy_conv_kernel(hy_ref, g_ref, ch_ref, cl_ref, sh_ref, sl_ref, cih_ref, cil_ref, sih_ref, sil_ref, o_ref):
    def conv(u, od):
        uh, ul = _split(u)
        re = _dot_split(ch_ref[...], cl_ref[...], uh, ul)
        im = -_dot_split(sh_ref[...], sl_ref[...], uh, ul)
        gre = g_ref[od, 0]
        gim = g_ref[od, 1]
        ah, al = _split(re * gre - im * gim)
        bh, bl = _split(re * gim + im * gre)
        return (_dot_split(cih_ref[...], cil_ref[...], ah, al)
                + _dot_split(sih_ref[...], sil_ref[...], bh, bl))

    z = hy_ref[1] * conv(hy_ref[0], 0)
    o_ref[...] = hy_ref[2] * conv(z, 1)


def _hy_conv(hy, g, mats, l, first_seq, n_seq):
    t, c = hy.shape[1], hy.shape[2]
    nfp = mats[0].shape[0]
    cb = 2 * LANES
    return pl.pallas_call(
        _hy_conv_kernel,
        grid=(n_seq, c // cb),
        in_specs=[pl.BlockSpec((3, l, cb), lambda b, j: (0, first_seq + b, j)),
                  pl.BlockSpec((2, 2, nfp, cb), lambda b, j: (0, 0, 0, j)),
                  _const((nfp, l)), _const((nfp, l)), _const((nfp, l)), _const((nfp, l)),
                  _const((l, nfp)), _const((l, nfp)), _const((l, nfp)), _const((l, nfp))],
        out_specs=pl.BlockSpec((l, cb), lambda b, j: (b, j)),
        out_shape=jax.ShapeDtypeStruct((n_seq * l, c), F32),
        compiler_params=_params("arbitrary", "arbitrary"),
        name="hy_conv",
    )(hy, g, *mats)


def _mix_post_kernel(n_ctx_tiles, ycf_ref, ylf_ref, ycb_ref, ylb_ref, gb_ref, zc_ref, zl_ref, xp_ref, xs_ref,
                     mod_ref, vec_ref, wout_ref, nf_ref, hs_ref, x1_ref, h2_ref):
    d = xp_ref.shape[1]
    dr = hs_ref.shape[0]
    is_ctx = pl.program_id(0) < n_ctx_tiles
    hs = hs_ref[...]
    vec = vec_ref[...]
    mod = mod_ref[0]
    gate1 = mod[:, 2 * d:3 * d]
    shift2 = mod[:, 3 * d:4 * d]
    scale2 = mod[:, 4 * d:5 * d]
    y = jnp.where(is_ctx, ycf_ref[...] + ycb_ref[...], ylf_ref[...] + ylb_ref[...])
    mu = _headsum(y, hs) * (1.0 / HEAD)
    yc = y - mu
    var = _headsum(yc * yc, hs) * (1.0 / HEAD)
    yn = yc * lax.rsqrt(var + GN_EPS) * vec[0:1] + vec[1:2]
    ya = (yn + gb_ref[1]) * gb_ref[0]
    z = jnp.where(is_ctx, zc_ref[...], zl_ref[...])
    yb = z * lax.rsqrt(jnp.mean(z * z, axis=-1, keepdims=True) + EPS) * vec[2:3]
    m = _dot(ya.astype(BF16), wout_ref[0:dr]) + _dot(yb.astype(BF16), wout_ref[dr:])
    x1 = jnp.where(is_ctx, xp_ref[...], xs_ref[...]) + gate1 * m
    x1_ref[...] = x1
    xn = x1 * lax.rsqrt(jnp.mean(x1 * x1, axis=-1, keepdims=True) + EPS) * nf_ref[...]
    h2_ref[...] = xn * (1.0 + scale2) + shift2


def _mix_post(y_fwd, y_bwd, gb, z, x, mods, n_ctx_tiles, tiles_per_lat, tm, vecs, w_out, norm_ffn, hs):
    d = x[0].shape[1]
    t = x[0].shape[0] + x[1].shape[0]
    dr = hs.shape[0]

    def mod_idx(i):
        return (jnp.where(i < n_ctx_tiles, 0, 1 + (i - n_ctx_tiles) // tiles_per_lat), 0, 0)

    tok = _two_part((tm, dr), n_ctx_tiles)
    tokd = pl.BlockSpec((tm, d), lambda i: (i, 0))
    return pl.pallas_call(
        functools.partial(_mix_post_kernel, n_ctx_tiles),
        grid=(t // tm,),
        in_specs=[*tok, *tok, pl.BlockSpec((2, tm, dr), lambda i: (0, i, 0)), *tok,
                  *_two_part((tm, d), n_ctx_tiles),
                  pl.BlockSpec((1, 1, mods.shape[-1]), mod_idx),
                  _full(vecs.shape), _full(w_out.shape), _full(norm_ffn.shape), _full(hs.shape)],
        out_specs=[tokd, tokd],
        out_shape=[jax.ShapeDtypeStruct((t, d), F32), jax.ShapeDtypeStruct((t, d), F32)],
        compiler_params=_params("arbitrary"),
        name="mix_post",
    )(*y_fwd, *y_bwd, gb, *z, *x, mods, vecs, w_out, norm_ffn, hs)


def _second_level_pairs():
    return [(a, b) for a in range(TOPK) for b in range(TOPK) if (a + 1) * (b + 1) <= TOPK]


def _route_kernel(h2_ref, wq_ref, keys_ref, r2_ref, b1_ref, e1_ref, e2_ref, sv_scr, s_scr, rk_scr, cnt_scr):
    nh = keys_ref.shape[1]
    nk = keys_ref.shape[2]
    half = keys_ref.shape[3]
    tt = h2_ref.shape[0]
    q = _dot(h2_ref[...].astype(BF16), wq_ref[...])
    key_id = lax.broadcasted_iota(jnp.int32, (nk, LANES), 0).astype(F32)
    neg = jnp.float32(-jnp.inf)

    for h in range(nh):
        for p in range(2):
            c0 = (h * 2 + p) * half
            qh, ql = _split(q[:, c0:c0 + half])
            kh, kl = _split(keys_ref[p, h])
            dims = (((1,), (1,)), ((), ()))
            s = (lax.dot_general(kh, qh, dims, preferred_element_type=F32)
                 + (lax.dot_general(kl, qh, dims, preferred_element_type=F32)
                    + lax.dot_general(kh, ql, dims, preferred_element_type=F32)))
            s_scr[p, h] = s

    nchunk = tt // LANES

    def first_level_quick():
        def chunk(c, ranked):
            col = pl.ds(pl.multiple_of(c * LANES, LANES), LANES)
            for h in range(nh):
                def extract(r, carry, h=h):
                    out = []
                    for p in range(2):
                        m_prev, above = carry[p]
                        s = s_scr[p, h, :, col]
                        below = s < m_prev
                        m = jnp.max(jnp.where(below, s, neg), axis=0, keepdims=True)
                        sv_scr[p, r, h:h + 1, col] = m
                        out.append((m, above + jnp.where(below, 1.0, 0.0)))
                    return tuple(out)

                start = (jnp.full((1, LANES), jnp.inf, F32), jnp.zeros((nk, LANES), F32))
                done = lax.fori_loop(0, TOPK, extract, (start, start))
                for p in range(2):
                    m_last, above = done[p]
                    rank = above - 1.0 + jnp.where(s_scr[p, h, :, col] < m_last, 1.0, 0.0)
                    rk_scr[p, h, :, col] = rank
                    ranked = jnp.maximum(ranked, jnp.sum(jnp.where(rank < float(TOPK), 1.0, 0.0),
                                                         axis=0, keepdims=True))
            return ranked

        return lax.fori_loop(0, nchunk, chunk, jnp.zeros((1, LANES), F32))

    def first_level_exact():
        def chunk(c, carry):
            col = pl.ds(pl.multiple_of(c * LANES, LANES), LANES)
            for h in range(nh):
                for p in range(2):
                    def extract(r, xr, p=p, h=h):
                        x, rank = xr
                        m = jnp.max(x, axis=0, keepdims=True)
                        first = jnp.min(jnp.where(x == m, key_id, float(nk)), axis=0, keepdims=True)
                        sel = key_id == first
                        sv_scr[p, r, h:h + 1, col] = m
                        return jnp.where(sel, neg, x), jnp.where(sel, lax.convert_element_type(r, F32), rank)

                    _, rank = lax.fori_loop(0, TOPK, extract,
                                            (s_scr[p, h, :, col], jnp.full((nk, LANES), float(TOPK), F32)))
                    rk_scr[p, h, :, col] = rank
            return carry

        lax.fori_loop(0, nchunk, chunk, 0)

    ranked = first_level_quick()

    @pl.when(jnp.max(ranked) > float(TOPK))
    def _():
        first_level_exact()

    pairs = _second_level_pairs()

    def second_level(c, carry):
        col = pl.ds(pl.multiple_of(c * LANES, LANES), LANES)
        v1 = [sv_scr[0, a, :, col] for a in range(TOPK)]
        v2 = [sv_scr[1, b, :, col] for b in range(TOPK)]
        top = v1[0] + v2[0]

        def select(_, carry):
            cand, cnt, zsum = list(carry[0]), list(carry[1]), carry[2]
            m = cand[0]
            for cv in cand[1:]:
                m = jnp.maximum(m, cv)
            zsum = zsum + jnp.exp(m - top)
            found = jnp.zeros((nh, LANES), F32)
            for ci, (a, _b) in enumerate(pairs):
                eq = jnp.where(cand[ci] == m, 1.0, 0.0)
                hit = eq * (1.0 - found)
                found = jnp.maximum(found, eq)
                cand[ci] = jnp.where(hit > 0.0, neg, cand[ci])
                cnt[a] = cnt[a] + hit
            return tuple(cand), tuple(cnt), zsum

        zero = jnp.zeros((nh, LANES), F32)
        _, cnt, zsum = lax.fori_loop(0, TOPK, select,
                                     (tuple(v1[a] + v2[b] for a, b in pairs), (zero,) * TOPK, zero))
        for a in range(TOPK):
            cnt_scr[a, :, col] = cnt[a]
        cnt_scr[TOPK, :, col] = 1.0 / zsum
        return carry

    lax.fori_loop(0, nchunk, second_level, 0)

    for h in range(nh):
        rank1 = rk_scr[0, h]
        b1 = jnp.zeros((nk, tt), F32)
        for a in range(TOPK):
            b1 = b1 + jnp.where(rank1 == float(a), cnt_scr[a, h:h + 1, :], 0.0)
        b1_ref[h] = b1
        r2_ref[h] = rk_scr[1, h].astype(BF16)
        e1_ref[h] = jnp.exp(s_scr[0, h] - sv_scr[0, 0, h:h + 1, :]) * cnt_scr[TOPK, h:h + 1, :]
        e2_ref[h] = jnp.exp(s_scr[1, h] - sv_scr[1, 0, h:h + 1, :]).astype(BF16)


def _route(h2, wq, keys, tt):
    t, d = h2.shape
    _, nh, nk, _ = keys.shape
    out = jax.ShapeDtypeStruct((nh, nk, t), F32)
    out16 = jax.ShapeDtypeStruct((nh, nk, t), BF16)
    ospec = pl.BlockSpec((nh, nk, tt), lambda i: (0, 0, i))
    return pl.pallas_call(
        _route_kernel,
        grid=(t // tt,),
        in_specs=[pl.BlockSpec((tt, d), lambda i: (i, 0)), _full(wq.shape), _full(keys.shape)],
        out_specs=[ospec, ospec, ospec, ospec],
        out_shape=[out16, out, out, out16],
        scratch_shapes=[pltpu.VMEM((2, TOPK, nh, tt), F32),
                        pltpu.VMEM((2, nh, nk, tt), F32),
                        pltpu.VMEM((2, nh, nk, tt), F32),
                        pltpu.VMEM((TOPK + 1, nh, tt), F32)],
        compiler_params=_params("arbitrary"),
        name="route",
    )(h2, wq, keys)


def _peer_kernel(n_ctx_tiles, h2t_ref, u_ref, vt_ref, r2_ref, e2_ref, b1_ref, e1_ref, x1_ref, mod_ref, fn_ref,
                 op_ref, os_ref, acc_scr, a_scr):
    d = x1_ref.shape[1]
    nh, nk, tt = r2_ref.shape
    n1 = b1_ref.shape[1]
    eb = pl.program_id(1)

    @pl.when(eb == 0)
    def _():
        acc_scr[...] = jnp.zeros_like(acc_scr)

    hid = _dot(u_ref[...].astype(BF16), h2t_ref[...])
    pk = 2 * SUBLANES
    for i in range(n1):
        w = jnp.zeros((nk // pk, pk, tt), BF16)
        for h in range(nh):
            b1r = jnp.broadcast_to(b1_ref[h, i:i + 1, :], (pk, tt)).astype(BF16)[None]
            e1r = jnp.broadcast_to(e1_ref[h, i:i + 1, :], (pk, tt)).astype(BF16)[None]
            r2 = r2_ref[h].reshape(nk // pk, pk, tt)
            e2 = e2_ref[h].reshape(nk // pk, pk, tt)
            w = w + jnp.where(r2 < b1r, e1r * e2, jnp.zeros((), BF16))
        hi = hid[i * nk:(i + 1) * nk]
        act = 0.5 * hi * (1.0 + lax.erf(hi * (1.0 / math.sqrt(2.0))))
        a_scr[i * nk:(i + 1) * nk, :] = (act.astype(BF16).reshape(nk // pk, pk, tt) * w).reshape(nk, tt)
    acc_scr[...] += _dot(vt_ref[...], a_scr[...])

    @pl.when(eb == pl.num_programs(1) - 1)
    def _():
        gate2 = mod_ref[0][:, 5 * d:6 * d]
        xo = x1_ref[...] + gate2 * acc_scr[...].T
        res = xo * lax.rsqrt(jnp.mean(xo * xo, axis=-1, keepdims=True) + EPS) * fn_ref[...]
        is_ctx = pl.program_id(0) < n_ctx_tiles

        @pl.when(is_ctx)
        def _():
            op_ref[...] = res

        @pl.when(jnp.logical_not(is_ctx))
        def _():
            os_ref[...] = res


def _peer(h2t, u, vt, r2, e2, b1, e1, x1, mods, final_norm, n_ctx_tiles, tiles_per_lat, tt, n1):
    d, t = h2t.shape
    ne = u.shape[0]
    nh, nk, _ = r2.shape
    eb = n1 * nk

    def mod_idx(i, e):
        return (jnp.where(i < n_ctx_tiles, 0, 1 + (i - n_ctx_tiles) // tiles_per_lat), 0, 0)

    dense = pl.BlockSpec((nh, nk, tt), lambda i, e: (0, 0, i))
    rows = pl.BlockSpec((nh, n1, tt), lambda i, e: (0, e, i))
    tc = n_ctx_tiles * tt
    return pl.pallas_call(
        functools.partial(_peer_kernel, n_ctx_tiles),
        grid=(t // tt, ne // eb),
        in_specs=[pl.BlockSpec((d, tt), lambda i, e: (0, i)),
                  pl.BlockSpec((eb, d), lambda i, e: (e, 0)),
                  pl.BlockSpec((d, eb), lambda i, e: (0, e)),
                  dense, dense, rows, rows,
                  pl.BlockSpec((tt, d), lambda i, e: (i, 0)),
                  pl.BlockSpec((1, 1, mods.shape[-1]), mod_idx),
                  _full(final_norm.shape)],
        out_specs=list(_two_part((tt, d), n_ctx_tiles)),
        out_shape=[jax.ShapeDtypeStruct((tc, d), F32), jax.ShapeDtypeStruct((t - tc, d), F32)],
        scratch_shapes=[pltpu.VMEM((d, tt), F32), pltpu.VMEM((eb, tt), BF16)],
        compiler_params=_params("arbitrary", "arbitrary"),
        name="peer",
    )(h2t, u, vt, r2, e2, b1, e1, x1, mods, final_norm)


def _hyena_features(l, bands):
    t = jnp.linspace(0.0, 1.0, l, dtype=F32)[:, None]
    wpos = 2.0 * math.pi * jnp.arange(l, dtype=F32)[:, None] / l
    f = jnp.linspace(1e-4, bands - 1, bands, dtype=F32)[None, :]
    return jnp.concatenate([t, jnp.cos(f * wpos), -jnp.sin(f * wpos)], axis=-1)


def kernel(x_prompt, x_sample, state_rwkv, c, c_ctx, w_ada, b_ada, norm_mix, norm_ffn, w_in, conv_w, w_out, rwkv_w0, rwkv_w1, rwkv_w2, rwkv_a0, rwkv_a1, rwkv_a2, rwkv_g1, rwkv_g2, rwkv_k_k, rwkv_k_a, rwkv_r_k, rwkv_ln_w, rwkv_ln_b, hy_f_w1, hy_f_b1, hy_f_w2, hy_f_b2, hy_freq, hy_f_w3, hy_bias, hy_norm, peer_wq, peer_keys, peer_u, peer_v, final_norm):
    bc, lc, d = x_prompt.shape
    bl, ll, _ = x_sample.shape
    depth = w_ada.shape[0]
    assert depth == 1
    dr = rwkv_w0.shape[-1]
    dh = hy_norm.shape[-1]
    nh = dr // HEAD
    assert dr == dh and w_in.shape[-1] == 3 * dr + 3 * dh
    tc, tl = bc * lc, bl * ll
    t = tc + tl
    tm = lc
    assert ll % tm == 0 and tm % GRID_W == 0 and tc % ll == 0
    n_ctx_tiles = tc // tm
    tiles_per_lat = ll // tm
    parts = LANES // (bl * (nh // 2))
    assert parts * bl * (nh // 2) == LANES and HEAD % parts == 0 and tc % tl == 0
    nil = HEAD // parts
    l = 0

    rows = jnp.concatenate([c_ctx[None, :], c, jnp.zeros((SUBLANES - 1 - bl, d), F32)], axis=0)
    mods = _ada(rows, w_ada[l], b_ada[l]).reshape(SUBLANES, 1, 6 * d)

    xp = x_prompt.reshape(tc, d)
    xs = x_sample.reshape(tl, d)

    def pad_cols(w):
        return jnp.pad(w, ((0, 0), (0, LANES - w.shape[1])))

    def pad_rows(w):
        return jnp.pad(w, ((0, LANES - w.shape[0]), (0, 0)))

    lora1 = jnp.concatenate([pad_cols(rwkv_w1[l, 0]), pad_cols(rwkv_w1[l, 1]), pad_cols(rwkv_a1[l, 0]),
                             pad_cols(rwkv_a1[l, 1]), pad_cols(rwkv_g1[l])], axis=1).astype(BF16)
    w2 = jnp.stack([pad_rows(rwkv_w2[l, 0]), pad_rows(rwkv_w2[l, 1])])
    a2 = jnp.stack([pad_rows(rwkv_a2[l, 0]), pad_rows(rwkv_a2[l, 1])])
    vecs_pre = jnp.stack([rwkv_w0[l, 0], rwkv_w0[l, 1], rwkv_a0[l, 0], rwkv_a0[l, 1], rwkv_k_k[l], rwkv_k_a[l],
                          rwkv_r_k[l, 0].reshape(dr), rwkv_r_k[l, 1].reshape(dr)])
    head_id = jnp.arange(dr, dtype=jnp.int32) // HEAD
    hs = (head_id[:, None] == head_id[None, :]).astype(BF16)
    shared, perdir, gb, hy = _mix_pre(xp, xs, mods, n_ctx_tiles, tiles_per_lat, tm, norm_mix[l][None, :],
                                      w_in[l].astype(BF16), conv_w[l], lora1, w2, a2, rwkv_g2[l], vecs_pre, hs)

    shared4 = shared.reshape(3, t // lc, lc, dr)
    perdir4 = perdir.reshape(2, 3, t // lc, lc, dr)
    y_cf, s_cf = _wkv_ctx(shared4, perdir4, bc, 8, False)
    y_cb, s_cb = _wkv_ctx(shared4, perdir4, bc, 8, True)
    shared_l = shared.reshape(3, t // ll, ll, dr)
    perdir_l = perdir.reshape(2, 3, t // ll, ll, dr)
    s0 = state_rwkv[:, l].reshape(bl, 2, nh // 2, 2, nil, parts, HEAD).transpose(1, 3, 4, 6, 2, 0, 5)
    s0 = s0.reshape(2, 2 * nil, HEAD, LANES)
    y_lf, y_lb = _wkv_lat(shared_l, perdir_l, s0, tc // tl, bl, 16)
    new_state = jnp.stack([s_cf, s_cb]).reshape(2, 2, HEAD, HEAD, nh // 2, bc).transpose(5, 0, 4, 1, 2, 3)
    new_state = new_state.reshape(bc, 1, 2, nh, HEAD, HEAD)

    bands = (hy_f_w1.shape[1] - 1) // 2
    femb = hy_f_w1.shape[1]
    fpad = 32
    w1p = jnp.pad(hy_f_w1[l], ((0, fpad - femb), (0, 0)))
    hidden = hy_f_w2.shape[-1]
    w3 = hy_f_w3[l].reshape(hidden, 4, dh).transpose(1, 0, 2)
    max_decay = math.log(1e-2) / 0.3
    min_decay = math.log(1e-2) / 1.5
    deltas = jnp.linspace(min_decay, max_decay, dh, dtype=F32)[None, :]
    z_parts = []
    for seq_len, first_seq, n_seq in ((lc, 0, bc), (ll, tc // ll, bl)):
        mats = _dft_mats(seq_len)
        zf = jnp.pad(_hyena_features(seq_len, bands), ((0, 0), (0, fpad - femb)))
        g = _hy_filter(zf, w1p, hy_f_b1[l][None, :], hy_f_w2[l], hy_f_b2[l][None, :], hy_freq[l], w3,
                       deltas, hy_bias[l], mats)
        z_parts.append(_hy_conv(hy, g, mats, seq_len, first_seq, n_seq))

    vecs_post = jnp.concatenate([jnp.stack([rwkv_ln_w[l], rwkv_ln_b[l], hy_norm[l]]),
                                 jnp.zeros((SUBLANES - 3, dr), F32)], axis=0)
    x1, h2 = _mix_post((y_cf.reshape(tc, dr), y_lf.reshape(tl, dr)), (y_cb.reshape(tc, dr), y_lb.reshape(tl, dr)),
                       gb, z_parts, (xp, xs), mods, n_ctx_tiles, tiles_per_lat, tm, vecs_post,
                       w_out[l].astype(BF16), norm_ffn[l][None, :], hs)

    r2, b1, e1, e2 = _route(h2, peer_wq[l].astype(BF16), peer_keys[l], tt=512)
    tt = 512
    assert ll % tt == 0 and tc % tt == 0
    y_p, y_s = _peer(h2.astype(BF16).T, peer_u[l], peer_v[l].astype(BF16).T, r2, e2, b1, e1, x1, mods,
                final_norm[None, :], tc // tt, ll // tt, tt, n1=SUBLANES)
    return y_p.reshape(bc, lc, d), y_s.reshape(bl, ll, d), new_state
```

```python
import functools
import math

import jax
import jax.numpy as jnp
import numpy as np
from jax import lax
from jax.experimental import pallas as pl
from jax.experimental.pallas import tpu as pltpu

F32 = jnp.float32
BF16 = jnp.bfloat16

GRID_W = 64
HEAD = 64
TOPK = 16
EPS = 1e-6
GN_EPS = 64e-5
LANES = 128
SUBLANES = 8
VMEM_LIMIT = 56 * 1024 * 1024


def _params(*sem):
    return pltpu.CompilerParams(dimension_semantics=sem, vmem_limit_bytes=VMEM_LIMIT)


def _split(x):
    hi = x.astype(BF16)
    lo = (x - hi.astype(F32)).astype(BF16)
    return hi, lo


def _dot(a, b):
    return jnp.dot(a, b, preferred_element_type=F32)


def _dot_split(ah, al, bh, bl):
    return _dot(ah, bh) + (_dot(al, bh) + _dot(ah, bl))


def _dot3(a, b):
    ah, al = _split(a)
    bh, bl = _split(b)
    return _dot_split(ah, al, bh, bl)


def _full(shape):
    n = len(shape)
    return pl.BlockSpec(shape, lambda *_: (0,) * n)


def _two_part(block, n_first):
    return (pl.BlockSpec(block, lambda i, *_: (jnp.minimum(i, n_first - 1), 0)),
            pl.BlockSpec(block, lambda i, *_: (jnp.maximum(i - n_first, 0), 0)))


def _const(shape):
    n = len(shape)
    return pl.BlockSpec(shape, lambda *_: (0,) * n, pipeline_mode=pl.Buffered(1))


def _ada_kernel(c_ref, w_ref, b_ref, o_ref):
    c = c_ref[...]
    s = c * jax.nn.sigmoid(c)
    o_ref[...] = _dot3(s, w_ref[...]) + b_ref[...]


def _ada(c_rows, w, b):
    rows, d = c_rows.shape
    n = w.shape[1]
    bn = n // 4
    return pl.pallas_call(
        _ada_kernel,
        grid=(4,),
        in_specs=[_full((rows, d)),
                  pl.BlockSpec((d, bn), lambda j: (0, j)),
                  pl.BlockSpec((1, bn), lambda j: (0, j))],
        out_specs=pl.BlockSpec((rows, bn), lambda j: (0, j)),
        out_shape=jax.ShapeDtypeStruct((rows, n), F32),
        compiler_params=_params("arbitrary"),
        name="ada",
    )(c_rows, w, b.reshape(1, n))


def _headsum(z, hs):
    zh, zl = _split(z)
    return _dot(zh, hs) + _dot(zl, hs)


def _mix_pre_kernel(n_ctx_tiles, xp_ref, xs_ref, mod_ref, nm_ref, win_ref, cw_ref, l1_ref, w2_ref, a2_ref,
                    g2_ref, vec_ref, hs_ref, sh_ref, dr_ref, gb_ref, hy_ref):
    tm, d = xp_ref.shape
    dr = hs_ref.shape[0]
    is_ctx = pl.program_id(0) < n_ctx_tiles
    x = jnp.where(is_ctx, xp_ref[...], xs_ref[...])
    mod = mod_ref[0]
    shift1 = mod[:, 0:d]
    scale1 = mod[:, d:2 * d]
    xn = x * lax.rsqrt(jnp.mean(x * x, axis=-1, keepdims=True) + EPS) * nm_ref[...]
    hb = (xn * (1.0 + scale1) + shift1).astype(BF16)

    pos = lax.broadcasted_iota(jnp.int32, (tm, dr), 0)
    row = pos % GRID_W
    keep_prev = jnp.where(is_ctx, jnp.where(pos == 0, 0.0, 1.0), jnp.where(row == 0, 0.0, 1.0))
    keep_next = jnp.where(is_ctx, jnp.where(pos == tm - 1, 0.0, 1.0),
                          jnp.where(row == GRID_W - 1, 0.0, 1.0))

    def proj(c):
        u = _dot(hb, win_ref[:, c * dr:(c + 1) * dr])
        cw = cw_ref[:, c * dr:(c + 1) * dr]
        up = pltpu.roll(u, 1, 0) * keep_prev
        un = pltpu.roll(u, tm - 1, 0) * keep_next
        return up * cw[0:1] + u * cw[1:2] + un * cw[2:3]

    r = proj(0)
    k = proj(1)
    v = proj(2)
    hy_ref[0] = proj(3)
    hy_ref[1] = proj(4)
    hy_ref[2] = proj(5)

    lo = _dot(hb, l1_ref[...])
    vec = vec_ref[...]
    k_k = vec[4:5]
    k_a = vec[5:6]
    hs = hs_ref[...]
    kkr = k * k_k
    kk = kkr * lax.rsqrt(jnp.maximum(_headsum(kkr * kkr, hs), 1e-24))
    sh_ref[0] = r
    sh_ref[1] = kk
    sh_ref[2] = v
    gb_ref[0] = _dot3(jax.nn.sigmoid(lo[:, 4 * LANES:5 * LANES]), g2_ref[...])
    bon = jnp.zeros_like(r)
    for dd in range(2):
        w = vec[dd:dd + 1] + _dot3(jnp.tanh(lo[:, dd * LANES:(dd + 1) * LANES]), w2_ref[dd])
        z = -w
        softplus = jnp.maximum(z, 0.0) + jnp.log1p(jnp.exp(-jnp.abs(z)))
        dr_ref[dd, 0] = jnp.exp(-jnp.exp(-softplus - 0.5))
        a = jax.nn.sigmoid(vec[2 + dd:3 + dd]
                           + _dot3(lo[:, (2 + dd) * LANES:(3 + dd) * LANES], a2_ref[dd]))
        kd = k * (1.0 + (a - 1.0) * k_a)
        dr_ref[dd, 1] = kd
        dr_ref[dd, 2] = kk * a
        bon = bon + r * kd * vec[6 + dd:7 + dd]
    gb_ref[1] = _headsum(bon, hs) * v


def _mix_pre(xp, xs, mods, n_ctx_tiles, tiles_per_lat, tm, norm_mix, w_in, conv_w, lora1, w2, a2, g2, vecs, hs):
    d = xp.shape[1]
    t = xp.shape[0] + xs.shape[0]
    dr = hs.shape[0]
    n_tiles = t // tm

    def mod_idx(i):
        return (jnp.where(i < n_ctx_tiles, 0, 1 + (i - n_ctx_tiles) // tiles_per_lat), 0, 0)

    return pl.pallas_call(
        functools.partial(_mix_pre_kernel, n_ctx_tiles),
        grid=(n_tiles,),
        in_specs=[*_two_part((tm, d), n_ctx_tiles),
                  pl.BlockSpec((1, 1, mods.shape[-1]), mod_idx),
                  _full(norm_mix.shape), _full(w_in.shape), _full(conv_w.shape), _full(lora1.shape),
                  _full(w2.shape), _full(a2.shape), _full(g2.shape), _full(vecs.shape), _full(hs.shape)],
        out_specs=[pl.BlockSpec((3, tm, dr), lambda i: (0, i, 0)),
                   pl.BlockSpec((2, 3, tm, dr), lambda i: (0, 0, i, 0)),
                   pl.BlockSpec((2, tm, dr), lambda i: (0, i, 0)),
                   pl.BlockSpec((3, tm, dr), lambda i: (0, i, 0))],
        out_shape=[jax.ShapeDtypeStruct((3, t, dr), F32),
                   jax.ShapeDtypeStruct((2, 3, t, dr), F32),
                   jax.ShapeDtypeStruct((2, t, dr), F32),
                   jax.ShapeDtypeStruct((3, t, dr), F32)],
        compiler_params=_params("arbitrary"),
        name="mix_pre",
    )(xp, xs, mods, norm_mix, w_in, conv_w, lora1, w2, a2, g2, vecs, hs)


def _wkv_ctx_kernel(reverse, sh_ref, pd_ref, y_ref, sfin_ref, s_scr, tr_scr, c_scr, yt_scr):
    nb, tb = sh_ref.shape[1], sh_ref.shape[2]
    nhp = sh_ref.shape[3] // LANES
    tblk = pl.program_id(0)

    @pl.when(tblk == 0)
    def _():
        s_scr[...] = jnp.zeros_like(s_scr)

    def to_lanes(ref, lead, t):
        blocks = [ref[lead + (slice(None), t, slice(hp * LANES, (hp + 1) * LANES))] for hp in range(nhp)]
        return jnp.concatenate(blocks, axis=0).T

    def relayout(t, slot):
        r_t = to_lanes(sh_ref, (0,), t)
        d_t = to_lanes(pd_ref, (0, 0), t)
        k_t = to_lanes(pd_ref, (0, 1), t)
        kka_t = to_lanes(pd_ref, (0, 2), t)
        tr_scr[slot, 0] = d_t * r_t
        tr_scr[slot, 1] = to_lanes(sh_ref, (1,), t)
        tr_scr[slot, 2] = to_lanes(sh_ref, (2,), t)
        tr_scr[slot, 3] = d_t
        tr_scr[slot, 4] = k_t
        tr_scr[slot, 5] = kka_t
        for h2 in range(2):
            rows = slice(h2 * HEAD, (h2 + 1) * HEAD)
            c_scr[slot, 2 * h2] = jnp.broadcast_to(jnp.sum(k_t[rows] * r_t[rows], axis=0, keepdims=True),
                                                   (SUBLANES, LANES))
            c_scr[slot, 2 * h2 + 1] = jnp.broadcast_to(jnp.sum(kka_t[rows] * r_t[rows], axis=0, keepdims=True),
                                                       (SUBLANES, LANES))

    def time_of(tau):
        return tb - 1 - tau if reverse else tau

    relayout(time_of(0), 0)
    for tau in range(tb):
        slot = tau % 2
        if tau + 1 < tb:
            relayout(time_of(tau + 1), 1 - slot)
        for h2 in range(2):
            rows = slice(h2 * HEAD, (h2 + 1) * HEAD)
            c1 = c_scr[slot, 2 * h2, 0:1, :]
            c2 = c_scr[slot, 2 * h2 + 1, 0:1, :]
            for i8 in range(HEAD // SUBLANES):
                base = h2 * HEAD + i8 * SUBLANES
                vblk = tr_scr[slot, 2, base:base + SUBLANES, :]
                outs = []
                for jj in range(SUBLANES):
                    s = s_scr[base + jj]
                    vi = vblk[jj:jj + 1]
                    sa = jnp.sum(s * tr_scr[slot, 1, rows, :], axis=0, keepdims=True)
                    q = jnp.sum(s * tr_scr[slot, 0, rows, :], axis=0, keepdims=True)
                    s_scr[base + jj] = (s * tr_scr[slot, 3, rows, :]
                                        + (vi * tr_scr[slot, 4, rows, :] - sa * tr_scr[slot, 5, rows, :]))
                    outs.append(q + vi * c1 - sa * c2)
                yt_scr[base:base + SUBLANES, :] = jnp.concatenate(outs, axis=0)
        y = yt_scr[...].T
        t = time_of(tau)
        for hp in range(nhp):
            y_ref[:, t, hp * LANES:(hp + 1) * LANES] = y[hp * nb:(hp + 1) * nb]

    @pl.when(tblk == pl.num_programs(0) - 1)
    def _():
        sfin_ref[...] = s_scr[...]


def _wkv_ctx(shared, perdir, nb, tb, reverse):
    _, nseq, lc, dr = shared.shape
    assert nb * (dr // LANES) == LANES and 2 * HEAD == LANES
    nt = lc // tb
    di = 1 if reverse else 0

    def tsel(ti):
        return nt - 1 - ti if reverse else ti

    return pl.pallas_call(
        functools.partial(_wkv_ctx_kernel, reverse),
        grid=(nt,),
        in_specs=[pl.BlockSpec((3, nb, tb, dr), lambda ti: (0, 0, tsel(ti), 0)),
                  pl.BlockSpec((1, 3, nb, tb, dr), lambda ti: (di, 0, 0, tsel(ti), 0))],
        out_specs=[pl.BlockSpec((nb, tb, dr), lambda ti: (0, tsel(ti), 0)),
                   pl.BlockSpec((LANES, HEAD, LANES), lambda ti: (0, 0, 0))],
        out_shape=[jax.ShapeDtypeStruct((nb, lc, dr), F32),
                   jax.ShapeDtypeStruct((LANES, HEAD, LANES), F32)],
        scratch_shapes=[pltpu.VMEM((LANES, HEAD, LANES), F32),
                        pltpu.VMEM((2, 6, LANES, LANES), F32),
                        pltpu.VMEM((2, 4, SUBLANES, LANES), F32),
                        pltpu.VMEM((LANES, LANES), F32)],
        compiler_params=_params("arbitrary"),
        name="wkv_ctx",
    )(shared, perdir)


def _wkv_lat_kernel(shf_ref, shb_ref, pdf_ref, pdb_ref, s0_ref, yf_ref, yb_ref, s_scr):
    nb, tb, dr = shf_ref.shape[1], shf_ref.shape[2], shf_ref.shape[3]
    nhp = dr // LANES
    parts = LANES // (nhp * nb)
    nil = HEAD // parts
    tblk = pl.program_id(0)
    lane_part = lax.broadcasted_iota(jnp.int32, (parts, LANES), 1) % parts
    diag = jnp.where(lane_part == lax.broadcasted_iota(jnp.int32, (parts, LANES), 0), 1.0, 0.0)

    @pl.when(tblk == 0)
    def _():
        s_scr[...] = s0_ref[...]

    def to_lanes(ref, lead, t):
        blocks = [jnp.broadcast_to(ref[lead + (b, slice(t, t + 1), slice(hp * LANES, (hp + 1) * LANES))],
                                   (parts, LANES))
                  for hp in range(nhp) for b in range(nb)]
        return jnp.concatenate(blocks, axis=0).T

    def fetch(sh_ref, pd_ref, t):
        return (to_lanes(sh_ref, (0,), t), to_lanes(sh_ref, (1,), t), to_lanes(sh_ref, (2,), t),
                to_lanes(pd_ref, (0, 0), t), to_lanes(pd_ref, (0, 1), t), to_lanes(pd_ref, (0, 2), t))

    def step(di, fetched, y_ref, t):
        r_t, kk_t, v_t, d_t, k_t, kka_t = fetched
        placed = []
        for h2 in range(2):
            rows = slice(h2 * HEAD, (h2 + 1) * HEAD)
            rt, kkt, dt, kt, kkat = r_t[rows], kk_t[rows], d_t[rows], k_t[rows], kka_t[rows]
            drt = dt * rt
            c1 = jnp.sum(kt * rt, axis=0, keepdims=True)
            c2 = jnp.sum(kkat * rt, axis=0, keepdims=True)
            for il in range(nil):
                r0 = h2 * HEAD + il * parts
                vi = jnp.sum(v_t[r0:r0 + parts] * diag, axis=0, keepdims=True)
                s = s_scr[di, h2 * nil + il]
                sa = jnp.sum(s * kkt, axis=0, keepdims=True)
                q = jnp.sum(s * drt, axis=0, keepdims=True)
                s_scr[di, h2 * nil + il] = s * dt + (vi * kt - sa * kkat)
                placed.append((q + vi * c1 - sa * c2) * diag)
        y = jnp.concatenate(placed, axis=0).T
        for hp in range(nhp):
            for b in range(nb):
                r0 = (hp * nb + b) * parts
                y_ref[b, t:t + 1, hp * LANES:(hp + 1) * LANES] = jnp.sum(y[r0:r0 + parts], axis=0, keepdims=True)

    ahead = (fetch(shf_ref, pdf_ref, 0), fetch(shb_ref, pdb_ref, tb - 1))
    for tau in range(tb):
        now = ahead
        if tau + 1 < tb:
            ahead = (fetch(shf_ref, pdf_ref, tau + 1), fetch(shb_ref, pdb_ref, tb - 2 - tau))
        step(0, now[0], yf_ref, tau)
        step(1, now[1], yb_ref, tb - 1 - tau)


def _wkv_lat(shared, perdir, s0, first_blk, nb, tb):
    _, nseq, ll, dr = shared.shape
    nt = ll // tb
    out = jax.ShapeDtypeStruct((nb, ll, dr), F32)
    return pl.pallas_call(
        _wkv_lat_kernel,
        grid=(nt,),
        in_specs=[pl.BlockSpec((3, nb, tb, dr), lambda ti: (0, first_blk, ti, 0)),
                  pl.BlockSpec((3, nb, tb, dr), lambda ti: (0, first_blk, nt - 1 - ti, 0)),
                  pl.BlockSpec((1, 3, nb, tb, dr), lambda ti: (0, 0, first_blk, ti, 0)),
                  pl.BlockSpec((1, 3, nb, tb, dr), lambda ti: (1, 0, first_blk, nt - 1 - ti, 0)),
                  _full(s0.shape)],
        out_specs=[pl.BlockSpec((nb, tb, dr), lambda ti: (0, ti, 0)),
                   pl.BlockSpec((nb, tb, dr), lambda ti: (0, nt - 1 - ti, 0))],
        out_shape=[out, out],
        scratch_shapes=[pltpu.VMEM(s0.shape, F32)],
        compiler_params=_params("arbitrary"),
        name="wkv_lat",
    )(shared, shared, perdir, perdir, s0)


def _dft_mats(l):
    n = 2 * l
    nfp = l + SUBLANES
    kf = np.arange(nfp, dtype=np.int64)
    ang = (2.0 * np.pi / n) * ((kf[:, None] * np.arange(l, dtype=np.int64)[None, :]) % n)
    valid = (kf <= l)[:, None]
    c = np.where(valid, np.cos(ang), 0.0)
    s = np.where(valid, np.sin(ang), 0.0)
    wk = np.where((kf == 0) | (kf == l), 1.0 / n, 2.0 / n)[None, :]
    mats = (c, s, c.T * wk, -(s.T) * wk)
    return [m for mat in mats for m in _split(jnp.asarray(mat.astype(np.float32)))]


def _hy_filter_kernel(zf_ref, w1_ref, b1_ref, w2_ref, b2_ref, fq_ref, w3_ref, dl_ref, bias_ref,
                      ch_ref, cl_ref, sh_ref, sl_ref, g_ref):
    zf = zf_ref[...]
    hid = jnp.sin(fq_ref[0:1] * (_dot3(zf, w1_ref[...]) + b1_ref[...]))
    hid = jnp.sin(fq_ref[1:2] * (_dot3(hid, w2_ref[...]) + b2_ref[...]))
    win = jnp.exp(-zf[:, 0:1] * jnp.abs(dl_ref[...]))
    hh, hl = _split(hid)
    filt = []
    for o in range(4):
        wh, wl = _split(w3_ref[o])
        filt.append(_dot_split(hh, hl, wh, wl) * win)
    for od in range(2):
        hf, hbk = filt[2 * od], filt[2 * od + 1]
        ph, plo = _split(hf + hbk)
        mh, ml = _split(hbk - hf)
        g_ref[od, 0] = _dot_split(ch_ref[...], cl_ref[...], ph, plo) + bias_ref[od:od + 1]
        g_ref[od, 1] = _dot_split(sh_ref[...], sl_ref[...], mh, ml)


def _hy_filter(zf, w1, b1, w2, b2, fq, w3, deltas, bias, mats):
    l = zf.shape[0]
    nfp = mats[0].shape[0]
    c = w3.shape[-1]
    cb = LANES
    return pl.pallas_call(
        _hy_filter_kernel,
        grid=(c // cb,),
        in_specs=[_full(zf.shape), _full(w1.shape), _full(b1.shape), _full(w2.shape), _full(b2.shape),
                  _full(fq.shape),
                  pl.BlockSpec((4, w3.shape[1], cb), lambda j: (0, 0, j)),
                  pl.BlockSpec((1, cb), lambda j: (0, j)),
                  pl.BlockSpec((2, cb), lambda j: (0, j)),
                  _const((nfp, l)), _const((nfp, l)), _const((nfp, l)), _const((nfp, l))],
        out_specs=pl.BlockSpec((2, 2, nfp, cb), lambda j: (0, 0, 0, j)),
        out_shape=jax.ShapeDtypeStruct((2, 2, nfp, c), F32),
        compiler_params=_params("arbitrary"),
        name="hy_filt",
    )(zf, w1, b1, w2, b2, fq, w3, deltas, bias, *mats[:4])


def _hy_conv_kernel(hy_ref, g_ref, ch_ref, cl_ref, sh_ref, sl_ref, cih_ref, cil_ref, sih_ref, sil_ref, o_ref):
    def conv(u, od):
        uh, ul = _split(u)
        re = _dot_split(ch_ref[...], cl_ref[...], uh, ul)
        im = -_dot_split(sh_ref[...], sl_ref[...], uh, ul)
        gre = g_ref[od, 0]
        gim = g_ref[od, 1]
        ah, al = _split(re * gre - im * gim)
        bh, bl = _split(re * gim + im * gre)
        return (_dot_split(cih_ref[...], cil_ref[...], ah, al)
                + _dot_split(sih_ref[...], sil_ref[...], bh, bl))

    z = hy_ref[1] * conv(hy_ref[0], 0)
    o_ref[...] = hy_ref[2] * conv(z, 1)


def _hy_conv(hy, g, mats, l, first_seq, n_seq):
    t, c = hy.shape[1], hy.shape[2]
    nfp = mats[0].shape[0]
    cb = 2 * LANES
    return pl.pallas_call(
        _hy_conv_kernel,
        grid=(n_seq, c // cb),
        in_specs=[pl.BlockSpec((3, l, cb), lambda b, j: (0, first_seq + b, j)),
                  pl.BlockSpec((2, 2, nfp, cb), lambda b, j: (0, 0, 0, j)),
                  _const((nfp, l)), _const((nfp, l)), _const((nfp, l)), _const((nfp, l)),
                  _const((l, nfp)), _const((l, nfp)), _const((l, nfp)), _const((l, nfp))],
        out_specs=pl.BlockSpec((l, cb), lambda b, j: (b, j)),
        out_shape=jax.ShapeDtypeStruct((n_seq * l, c), F32),
        compiler_params=_params("arbitrary", "arbitrary"),
        name="hy_conv",
    )(hy, g, *mats)


def _mix_post_kernel(n_ctx_tiles, ycf_ref, ylf_ref, ycb_ref, ylb_ref, gb_ref, zc_ref, zl_ref, xp_ref, xs_ref,
                     mod_ref, vec_ref, wout_ref, nf_ref, hs_ref, x1_ref, h2_ref):
    d = xp_ref.shape[1]
    dr = hs_ref.shape[0]
    is_ctx = pl.program_id(0) < n_ctx_tiles
    hs = hs_ref[...]
    vec = vec_ref[...]
    mod = mod_ref[0]
    gate1 = mod[:, 2 * d:3 * d]
    shift2 = mod[:, 3 * d:4 * d]
    scale2 = mod[:, 4 * d:5 * d]
    y = jnp.where(is_ctx, ycf_ref[...] + ycb_ref[...], ylf_ref[...] + ylb_ref[...])
    mu = _headsum(y, hs) * (1.0 / HEAD)
    yc = y - mu
    var = _headsum(yc * yc, hs) * (1.0 / HEAD)
    yn = yc * lax.rsqrt(var + GN_EPS) * vec[0:1] + vec[1:2]
    ya = (yn + gb_ref[1]) * gb_ref[0]
    z = jnp.where(is_ctx, zc_ref[...], zl_ref[...])
    yb = z * lax.rsqrt(jnp.mean(z * z, axis=-1, keepdims=True) + EPS) * vec[2:3]
    m = _dot(ya.astype(BF16), wout_ref[0:dr]) + _dot(yb.astype(BF16), wout_ref[dr:])
    x1 = jnp.where(is_ctx, xp_ref[...], xs_ref[...]) + gate1 * m
    x1_ref[...] = x1
    xn = x1 * lax.rsqrt(jnp.mean(x1 * x1, axis=-1, keepdims=True) + EPS) * nf_ref[...]
    h2_ref[...] = (xn * (1.0 + scale2) + shift2).astype(BF16)


def _mix_post(y_fwd, y_bwd, gb, z, x, mods, n_ctx_tiles, tiles_per_lat, tm, vecs, w_out, norm_ffn, hs):
    d = x[0].shape[1]
    t = x[0].shape[0] + x[1].shape[0]
    dr = hs.shape[0]

    def mod_idx(i):
        return (jnp.where(i < n_ctx_tiles, 0, 1 + (i - n_ctx_tiles) // tiles_per_lat), 0, 0)

    tok = _two_part((tm, dr), n_ctx_tiles)
    tokd = pl.BlockSpec((tm, d), lambda i: (i, 0))
    return pl.pallas_call(
        functools.partial(_mix_post_kernel, n_ctx_tiles),
        grid=(t // tm,),
        in_specs=[*tok, *tok, pl.BlockSpec((2, tm, dr), lambda i: (0, i, 0)), *tok,
                  *_two_part((tm, d), n_ctx_tiles),
                  pl.BlockSpec((1, 1, mods.shape[-1]), mod_idx),
                  _full(vecs.shape), _full(w_out.shape), _full(norm_ffn.shape), _full(hs.shape)],
        out_specs=[tokd, tokd],
        out_shape=[jax.ShapeDtypeStruct((t, d), F32), jax.ShapeDtypeStruct((t, d), BF16)],
        compiler_params=_params("arbitrary"),
        name="mix_post",
    )(*y_fwd, *y_bwd, gb, *z, *x, mods, vecs, w_out, norm_ffn, hs)


def _second_level_pairs():
    return [(a, b) for a in range(TOPK) for b in range(TOPK) if (a + 1) * (b + 1) <= TOPK]


def _route_kernel(h2_ref, wq_ref, keys_ref, r2_ref, b1_ref, e1_ref, e2_ref, sv_scr, s_scr, rk_scr, cnt_scr):
    nh = keys_ref.shape[1]
    nk = keys_ref.shape[2]
    half = keys_ref.shape[3]
    tt = h2_ref.shape[0]
    q = _dot(h2_ref[...], wq_ref[...])
    key_id = lax.broadcasted_iota(jnp.int32, (nk, LANES), 0).astype(F32)
    neg = jnp.float32(-jnp.inf)

    for h in range(nh):
        for p in range(2):
            c0 = (h * 2 + p) * half
            qh, ql = _split(q[:, c0:c0 + half])
            kh, kl = _split(keys_ref[p, h])
            dims = (((1,), (1,)), ((), ()))
            s = (lax.dot_general(kh, qh, dims, preferred_element_type=F32)
                 + (lax.dot_general(kl, qh, dims, preferred_element_type=F32)
                    + lax.dot_general(kh, ql, dims, preferred_element_type=F32)))
            s_scr[p, h] = s

    nchunk = tt // LANES

    def first_level_quick():
        def chunk(c, ranked):
            col = pl.ds(pl.multiple_of(c * LANES, LANES), LANES)
            for h in range(nh):
                def extract(r, carry, h=h):
                    out = []
                    for p in range(2):
                        m_prev, above = carry[p]
                        s = s_scr[p, h, :, col]
                        below = s < m_prev
                        m = jnp.max(jnp.where(below, s, neg), axis=0, keepdims=True)
                        sv_scr[p, r, h:h + 1, col] = m
                        out.append((m, above + jnp.where(below, 1.0, 0.0)))
                    return tuple(out)

                start = (jnp.full((1, LANES), jnp.inf, F32), jnp.zeros((nk, LANES), F32))
                done = lax.fori_loop(0, TOPK, extract, (start, start))
                for p in range(2):
                    m_last, above = done[p]
                    rank = above - 1.0 + jnp.where(s_scr[p, h, :, col] < m_last, 1.0, 0.0)
                    rk_scr[p, h, :, col] = rank
                    ranked = jnp.maximum(ranked, jnp.sum(jnp.where(rank < float(TOPK), 1.0, 0.0),
                                                         axis=0, keepdims=True))
            return ranked

        return lax.fori_loop(0, nchunk, chunk, jnp.zeros((1, LANES), F32))

    def first_level_exact():
        def chunk(c, carry):
            col = pl.ds(pl.multiple_of(c * LANES, LANES), LANES)
            for h in range(nh):
                for p in range(2):
                    def extract(r, xr, p=p, h=h):
                        x, rank = xr
                        m = jnp.max(x, axis=0, keepdims=True)
                        first = jnp.min(jnp.where(x == m, key_id, float(nk)), axis=0, keepdims=True)
                        sel = key_id == first
                        sv_scr[p, r, h:h + 1, col] = m
                        return jnp.where(sel, neg, x), jnp.where(sel, lax.convert_element_type(r, F32), rank)

                    _, rank = lax.fori_loop(0, TOPK, extract,
                                            (s_scr[p, h, :, col], jnp.full((nk, LANES), float(TOPK), F32)))
                    rk_scr[p, h, :, col] = rank
            return carry

        lax.fori_loop(0, nchunk, chunk, 0)

    ranked = first_level_quick()

    @pl.when(jnp.max(ranked) > float(TOPK))
    def _():
        first_level_exact()

    pairs = _second_level_pairs()

    def second_level(c, carry):
        col = pl.ds(pl.multiple_of(c * LANES, LANES), LANES)
        v1 = [sv_scr[0, a, :, col] for a in range(TOPK)]
        v2 = [sv_scr[1, b, :, col] for b in range(TOPK)]
        top = v1[0] + v2[0]

        def select(_, carry):
            cand, cnt, zsum = list(carry[0]), list(carry[1]), carry[2]
            m = cand[0]
            for cv in cand[1:]:
                m = jnp.maximum(m, cv)
            zsum = zsum + jnp.exp(m - top)
            found = jnp.zeros((nh, LANES), F32)
            for ci, (a, _b) in enumerate(pairs):
                eq = jnp.where(cand[ci] == m, 1.0, 0.0)
                hit = eq * (1.0 - found)
                found = jnp.maximum(found, eq)
                cand[ci] = jnp.where(hit > 0.0, neg, cand[ci])
                cnt[a] = cnt[a] + hit
            return tuple(cand), tuple(cnt), zsum

        zero = jnp.zeros((nh, LANES), F32)
        _, cnt, zsum = lax.fori_loop(0, TOPK, select,
                                     (tuple(v1[a] + v2[b] for a, b in pairs), (zero,) * TOPK, zero))
        for a in range(TOPK):
            cnt_scr[a, :, col] = cnt[a]
        cnt_scr[TOPK, :, col] = 1.0 / zsum
        return carry

    lax.fori_loop(0, nchunk, second_level, 0)

    for h in range(nh):
        rank1 = rk_scr[0, h]
        b1 = jnp.zeros((nk, tt), F32)
        for a in range(TOPK):
            b1 = b1 + jnp.where(rank1 == float(a), cnt_scr[a, h:h + 1, :], 0.0)
        b1_ref[h] = b1
        r2_ref[h] = rk_scr[1, h].astype(BF16)
        e1_ref[h] = jnp.exp(s_scr[0, h] - sv_scr[0, 0, h:h + 1, :]) * cnt_scr[TOPK, h:h + 1, :]
        e2_ref[h] = jnp.exp(s_scr[1, h] - sv_scr[1, 0, h:h + 1, :]).astype(BF16)


def _route(h2, wq, keys, tt):
    t, d = h2.shape
    _, nh, nk, _ = keys.shape
    out = jax.ShapeDtypeStruct((nh, nk, t), F32)
    out16 = jax.ShapeDtypeStruct((nh, nk, t), BF16)
    ospec = pl.BlockSpec((nh, nk, tt), lambda i: (0, 0, i))
    return pl.pallas_call(
        _route_kernel,
        grid=(t // tt,),
        in_specs=[pl.BlockSpec((tt, d), lambda i: (i, 0)), _full(wq.shape), _full(keys.shape)],
        out_specs=[ospec, ospec, ospec, ospec],
        out_shape=[out16, out, out, out16],
        scratch_shapes=[pltpu.VMEM((2, TOPK, nh, tt), F32),
                        pltpu.VMEM((2, nh, nk, tt), F32),
                        pltpu.VMEM((2, nh, nk, tt), F32),
                        pltpu.VMEM((TOPK + 1, nh, tt), F32)],
        compiler_params=_params("arbitrary"),
        name="route",
    )(h2, wq, keys)


def _peer_kernel(n_ctx_tiles, h2_ref, u_ref, v_ref, r2_ref, e2_ref, b1_ref, e1_ref, x1_ref, mod_ref, fn_ref,
                 op_ref, os_ref, acc_scr, a_scr):
    d = x1_ref.shape[1]
    nh, nk, tt = r2_ref.shape
    n1 = b1_ref.shape[1]
    eb = pl.program_id(1)

    @pl.when(eb == 0)
    def _():
        acc_scr[...] = jnp.zeros_like(acc_scr)

    hid = lax.dot_general(u_ref[...].astype(BF16), h2_ref[...], (((1,), (1,)), ((), ())),
                          preferred_element_type=F32)
    pk = 2 * SUBLANES
    for i in range(n1):
        w = jnp.zeros((nk // pk, pk, tt), BF16)
        for h in range(nh):
            b1r = jnp.broadcast_to(b1_ref[h, i:i + 1, :], (pk, tt)).astype(BF16)[None]
            e1r = jnp.broadcast_to(e1_ref[h, i:i + 1, :], (pk, tt)).astype(BF16)[None]
            r2 = r2_ref[h].reshape(nk // pk, pk, tt)
            e2 = e2_ref[h].reshape(nk // pk, pk, tt)
            w = w + jnp.where(r2 < b1r, e1r * e2, jnp.zeros((), BF16))
        hi = hid[i * nk:(i + 1) * nk]
        act = 0.5 * hi * (1.0 + lax.erf(hi * (1.0 / math.sqrt(2.0))))
        a_scr[i * nk:(i + 1) * nk, :] = (act.astype(BF16).reshape(nk // pk, pk, tt) * w).reshape(nk, tt)
    acc_scr[...] += lax.dot_general(v_ref[...], a_scr[...], (((0,), (0,)), ((), ())),
                                    preferred_element_type=F32)

    @pl.when(eb == pl.num_programs(1) - 1)
    def _():
        gate2 = mod_ref[0][:, 5 * d:6 * d]
        xo = x1_ref[...] + gate2 * acc_scr[...].T
        res = xo * lax.rsqrt(jnp.mean(xo * xo, axis=-1, keepdims=True) + EPS) * fn_ref[...]
        is_ctx = pl.program_id(0) < n_ctx_tiles

        @pl.when(is_ctx)
        def _():
            op_ref[...] = res

        @pl.when(jnp.logical_not(is_ctx))
        def _():
            os_ref[...] = res


def _peer(h2, u, v, r2, e2, b1, e1, x1, mods, final_norm, n_ctx_tiles, tiles_per_lat, tt, n1):
    t, d = h2.shape
    ne = u.shape[0]
    nh, nk, _ = r2.shape
    eb = n1 * nk

    def mod_idx(i, e):
        return (jnp.where(i < n_ctx_tiles, 0, 1 + (i - n_ctx_tiles) // tiles_per_lat), 0, 0)

    dense = pl.BlockSpec((nh, nk, tt), lambda i, e: (0, 0, i))
    rows = pl.BlockSpec((nh, n1, tt), lambda i, e: (0, e, i))
    tc = n_ctx_tiles * tt
    return pl.pallas_call(
        functools.partial(_peer_kernel, n_ctx_tiles),
        grid=(t // tt, ne // eb),
        in_specs=[pl.BlockSpec((tt, d), lambda i, e: (i, 0)),
                  pl.BlockSpec((eb, d), lambda i, e: (e, 0)),
                  pl.BlockSpec((eb, d), lambda i, e: (e, 0)),
                  dense, dense, rows, rows,
                  pl.BlockSpec((tt, d), lambda i, e: (i, 0)),
                  pl.BlockSpec((1, 1, mods.shape[-1]), mod_idx),
                  _full(final_norm.shape)],
        out_specs=list(_two_part((tt, d), n_ctx_tiles)),
        out_shape=[jax.ShapeDtypeStruct((tc, d), F32), jax.ShapeDtypeStruct((t - tc, d), F32)],
        scratch_shapes=[pltpu.VMEM((d, tt), F32), pltpu.VMEM((eb, tt), BF16)],
        compiler_params=_params("arbitrary", "arbitrary"),
        name="peer",
    )(h2, u, v, r2, e2, b1, e1, x1, mods, final_norm)


def _hyena_features(l, bands):
    t = jnp.linspace(0.0, 1.0, l, dtype=F32)[:, None]
    wpos = 2.0 * math.pi * jnp.arange(l, dtype=F32)[:, None] / l
    f = jnp.linspace(1e-4, bands - 1, bands, dtype=F32)[None, :]
    return jnp.concatenate([t, jnp.cos(f * wpos), -jnp.sin(f * wpos)], axis=-1)


def kernel(x_prompt, x_sample, state_rwkv, c, c_ctx, w_ada, b_ada, norm_mix, norm_ffn, w_in, conv_w, w_out, rwkv_w0, rwkv_w1, rwkv_w2, rwkv_a0, rwkv_a1, rwkv_a2, rwkv_g1, rwkv_g2, rwkv_k_k, rwkv_k_a, rwkv_r_k, rwkv_ln_w, rwkv_ln_b, hy_f_w1, hy_f_b1, hy_f_w2, hy_f_b2, hy_freq, hy_f_w3, hy_bias, hy_norm, peer_wq, peer_keys, peer_u, peer_v, final_norm):
    bc, lc, d = x_prompt.shape
    bl, ll, _ = x_sample.shape
    depth = w_ada.shape[0]
    assert depth == 1
    dr = rwkv_w0.shape[-1]
    dh = hy_norm.shape[-1]
    nh = dr // HEAD
    assert dr == dh and w_in.shape[-1] == 3 * dr + 3 * dh
    tc, tl = bc * lc, bl * ll
    t = tc + tl
    tm = lc
    assert ll % tm == 0 and tm % GRID_W == 0 and tc % ll == 0
    n_ctx_tiles = tc // tm
    tiles_per_lat = ll // tm
    parts = LANES // (bl * (nh // 2))
    assert parts * bl * (nh // 2) == LANES and HEAD % parts == 0 and tc % tl == 0
    nil = HEAD // parts
    l = 0

    rows = jnp.concatenate([c_ctx[None, :], c, jnp.zeros((SUBLANES - 1 - bl, d), F32)], axis=0)
    mods = _ada(rows, w_ada[l], b_ada[l]).reshape(SUBLANES, 1, 6 * d)

    xp = x_prompt.reshape(tc, d)
    xs = x_sample.reshape(tl, d)

    def pad_cols(w):
        return jnp.pad(w, ((0, 0), (0, LANES - w.shape[1])))

    def pad_rows(w):
        return jnp.pad(w, ((0, LANES - w.shape[0]), (0, 0)))

    lora1 = jnp.concatenate([pad_cols(rwkv_w1[l, 0]), pad_cols(rwkv_w1[l, 1]), pad_cols(rwkv_a1[l, 0]),
                             pad_cols(rwkv_a1[l, 1]), pad_cols(rwkv_g1[l])], axis=1).astype(BF16)
    w2 = jnp.stack([pad_rows(rwkv_w2[l, 0]), pad_rows(rwkv_w2[l, 1])])
    a2 = jnp.stack([pad_rows(rwkv_a2[l, 0]), pad_rows(rwkv_a2[l, 1])])
    vecs_pre = jnp.stack([rwkv_w0[l, 0], rwkv_w0[l, 1], rwkv_a0[l, 0], rwkv_a0[l, 1], rwkv_k_k[l], rwkv_k_a[l],
                          rwkv_r_k[l, 0].reshape(dr), rwkv_r_k[l, 1].reshape(dr)])
    head_id = jnp.arange(dr, dtype=jnp.int32) // HEAD
    hs = (head_id[:, None] == head_id[None, :]).astype(BF16)
    shared, perdir, gb, hy = _mix_pre(xp, xs, mods, n_ctx_tiles, tiles_per_lat, tm, norm_mix[l][None, :],
                                      w_in[l].astype(BF16), conv_w[l], lora1, w2, a2, rwkv_g2[l], vecs_pre, hs)

    shared4 = shared.reshape(3, t // lc, lc, dr)
    perdir4 = perdir.reshape(2, 3, t // lc, lc, dr)
    y_cf, s_cf = _wkv_ctx(shared4, perdir4, bc, 8, False)
    y_cb, s_cb = _wkv_ctx(shared4, perdir4, bc, 8, True)
    shared_l = shared.reshape(3, t // ll, ll, dr)
    perdir_l = perdir.reshape(2, 3, t // ll, ll, dr)
    s0 = state_rwkv[:, l].reshape(bl, 2, nh // 2, 2, nil, parts, HEAD).transpose(1, 3, 4, 6, 2, 0, 5)
    s0 = s0.reshape(2, 2 * nil, HEAD, LANES)
    y_lf, y_lb = _wkv_lat(shared_l, perdir_l, s0, tc // tl, bl, 16)
    new_state = jnp.stack([s_cf, s_cb]).reshape(2, 2, HEAD, HEAD, nh // 2, bc).transpose(5, 0, 4, 1, 2, 3)
    new_state = new_state.reshape(bc, 1, 2, nh, HEAD, HEAD)

    bands = (hy_f_w1.shape[1] - 1) // 2
    femb = hy_f_w1.shape[1]
    fpad = 32
    w1p = jnp.pad(hy_f_w1[l], ((0, fpad - femb), (0, 0)))
    hidden = hy_f_w2.shape[-1]
    w3 = hy_f_w3[l].reshape(hidden, 4, dh).transpose(1, 0, 2)
    max_decay = math.log(1e-2) / 0.3
    min_decay = math.log(1e-2) / 1.5
    deltas = jnp.linspace(min_decay, max_decay, dh, dtype=F32)[None, :]
    z_parts = []
    for seq_len, first_seq, n_seq in ((lc, 0, bc), (ll, tc // ll, bl)):
        mats = _dft_mats(seq_len)
        zf = jnp.pad(_hyena_features(seq_len, bands), ((0, 0), (0, fpad - femb)))
        g = _hy_filter(zf, w1p, hy_f_b1[l][None, :], hy_f_w2[l], hy_f_b2[l][None, :], hy_freq[l], w3,
                       deltas, hy_bias[l], mats)
        z_parts.append(_hy_conv(hy, g, mats, seq_len, first_seq, n_seq))

    vecs_post = jnp.concatenate([jnp.stack([rwkv_ln_w[l], rwkv_ln_b[l], hy_norm[l]]),
                                 jnp.zeros((SUBLANES - 3, dr), F32)], axis=0)
    x1, h2 = _mix_post((y_cf.reshape(tc, dr), y_lf.reshape(tl, dr)), (y_cb.reshape(tc, dr), y_lb.reshape(tl, dr)),
                       gb, z_parts, (xp, xs), mods, n_ctx_tiles, tiles_per_lat, tm, vecs_post,
                       w_out[l].astype(BF16), norm_ffn[l][None, :], hs)

    r2, b1, e1, e2 = _route(h2, peer_wq[l].astype(BF16), peer_keys[l], tt=512)
    tt = 512
    assert ll % tt == 0 and tc % tt == 0
    y_p, y_s = _peer(h2, peer_u[l], peer_v[l].astype(BF16), r2, e2, b1, e1, x1, mods,
                final_norm[None, :], tc // tt, ll // tt, tt, n1=SUBLANES)
    return y_p.reshape(bc, lc, d), y_s.reshape(bl, ll, d), new_state
```

```python
import functools
import math

import jax
import jax.numpy as jnp
import numpy as np
from jax import lax
from jax.experimental import pallas as pl
from jax.experimental.pallas import tpu as pltpu

F32 = jnp.float32
BF16 = jnp.bfloat16

GRID_W = 64
HEAD = 64
TOPK = 16
EPS = 1e-6
GN_EPS = 64e-5
LANES = 128
SUBLANES = 8
VMEM_LIMIT = 56 * 1024 * 1024


def _params(*sem):
    return pltpu.CompilerParams(dimension_semantics=sem, vmem_limit_bytes=VMEM_LIMIT)


def _split(x):
    hi = x.astype(BF16)
    lo = (x - hi.astype(F32)).astype(BF16)
    return hi, lo


def _dot(a, b):
    return jnp.dot(a, b, preferred_element_type=F32)


def _dot_split(ah, al, bh, bl):
    return _dot(ah, bh) + (_dot(al, bh) + _dot(ah, bl))


def _dot3(a, b):
    ah, al = _split(a)
    bh, bl = _split(b)
    return _dot_split(ah, al, bh, bl)


def _full(shape):
    n = len(shape)
    return pl.BlockSpec(shape, lambda *_: (0,) * n)


def _two_part(block, n_first):
    return (pl.BlockSpec(block, lambda i, *_: (jnp.minimum(i, n_first - 1), 0)),
            pl.BlockSpec(block, lambda i, *_: (jnp.maximum(i - n_first, 0), 0)))


def _const(shape):
    n = len(shape)
    return pl.BlockSpec(shape, lambda *_: (0,) * n, pipeline_mode=pl.Buffered(1))


def _ada_kernel(c_ref, w_ref, b_ref, o_ref):
    c = c_ref[...]
    s = c * jax.nn.sigmoid(c)
    o_ref[...] = _dot3(s, w_ref[...]) + b_ref[...]


def _ada(c_rows, w, b):
    rows, d = c_rows.shape
    n = w.shape[1]
    bn = n // 4
    return pl.pallas_call(
        _ada_kernel,
        grid=(4,),
        in_specs=[_full((rows, d)),
                  pl.BlockSpec((d, bn), lambda j: (0, j)),
                  pl.BlockSpec((1, bn), lambda j: (0, j))],
        out_specs=pl.BlockSpec((rows, bn), lambda j: (0, j)),
        out_shape=jax.ShapeDtypeStruct((rows, n), F32),
        compiler_params=_params("arbitrary"),
        name="ada",
    )(c_rows, w, b.reshape(1, n))


def _headsum(z, hs):
    zh, zl = _split(z)
    return _dot(zh, hs) + _dot(zl, hs)


def _mix_pre_kernel(n_ctx_tiles, xp_ref, xs_ref, mod_ref, nm_ref, win_ref, cw_ref, l1_ref, w2_ref, a2_ref,
                    g2_ref, vec_ref, hs_ref, sh_ref, dr_ref, gb_ref, hy_ref):
    tm, d = xp_ref.shape
    dr = hs_ref.shape[0]
    is_ctx = pl.program_id(0) < n_ctx_tiles
    x = jnp.where(is_ctx, xp_ref[...], xs_ref[...])
    mod = mod_ref[0]
    shift1 = mod[:, 0:d]
    scale1 = mod[:, d:2 * d]
    xn = x * lax.rsqrt(jnp.mean(x * x, axis=-1, keepdims=True) + EPS) * nm_ref[...]
    hb = (xn * (1.0 + scale1) + shift1).astype(BF16)

    pos = lax.broadcasted_iota(jnp.int32, (tm, dr), 0)
    row = pos % GRID_W
    keep_prev = jnp.where(is_ctx, jnp.where(pos == 0, 0.0, 1.0), jnp.where(row == 0, 0.0, 1.0))
    keep_next = jnp.where(is_ctx, jnp.where(pos == tm - 1, 0.0, 1.0),
                          jnp.where(row == GRID_W - 1, 0.0, 1.0))

    def proj(c):
        u = _dot(hb, win_ref[:, c * dr:(c + 1) * dr])
        cw = cw_ref[:, c * dr:(c + 1) * dr]
        up = pltpu.roll(u, 1, 0) * keep_prev
        un = pltpu.roll(u, tm - 1, 0) * keep_next
        return up * cw[0:1] + u * cw[1:2] + un * cw[2:3]

    r = proj(0)
    k = proj(1)
    v = proj(2)
    hy_ref[0] = proj(3)
    hy_ref[1] = proj(4)
    hy_ref[2] = proj(5)

    lo = _dot(hb, l1_ref[...])
    vec = vec_ref[...]
    k_k = vec[4:5]
    k_a = vec[5:6]
    hs = hs_ref[...]
    kkr = k * k_k
    kk = kkr * lax.rsqrt(jnp.maximum(_headsum(kkr * kkr, hs), 1e-24))
    sh_ref[0] = r
    sh_ref[1] = kk
    sh_ref[2] = v
    gb_ref[0] = _dot3(jax.nn.sigmoid(lo[:, 4 * LANES:5 * LANES]), g2_ref[...])
    bon = jnp.zeros_like(r)
    for dd in range(2):
        w = vec[dd:dd + 1] + _dot3(jnp.tanh(lo[:, dd * LANES:(dd + 1) * LANES]), w2_ref[dd])
        z = -w
        softplus = jnp.maximum(z, 0.0) + jnp.log1p(jnp.exp(-jnp.abs(z)))
        dr_ref[dd, 0] = jnp.exp(-jnp.exp(-softplus - 0.5))
        a = jax.nn.sigmoid(vec[2 + dd:3 + dd]
                           + _dot3(lo[:, (2 + dd) * LANES:(3 + dd) * LANES], a2_ref[dd]))
        kd = k * (1.0 + (a - 1.0) * k_a)
        dr_ref[dd, 1] = kd
        dr_ref[dd, 2] = kk * a
        bon = bon + r * kd * vec[6 + dd:7 + dd]
    gb_ref[1] = _headsum(bon, hs) * v


def _mix_pre(xp, xs, mods, n_ctx_tiles, tiles_per_lat, tm, norm_mix, w_in, conv_w, lora1, w2, a2, g2, vecs, hs):
    d = xp.shape[1]
    t = xp.shape[0] + xs.shape[0]
    dr = hs.shape[0]
    n_tiles = t // tm

    def mod_idx(i):
        return (jnp.where(i < n_ctx_tiles, 0, 1 + (i - n_ctx_tiles) // tiles_per_lat), 0, 0)

    return pl.pallas_call(
        functools.partial(_mix_pre_kernel, n_ctx_tiles),
        grid=(n_tiles,),
        in_specs=[*_two_part((tm, d), n_ctx_tiles),
                  pl.BlockSpec((1, 1, mods.shape[-1]), mod_idx),
                  _full(norm_mix.shape), _full(w_in.shape), _full(conv_w.shape), _full(lora1.shape),
                  _full(w2.shape), _full(a2.shape), _full(g2.shape), _full(vecs.shape), _full(hs.shape)],
        out_specs=[pl.BlockSpec((3, tm, dr), lambda i: (0, i, 0)),
                   pl.BlockSpec((2, 3, tm, dr), lambda i: (0, 0, i, 0)),
                   pl.BlockSpec((2, tm, dr), lambda i: (0, i, 0)),
                   pl.BlockSpec((3, tm, dr), lambda i: (0, i, 0))],
        out_shape=[jax.ShapeDtypeStruct((3, t, dr), F32),
                   jax.ShapeDtypeStruct((2, 3, t, dr), F32),
                   jax.ShapeDtypeStruct((2, t, dr), F32),
                   jax.ShapeDtypeStruct((3, t, dr), F32)],
        compiler_params=_params("arbitrary"),
        name="mix_pre",
    )(xp, xs, mods, norm_mix, w_in, conv_w, lora1, w2, a2, g2, vecs, hs)


def _wkv_ctx_kernel(reverse, sh_ref, pd_ref, y_ref, sfin_ref, s_scr, tr_scr, c_scr, yt_scr):
    nb, tb = sh_ref.shape[1], sh_ref.shape[2]
    nhp = sh_ref.shape[3] // LANES
    tblk = pl.program_id(0)

    @pl.when(tblk == 0)
    def _():
        s_scr[...] = jnp.zeros_like(s_scr)

    def to_lanes(ref, lead, t):
        blocks = [ref[lead + (slice(None), t, slice(hp * LANES, (hp + 1) * LANES))] for hp in range(nhp)]
        return jnp.concatenate(blocks, axis=0).T

    def relayout(t, slot):
        r_t = to_lanes(sh_ref, (0,), t)
        d_t = to_lanes(pd_ref, (0, 0), t)
        k_t = to_lanes(pd_ref, (0, 1), t)
        kka_t = to_lanes(pd_ref, (0, 2), t)
        tr_scr[slot, 0] = d_t * r_t
        tr_scr[slot, 1] = to_lanes(sh_ref, (1,), t)
        tr_scr[slot, 2] = to_lanes(sh_ref, (2,), t)
        tr_scr[slot, 3] = d_t
        tr_scr[slot, 4] = k_t
        tr_scr[slot, 5] = kka_t
        for h2 in range(2):
            rows = slice(h2 * HEAD, (h2 + 1) * HEAD)
            c_scr[slot, 2 * h2] = jnp.broadcast_to(jnp.sum(k_t[rows] * r_t[rows], axis=0, keepdims=True),
                                                   (SUBLANES, LANES))
            c_scr[slot, 2 * h2 + 1] = jnp.broadcast_to(jnp.sum(kka_t[rows] * r_t[rows], axis=0, keepdims=True),
                                                       (SUBLANES, LANES))

    def time_of(tau):
        return tb - 1 - tau if reverse else tau

    relayout(time_of(0), 0)
    for tau in range(tb):
        slot = tau % 2
        if tau + 1 < tb:
            relayout(time_of(tau + 1), 1 - slot)
        for h2 in range(2):
            rows = slice(h2 * HEAD, (h2 + 1) * HEAD)
            c1 = c_scr[slot, 2 * h2, 0:1, :]
            c2 = c_scr[slot, 2 * h2 + 1, 0:1, :]
            for i8 in range(HEAD // SUBLANES):
                base = h2 * HEAD + i8 * SUBLANES
                vblk = tr_scr[slot, 2, base:base + SUBLANES, :]
                outs = []
                for jj in range(SUBLANES):
                    s = s_scr[base + jj]
                    vi = vblk[jj:jj + 1]
                    sa = jnp.sum(s * tr_scr[slot, 1, rows, :], axis=0, keepdims=True)
                    q = jnp.sum(s * tr_scr[slot, 0, rows, :], axis=0, keepdims=True)
                    s_scr[base + jj] = (s * tr_scr[slot, 3, rows, :]
                                        + (vi * tr_scr[slot, 4, rows, :] - sa * tr_scr[slot, 5, rows, :]))
                    outs.append(q + vi * c1 - sa * c2)
                yt_scr[base:base + SUBLANES, :] = jnp.concatenate(outs, axis=0)
        y = yt_scr[...].T
        t = time_of(tau)
        for hp in range(nhp):
            y_ref[:, t, hp * LANES:(hp + 1) * LANES] = y[hp * nb:(hp + 1) * nb]

    @pl.when(tblk == pl.num_programs(0) - 1)
    def _():
        sfin_ref[...] = s_scr[...]


def _wkv_ctx(shared, perdir, nb, tb, reverse):
    _, nseq, lc, dr = shared.shape
    assert nb * (dr // LANES) == LANES and 2 * HEAD == LANES
    nt = lc // tb
    di = 1 if reverse else 0

    def tsel(ti):
        return nt - 1 - ti if reverse else ti

    return pl.pallas_call(
        functools.partial(_wkv_ctx_kernel, reverse),
        grid=(nt,),
        in_specs=[pl.BlockSpec((3, nb, tb, dr), lambda ti: (0, 0, tsel(ti), 0)),
                  pl.BlockSpec((1, 3, nb, tb, dr), lambda ti: (di, 0, 0, tsel(ti), 0))],
        out_specs=[pl.BlockSpec((nb, tb, dr), lambda ti: (0, tsel(ti), 0)),
                   pl.BlockSpec((LANES, HEAD, LANES), lambda ti: (0, 0, 0))],
        out_shape=[jax.ShapeDtypeStruct((nb, lc, dr), F32),
                   jax.ShapeDtypeStruct((LANES, HEAD, LANES), F32)],
        scratch_shapes=[pltpu.VMEM((LANES, HEAD, LANES), F32),
                        pltpu.VMEM((2, 6, LANES, LANES), F32),
                        pltpu.VMEM((2, 4, SUBLANES, LANES), F32),
                        pltpu.VMEM((LANES, LANES), F32)],
        compiler_params=_params("arbitrary"),
        name="wkv_ctx",
    )(shared, perdir)


def _wkv_lat_kernel(shf_ref, shb_ref, pdf_ref, pdb_ref, s0_ref, yf_ref, yb_ref, s_scr):
    nb, tb, dr = shf_ref.shape[1], shf_ref.shape[2], shf_ref.shape[3]
    nhp = dr // LANES
    parts = LANES // (nhp * nb)
    nil = HEAD // parts
    tblk = pl.program_id(0)
    lane_part = lax.broadcasted_iota(jnp.int32, (parts, LANES), 1) % parts
    diag = jnp.where(lane_part == lax.broadcasted_iota(jnp.int32, (parts, LANES), 0), 1.0, 0.0)

    @pl.when(tblk == 0)
    def _():
        s_scr[...] = s0_ref[...]

    def to_lanes(ref, lead, t):
        blocks = [jnp.broadcast_to(ref[lead + (b, slice(t, t + 1), slice(hp * LANES, (hp + 1) * LANES))],
                                   (parts, LANES))
                  for hp in range(nhp) for b in range(nb)]
        return jnp.concatenate(blocks, axis=0).T

    def fetch(sh_ref, pd_ref, t):
        return (to_lanes(sh_ref, (0,), t), to_lanes(sh_ref, (1,), t), to_lanes(sh_ref, (2,), t),
                to_lanes(pd_ref, (0, 0), t), to_lanes(pd_ref, (0, 1), t), to_lanes(pd_ref, (0, 2), t))

    def step(di, fetched, y_ref, t):
        r_t, kk_t, v_t, d_t, k_t, kka_t = fetched
        placed = []
        for h2 in range(2):
            rows = slice(h2 * HEAD, (h2 + 1) * HEAD)
            rt, kkt, dt, kt, kkat = r_t[rows], kk_t[rows], d_t[rows], k_t[rows], kka_t[rows]
            drt = dt * rt
            c1 = jnp.sum(kt * rt, axis=0, keepdims=True)
            c2 = jnp.sum(kkat * rt, axis=0, keepdims=True)
            for il in range(nil):
                r0 = h2 * HEAD + il * parts
                vi = jnp.sum(v_t[r0:r0 + parts] * diag, axis=0, keepdims=True)
                s = s_scr[di, h2 * nil + il]
                sa = jnp.sum(s * kkt, axis=0, keepdims=True)
                q = jnp.sum(s * drt, axis=0, keepdims=True)
                s_scr[di, h2 * nil + il] = s * dt + (vi * kt - sa * kkat)
                placed.append((q + vi * c1 - sa * c2) * diag)
        y = jnp.concatenate(placed, axis=0).T
        for hp in range(nhp):
            for b in range(nb):
                r0 = (hp * nb + b) * parts
                y_ref[b, t:t + 1, hp * LANES:(hp + 1) * LANES] = jnp.sum(y[r0:r0 + parts], axis=0, keepdims=True)

    ahead = (fetch(shf_ref, pdf_ref, 0), fetch(shb_ref, pdb_ref, tb - 1))
    for tau in range(tb):
        now = ahead
        if tau + 1 < tb:
            ahead = (fetch(shf_ref, pdf_ref, tau + 1), fetch(shb_ref, pdb_ref, tb - 2 - tau))
        step(0, now[0], yf_ref, tau)
        step(1, now[1], yb_ref, tb - 1 - tau)


def _wkv_lat(shared, perdir, s0, first_blk, nb, tb):
    _, nseq, ll, dr = shared.shape
    nt = ll // tb
    out = jax.ShapeDtypeStruct((nb, ll, dr), F32)
    return pl.pallas_call(
        _wkv_lat_kernel,
        grid=(nt,),
        in_specs=[pl.BlockSpec((3, nb, tb, dr), lambda ti: (0, first_blk, ti, 0)),
                  pl.BlockSpec((3, nb, tb, dr), lambda ti: (0, first_blk, nt - 1 - ti, 0)),
                  pl.BlockSpec((1, 3, nb, tb, dr), lambda ti: (0, 0, first_blk, ti, 0)),
                  pl.BlockSpec((1, 3, nb, tb, dr), lambda ti: (1, 0, first_blk, nt - 1 - ti, 0)),
                  _full(s0.shape)],
        out_specs=[pl.BlockSpec((nb, tb, dr), lambda ti: (0, ti, 0)),
                   pl.BlockSpec((nb, tb, dr), lambda ti: (0, nt - 1 - ti, 0))],
        out_shape=[out, out],
        scratch_shapes=[pltpu.VMEM(s0.shape, F32)],
        compiler_params=_params("arbitrary"),
        name="wkv_lat",
    )(shared, shared, perdir, perdir, s0)


def _dft_mats(l):
    n = 2 * l
    nfp = l + SUBLANES
    kf = np.arange(nfp, dtype=np.int64)
    ang = (2.0 * np.pi / n) * ((kf[:, None] * np.arange(l, dtype=np.int64)[None, :]) % n)
    valid = (kf <= l)[:, None]
    c = np.where(valid, np.cos(ang), 0.0)
    s = np.where(valid, np.sin(ang), 0.0)
    wk = np.where((kf == 0) | (kf == l), 1.0 / n, 2.0 / n)[None, :]
    mats = (c, s, c.T * wk, -(s.T) * wk)
    return [m for mat in mats for m in _split(jnp.asarray(mat.astype(np.float32)))]


def _hy_filter_kernel(zf_ref, w1_ref, b1_ref, w2_ref, b2_ref, fq_ref, w3_ref, dl_ref, bias_ref,
                      ch_ref, cl_ref, sh_ref, sl_ref, g_ref):
    zf = zf_ref[...]
    hid = jnp.sin(fq_ref[0:1] * (_dot3(zf, w1_ref[...]) + b1_ref[...]))
    hid = jnp.sin(fq_ref[1:2] * (_dot3(hid, w2_ref[...]) + b2_ref[...]))
    win = jnp.exp(-zf[:, 0:1] * jnp.abs(dl_ref[...]))
    hh, hl = _split(hid)
    filt = []
    for o in range(4):
        wh, wl = _split(w3_ref[o])
        filt.append(_dot_split(hh, hl, wh, wl) * win)
    for od in range(2):
        hf, hbk = filt[2 * od], filt[2 * od + 1]
        ph, plo = _split(hf + hbk)
        mh, ml = _split(hbk - hf)
        g_ref[od, 0] = _dot_split(ch_ref[...], cl_ref[...], ph, plo) + bias_ref[od:od + 1]
        g_ref[od, 1] = _dot_split(sh_ref[...], sl_ref[...], mh, ml)


def _hy_filter(zf, w1, b1, w2, b2, fq, w3, deltas, bias, mats):
    l = zf.shape[0]
    nfp = mats[0].shape[0]
    c = w3.shape[-1]
    cb = LANES
    return pl.pallas_call(
        _hy_filter_kernel,
        grid=(c // cb,),
        in_specs=[_full(zf.shape), _full(w1.shape), _full(b1.shape), _full(w2.shape), _full(b2.shape),
                  _full(fq.shape),
                  pl.BlockSpec((4, w3.shape[1], cb), lambda j: (0, 0, j)),
                  pl.BlockSpec((1, cb), lambda j: (0, j)),
                  pl.BlockSpec((2, cb), lambda j: (0, j)),
                  _const((nfp, l)), _const((nfp, l)), _const((nfp, l)), _const((nfp, l))],
        out_specs=pl.BlockSpec((2, 2, nfp, cb), lambda j: (0, 0, 0, j)),
        out_shape=jax.ShapeDtypeStruct((2, 2, nfp, c), F32),
        compiler_params=_params("arbitrary"),
        name="hy_filt",
    )(zf, w1, b1, w2, b2, fq, w3, deltas, bias, *mats[:4])


def _hy_conv_kernel(hy_ref, g_ref, ch_ref, cl_ref, sh_ref, sl_ref, cih_ref, cil_ref, sih_ref, sil_ref, o_ref):
    def conv(u, od):
        uh, ul = _split(u)
        re = _dot_split(ch_ref[...], cl_ref[...], uh, ul)
        im = -_dot_split(sh_ref[...], sl_ref[...], uh, ul)
        gre = g_ref[od, 0]
        gim = g_ref[od, 1]
        ah, al = _split(re * gre - im * gim)
        bh, bl = _split(re * gim + im * gre)
        return (_dot_split(cih_ref[...], cil_ref[...], ah, al)
                + _dot_split(sih_ref[...], sil_ref[...], bh, bl))

    z = hy_ref[1] * conv(hy_ref[0], 0)
    o_ref[...] = hy_ref[2] * conv(z, 1)


def _hy_conv(hy, g, mats, l, first_seq, n_seq):
    t, c = hy.shape[1], hy.shape[2]
    nfp = mats[0].shape[0]
    cb = 2 * LANES
    return pl.pallas_call(
        _hy_conv_kernel,
        grid=(n_seq, c // cb),
        in_specs=[pl.BlockSpec((3, l, cb), lambda b, j: (0, first_seq + b, j)),
                  pl.BlockSpec((2, 2, nfp, cb), lambda b, j: (0, 0, 0, j)),
                  _const((nfp, l)), _const((nfp, l)), _const((nfp, l)), _const((nfp, l)),
                  _const((l, nfp)), _const((l, nfp)), _const((l, nfp)), _const((l, nfp))],
        out_specs=pl.BlockSpec((l, cb), lambda b, j: (b, j)),
        out_shape=jax.ShapeDtypeStruct((n_seq * l, c), F32),
        compiler_params=_params("arbitrary", "arbitrary"),
        name="hy_conv",
    )(hy, g, *mats)


def _mix_post_kernel(n_ctx_tiles, ycf_ref, ylf_ref, ycb_ref, ylb_ref, gb_ref, zc_ref, zl_ref, xp_ref, xs_ref,
                     mod_ref, vec_ref, wout_ref, nf_ref, hs_ref, x1_ref, h2_ref):
    d = xp_ref.shape[1]
    dr = hs_ref.shape[0]
    is_ctx = pl.program_id(0) < n_ctx_tiles
    hs = hs_ref[...]
    vec = vec_ref[...]
    mod = mod_ref[0]
    gate1 = mod[:, 2 * d:3 * d]
    shift2 = mod[:, 3 * d:4 * d]
    scale2 = mod[:, 4 * d:5 * d]
    y = jnp.where(is_ctx, ycf_ref[...] + ycb_ref[...], ylf_ref[...] + ylb_ref[...])
    mu = _headsum(y, hs) * (1.0 / HEAD)
    yc = y - mu
    var = _headsum(yc * yc, hs) * (1.0 / HEAD)
    yn = yc * lax.rsqrt(var + GN_EPS) * vec[0:1] + vec[1:2]
    ya = (yn + gb_ref[1]) * gb_ref[0]
    z = jnp.where(is_ctx, zc_ref[...], zl_ref[...])
    yb = z * lax.rsqrt(jnp.mean(z * z, axis=-1, keepdims=True) + EPS) * vec[2:3]
    m = _dot(ya.astype(BF16), wout_ref[0:dr]) + _dot(yb.astype(BF16), wout_ref[dr:])
    x1 = jnp.where(is_ctx, xp_ref[...], xs_ref[...]) + gate1 * m
    x1_ref[...] = x1
    xn = x1 * lax.rsqrt(jnp.mean(x1 * x1, axis=-1, keepdims=True) + EPS) * nf_ref[...]
    h2_ref[...] = (xn * (1.0 + scale2) + shift2).astype(BF16)


def _mix_post(y_fwd, y_bwd, gb, z, x, mods, n_ctx_tiles, tiles_per_lat, tm, vecs, w_out, norm_ffn, hs):
    d = x[0].shape[1]
    t = x[0].shape[0] + x[1].shape[0]
    dr = hs.shape[0]

    def mod_idx(i):
        return (jnp.where(i < n_ctx_tiles, 0, 1 + (i - n_ctx_tiles) // tiles_per_lat), 0, 0)

    tok = _two_part((tm, dr), n_ctx_tiles)
    tokd = pl.BlockSpec((tm, d), lambda i: (i, 0))
    return pl.pallas_call(
        functools.partial(_mix_post_kernel, n_ctx_tiles),
        grid=(t // tm,),
        in_specs=[*tok, *tok, pl.BlockSpec((2, tm, dr), lambda i: (0, i, 0)), *tok,
                  *_two_part((tm, d), n_ctx_tiles),
                  pl.BlockSpec((1, 1, mods.shape[-1]), mod_idx),
                  _full(vecs.shape), _full(w_out.shape), _full(norm_ffn.shape), _full(hs.shape)],
        out_specs=[tokd, tokd],
        out_shape=[jax.ShapeDtypeStruct((t, d), F32), jax.ShapeDtypeStruct((t, d), BF16)],
        compiler_params=_params("arbitrary"),
        name="mix_post",
    )(*y_fwd, *y_bwd, gb, *z, *x, mods, vecs, w_out, norm_ffn, hs)


def _tree(op, xs):
    xs = list(xs)
    while len(xs) > 1:
        xs = [op(xs[i], xs[i + 1]) for i in range(0, len(xs) - 1, 2)] + ([xs[-1]] if len(xs) % 2 else [])
    return xs[0]


def _second_level_pairs():
    return [(a, b) for a in range(TOPK) for b in range(TOPK) if (a + 1) * (b + 1) <= TOPK]


def _route_kernel(h2_ref, wq_ref, keys_ref, r2_ref, b1_ref, e1_ref, e2_ref, sv_scr, s_scr, rk_scr, cnt_scr):
    nh = keys_ref.shape[1]
    nk = keys_ref.shape[2]
    half = keys_ref.shape[3]
    tt = h2_ref.shape[0]
    q = _dot(h2_ref[...], wq_ref[...])
    key_id = lax.broadcasted_iota(jnp.int32, (nk, LANES), 0).astype(F32)
    neg = jnp.float32(-jnp.inf)

    for h in range(nh):
        for p in range(2):
            c0 = (h * 2 + p) * half
            qh, ql = _split(q[:, c0:c0 + half])
            kh, kl = _split(keys_ref[p, h])
            dims = (((1,), (1,)), ((), ()))
            s = (lax.dot_general(kh, qh, dims, preferred_element_type=F32)
                 + (lax.dot_general(kl, qh, dims, preferred_element_type=F32)
                    + lax.dot_general(kh, ql, dims, preferred_element_type=F32)))
            s_scr[p, h] = s

    nchunk = tt // LANES

    def first_level_quick():
        def chunk(c, ranked):
            col = pl.ds(pl.multiple_of(c * LANES, LANES), LANES)
            for h in range(nh):
                def extract(r, carry, h=h):
                    out = []
                    for p in range(2):
                        m_prev, above = carry[p]
                        s = s_scr[p, h, :, col]
                        below = s < m_prev
                        m = jnp.max(jnp.where(below, s, neg), axis=0, keepdims=True)
                        sv_scr[p, r, h:h + 1, col] = m
                        out.append((m, above + jnp.where(below, 1.0, 0.0)))
                    return tuple(out)

                start = (jnp.full((1, LANES), jnp.inf, F32), jnp.zeros((nk, LANES), F32))
                done = lax.fori_loop(0, TOPK, extract, (start, start))
                for p in range(2):
                    m_last, above = done[p]
                    rank = above - 1.0 + jnp.where(s_scr[p, h, :, col] < m_last, 1.0, 0.0)
                    rk_scr[p, h, :, col] = rank
                    ranked = jnp.maximum(ranked, jnp.sum(jnp.where(rank < float(TOPK), 1.0, 0.0),
                                                         axis=0, keepdims=True))
            return ranked

        return lax.fori_loop(0, nchunk, chunk, jnp.zeros((1, LANES), F32))

    def first_level_exact():
        def chunk(c, carry):
            col = pl.ds(pl.multiple_of(c * LANES, LANES), LANES)
            for h in range(nh):
                for p in range(2):
                    def extract(r, xr, p=p, h=h):
                        x, rank = xr
                        m = jnp.max(x, axis=0, keepdims=True)
                        first = jnp.min(jnp.where(x == m, key_id, float(nk)), axis=0, keepdims=True)
                        sel = key_id == first
                        sv_scr[p, r, h:h + 1, col] = m
                        return jnp.where(sel, neg, x), jnp.where(sel, lax.convert_element_type(r, F32), rank)

                    _, rank = lax.fori_loop(0, TOPK, extract,
                                            (s_scr[p, h, :, col], jnp.full((nk, LANES), float(TOPK), F32)))
                    rk_scr[p, h, :, col] = rank
            return carry

        lax.fori_loop(0, nchunk, chunk, 0)

    ranked = first_level_quick()

    @pl.when(jnp.max(ranked) > float(TOPK))
    def _():
        first_level_exact()

    pairs = _second_level_pairs()

    def second_level(c, carry):
        col = pl.ds(pl.multiple_of(c * LANES, LANES), LANES)
        v1 = [sv_scr[0, a, :, col] for a in range(TOPK)]
        v2 = [sv_scr[1, b, :, col] for b in range(TOPK)]
        top = v1[0] + v2[0]

        def select(_, carry):
            cand, cnt, zsum = list(carry[0]), list(carry[1]), carry[2]
            m = _tree(jnp.maximum, cand)
            zsum = zsum + jnp.exp(m - top)
            found = jnp.zeros((nh, LANES), F32)
            for ci, (a, _b) in enumerate(pairs):
                eq = jnp.where(cand[ci] == m, 1.0, 0.0)
                hit = eq * (1.0 - found)
                found = jnp.maximum(found, eq)
                cand[ci] = jnp.where(hit > 0.0, neg, cand[ci])
                cnt[a] = cnt[a] + hit
            return tuple(cand), tuple(cnt), zsum

        def select_quick(_, carry):
            cand, cnt, zsum, left, bad = list(carry[0]), list(carry[1]), carry[2], carry[3], carry[4]
            m = _tree(jnp.maximum, cand)
            eq = [jnp.where(cv == m, 1.0, 0.0) for cv in cand]
            n_eq = _tree(jnp.add, eq)
            act = jnp.where(left > 0.0, 1.0, 0.0)
            bad = jnp.maximum(bad, jnp.where(n_eq > left, act, 0.0))
            zsum = zsum + act * n_eq * jnp.exp(m - top)
            for ci, (a, _b) in enumerate(pairs):
                cnt[a] = cnt[a] + eq[ci] * act
                cand[ci] = jnp.where(cand[ci] == m, neg, cand[ci])
            return tuple(cand), tuple(cnt), zsum, left - act * n_eq, bad

        zero = jnp.zeros((nh, LANES), F32)
        cands = tuple(v1[a] + v2[b] for a, b in pairs)

        def finish(cnt, zsum):
            for a in range(TOPK):
                cnt_scr[a, :, col] = cnt[a]
            cnt_scr[TOPK, :, col] = 1.0 / zsum

        _, cnt, zsum, _, bad = lax.fori_loop(0, TOPK, select_quick,
                                             (cands, (zero,) * TOPK, zero, zero + float(TOPK), zero))
        finish(cnt, zsum)

        @pl.when(jnp.max(bad) > 0.0)
        def _():
            _, cnt_x, zsum_x = lax.fori_loop(0, TOPK, select, (cands, (zero,) * TOPK, zero))
            finish(cnt_x, zsum_x)

        return carry

    lax.fori_loop(0, nchunk, second_level, 0)

    for h in range(nh):
        rank1 = rk_scr[0, h]
        b1 = jnp.zeros((nk, tt), F32)
        for a in range(TOPK):
            b1 = b1 + jnp.where(rank1 == float(a), cnt_scr[a, h:h + 1, :], 0.0)
        b1_ref[h] = b1
        r2_ref[h] = rk_scr[1, h].astype(BF16)
        e1_ref[h] = jnp.exp(s_scr[0, h] - sv_scr[0, 0, h:h + 1, :]) * cnt_scr[TOPK, h:h + 1, :]
        e2_ref[h] = jnp.exp(s_scr[1, h] - sv_scr[1, 0, h:h + 1, :]).astype(BF16)


def _route(h2, wq, keys, tt):
    t, d = h2.shape
    _, nh, nk, _ = keys.shape
    out = jax.ShapeDtypeStruct((nh, nk, t), F32)
    out16 = jax.ShapeDtypeStruct((nh, nk, t), BF16)
    ospec = pl.BlockSpec((nh, nk, tt), lambda i: (0, 0, i))
    return pl.pallas_call(
        _route_kernel,
        grid=(t // tt,),
        in_specs=[pl.BlockSpec((tt, d), lambda i: (i, 0)), _full(wq.shape), _full(keys.shape)],
        out_specs=[ospec, ospec, ospec, ospec],
        out_shape=[out16, out, out, out16],
        scratch_shapes=[pltpu.VMEM((2, TOPK, nh, tt), F32),
                        pltpu.VMEM((2, nh, nk, tt), F32),
                        pltpu.VMEM((2, nh, nk, tt), F32),
                        pltpu.VMEM((TOPK + 1, nh, tt), F32)],
        compiler_params=_params("arbitrary"),
        name="route",
    )(h2, wq, keys)


def _peer_kernel(n_ctx_tiles, h2_ref, u_ref, v_ref, r2_ref, e2_ref, b1_ref, e1_ref, x1_ref, mod_ref, fn_ref,
                 op_ref, os_ref, acc_scr, a_scr):
    d = x1_ref.shape[1]
    nh, nk, tt = r2_ref.shape
    n1 = b1_ref.shape[1]
    eb = pl.program_id(1)

    @pl.when(eb == 0)
    def _():
        acc_scr[...] = jnp.zeros_like(acc_scr)

    hid = lax.dot_general(u_ref[...].astype(BF16), h2_ref[...], (((1,), (1,)), ((), ())),
                          preferred_element_type=F32)
    pk = 2 * SUBLANES
    for i in range(n1):
        w = jnp.zeros((nk // pk, pk, tt), BF16)
        for h in range(nh):
            b1r = jnp.broadcast_to(b1_ref[h, i:i + 1, :], (pk, tt)).astype(BF16)[None]
            e1r = jnp.broadcast_to(e1_ref[h, i:i + 1, :], (pk, tt)).astype(BF16)[None]
            r2 = r2_ref[h].reshape(nk // pk, pk, tt)
            e2 = e2_ref[h].reshape(nk // pk, pk, tt)
            w = w + jnp.where(r2 < b1r, e1r * e2, jnp.zeros((), BF16))
        hi = hid[i * nk:(i + 1) * nk]
        act = 0.5 * hi * (1.0 + lax.erf(hi * (1.0 / math.sqrt(2.0))))
        a_scr[i * nk:(i + 1) * nk, :] = (act.astype(BF16).reshape(nk // pk, pk, tt) * w).reshape(nk, tt)
    acc_scr[...] += lax.dot_general(v_ref[...], a_scr[...], (((0,), (0,)), ((), ())),
                                    preferred_element_type=F32)

    @pl.when(eb == pl.num_programs(1) - 1)
    def _():
        gate2 = mod_ref[0][:, 5 * d:6 * d]
        xo = x1_ref[...] + gate2 * acc_scr[...].T
        res = xo * lax.rsqrt(jnp.mean(xo * xo, axis=-1, keepdims=True) + EPS) * fn_ref[...]
        is_ctx = pl.program_id(0) < n_ctx_tiles

        @pl.when(is_ctx)
        def _():
            op_ref[...] = res

        @pl.when(jnp.logical_not(is_ctx))
        def _():
            os_ref[...] = res


def _peer(h2, u, v, r2, e2, b1, e1, x1, mods, final_norm, n_ctx_tiles, tiles_per_lat, tt, n1):
    t, d = h2.shape
    ne = u.shape[0]
    nh, nk, _ = r2.shape
    eb = n1 * nk

    def mod_idx(i, e):
        return (jnp.where(i < n_ctx_tiles, 0, 1 + (i - n_ctx_tiles) // tiles_per_lat), 0, 0)

    dense = pl.BlockSpec((nh, nk, tt), lambda i, e: (0, 0, i))
    rows = pl.BlockSpec((nh, n1, tt), lambda i, e: (0, e, i))
    tc = n_ctx_tiles * tt
    return pl.pallas_call(
        functools.partial(_peer_kernel, n_ctx_tiles),
        grid=(t // tt, ne // eb),
        in_specs=[pl.BlockSpec((tt, d), lambda i, e: (i, 0)),
                  pl.BlockSpec((eb, d), lambda i, e: (e, 0)),
                  pl.BlockSpec((eb, d), lambda i, e: (e, 0)),
                  dense, dense, rows, rows,
                  pl.BlockSpec((tt, d), lambda i, e: (i, 0)),
                  pl.BlockSpec((1, 1, mods.shape[-1]), mod_idx),
                  _full(final_norm.shape)],
        out_specs=list(_two_part((tt, d), n_ctx_tiles)),
        out_shape=[jax.ShapeDtypeStruct((tc, d), F32), jax.ShapeDtypeStruct((t - tc, d), F32)],
        scratch_shapes=[pltpu.VMEM((d, tt), F32), pltpu.VMEM((eb, tt), BF16)],
        compiler_params=_params("arbitrary", "arbitrary"),
        name="peer",
    )(h2, u, v, r2, e2, b1, e1, x1, mods, final_norm)


def _hyena_features(l, bands):
    t = jnp.linspace(0.0, 1.0, l, dtype=F32)[:, None]
    wpos = 2.0 * math.pi * jnp.arange(l, dtype=F32)[:, None] / l
    f = jnp.linspace(1e-4, bands - 1, bands, dtype=F32)[None, :]
    return jnp.concatenate([t, jnp.cos(f * wpos), -jnp.sin(f * wpos)], axis=-1)


def kernel(x_prompt, x_sample, state_rwkv, c, c_ctx, w_ada, b_ada, norm_mix, norm_ffn, w_in, conv_w, w_out, rwkv_w0, rwkv_w1, rwkv_w2, rwkv_a0, rwkv_a1, rwkv_a2, rwkv_g1, rwkv_g2, rwkv_k_k, rwkv_k_a, rwkv_r_k, rwkv_ln_w, rwkv_ln_b, hy_f_w1, hy_f_b1, hy_f_w2, hy_f_b2, hy_freq, hy_f_w3, hy_bias, hy_norm, peer_wq, peer_keys, peer_u, peer_v, final_norm):
    bc, lc, d = x_prompt.shape
    bl, ll, _ = x_sample.shape
    depth = w_ada.shape[0]
    assert depth == 1
    dr = rwkv_w0.shape[-1]
    dh = hy_norm.shape[-1]
    nh = dr // HEAD
    assert dr == dh and w_in.shape[-1] == 3 * dr + 3 * dh
    tc, tl = bc * lc, bl * ll
    t = tc + tl
    tm = lc
    assert ll % tm == 0 and tm % GRID_W == 0 and tc % ll == 0
    n_ctx_tiles = tc // tm
    tiles_per_lat = ll // tm
    parts = LANES // (bl * (nh // 2))
    assert parts * bl * (nh // 2) == LANES and HEAD % parts == 0 and tc % tl == 0
    nil = HEAD // parts
    l = 0

    rows = jnp.concatenate([c_ctx[None, :], c, jnp.zeros((SUBLANES - 1 - bl, d), F32)], axis=0)
    mods = _ada(rows, w_ada[l], b_ada[l]).reshape(SUBLANES, 1, 6 * d)

    xp = x_prompt.reshape(tc, d)
    xs = x_sample.reshape(tl, d)

    def pad_cols(w):
        return jnp.pad(w, ((0, 0), (0, LANES - w.shape[1])))

    def pad_rows(w):
        return jnp.pad(w, ((0, LANES - w.shape[0]), (0, 0)))

    lora1 = jnp.concatenate([pad_cols(rwkv_w1[l, 0]), pad_cols(rwkv_w1[l, 1]), pad_cols(rwkv_a1[l, 0]),
                             pad_cols(rwkv_a1[l, 1]), pad_cols(rwkv_g1[l])], axis=1).astype(BF16)
    w2 = jnp.stack([pad_rows(rwkv_w2[l, 0]), pad_rows(rwkv_w2[l, 1])])
    a2 = jnp.stack([pad_rows(rwkv_a2[l, 0]), pad_rows(rwkv_a2[l, 1])])
    vecs_pre = jnp.stack([rwkv_w0[l, 0], rwkv_w0[l, 1], rwkv_a0[l, 0], rwkv_a0[l, 1], rwkv_k_k[l], rwkv_k_a[l],
                          rwkv_r_k[l, 0].reshape(dr), rwkv_r_k[l, 1].reshape(dr)])
    head_id = jnp.arange(dr, dtype=jnp.int32) // HEAD
    hs = (head_id[:, None] == head_id[None, :]).astype(BF16)
    shared, perdir, gb, hy = _mix_pre(xp, xs, mods, n_ctx_tiles, tiles_per_lat, tm, norm_mix[l][None, :],
                                      w_in[l].astype(BF16), conv_w[l], lora1, w2, a2, rwkv_g2[l], vecs_pre, hs)

    shared4 = shared.reshape(3, t // lc, lc, dr)
    perdir4 = perdir.reshape(2, 3, t // lc, lc, dr)
    y_cf, s_cf = _wkv_ctx(shared4, perdir4, bc, 8, False)
    y_cb, s_cb = _wkv_ctx(shared4, perdir4, bc, 8, True)
    shared_l = shared.reshape(3, t // ll, ll, dr)
    perdir_l = perdir.reshape(2, 3, t // ll, ll, dr)
    s0 = state_rwkv[:, l].reshape(bl, 2, nh // 2, 2, nil, parts, HEAD).transpose(1, 3, 4, 6, 2, 0, 5)
    s0 = s0.reshape(2, 2 * nil, HEAD, LANES)
    y_lf, y_lb = _wkv_lat(shared_l, perdir_l, s0, tc // tl, bl, 16)
    new_state = jnp.stack([s_cf, s_cb]).reshape(2, 2, HEAD, HEAD, nh // 2, bc).transpose(5, 0, 4, 1, 2, 3)
    new_state = new_state.reshape(bc, 1, 2, nh, HEAD, HEAD)

    bands = (hy_f_w1.shape[1] - 1) // 2
    femb = hy_f_w1.shape[1]
    fpad = 32
    w1p = jnp.pad(hy_f_w1[l], ((0, fpad - femb), (0, 0)))
    hidden = hy_f_w2.shape[-1]
    w3 = hy_f_w3[l].reshape(hidden, 4, dh).transpose(1, 0, 2)
    max_decay = math.log(1e-2) / 0.3
    min_decay = math.log(1e-2) / 1.5
    deltas = jnp.linspace(min_decay, max_decay, dh, dtype=F32)[None, :]
    z_parts = []
    for seq_len, first_seq, n_seq in ((lc, 0, bc), (ll, tc // ll, bl)):
        mats = _dft_mats(seq_len)
        zf = jnp.pad(_hyena_features(seq_len, bands), ((0, 0), (0, fpad - femb)))
        g = _hy_filter(zf, w1p, hy_f_b1[l][None, :], hy_f_w2[l], hy_f_b2[l][None, :], hy_freq[l], w3,
                       deltas, hy_bias[l], mats)
        z_parts.append(_hy_conv(hy, g, mats, seq_len, first_seq, n_seq))

    vecs_post = jnp.concatenate([jnp.stack([rwkv_ln_w[l], rwkv_ln_b[l], hy_norm[l]]),
                                 jnp.zeros((SUBLANES - 3, dr), F32)], axis=0)
    x1, h2 = _mix_post((y_cf.reshape(tc, dr), y_lf.reshape(tl, dr)), (y_cb.reshape(tc, dr), y_lb.reshape(tl, dr)),
                       gb, z_parts, (xp, xs), mods, n_ctx_tiles, tiles_per_lat, tm, vecs_post,
                       w_out[l].astype(BF16), norm_ffn[l][None, :], hs)

    r2, b1, e1, e2 = _route(h2, peer_wq[l].astype(BF16), peer_keys[l], tt=512)
    tt = 512
    assert ll % tt == 0 and tc % tt == 0
    y_p, y_s = _peer(h2, peer_u[l], peer_v[l].astype(BF16), r2, e2, b1, e1, x1, mods,
                final_norm[None, :], tc // tt, ll // tt, tt, n1=SUBLANES)
    return y_p.reshape(bc, lc, d), y_s.reshape(bl, ll, d), new_state
```

```python
import functools
import math

import jax
import jax.numpy as jnp
import numpy as np
from jax import lax
from jax.experimental import pallas as pl
from jax.experimental.pallas import tpu as pltpu

F32 = jnp.float32
BF16 = jnp.bfloat16

GRID_W = 64
HEAD = 64
TOPK = 16
EPS = 1e-6
GN_EPS = 64e-5
LANES = 128
SUBLANES = 8
VMEM_LIMIT = 56 * 1024 * 1024


def _params(*sem):
    return pltpu.CompilerParams(dimension_semantics=sem, vmem_limit_bytes=VMEM_LIMIT)


def _split(x):
    hi = x.astype(BF16)
    lo = (x - hi.astype(F32)).astype(BF16)
    return hi, lo


def _dot(a, b):
    return jnp.dot(a, b, preferred_element_type=F32)


def _dot_split(ah, al, bh, bl):
    return _dot(ah, bh) + (_dot(al, bh) + _dot(ah, bl))


def _dot3(a, b):
    ah, al = _split(a)
    bh, bl = _split(b)
    return _dot_split(ah, al, bh, bl)


def _full(shape):
    n = len(shape)
    return pl.BlockSpec(shape, lambda *_: (0,) * n)


def _two_part(block, n_first):
    return (pl.BlockSpec(block, lambda i, *_: (jnp.minimum(i, n_first - 1), 0)),
            pl.BlockSpec(block, lambda i, *_: (jnp.maximum(i - n_first, 0), 0)))


def _const(shape):
    n = len(shape)
    return pl.BlockSpec(shape, lambda *_: (0,) * n, pipeline_mode=pl.Buffered(1))


def _ada_kernel(c_ref, w_ref, b_ref, o_ref):
    c = c_ref[...]
    s = c * jax.nn.sigmoid(c)
    o_ref[...] = _dot3(s, w_ref[...]) + b_ref[...]


def _ada(c_rows, w, b):
    rows, d = c_rows.shape
    n = w.shape[1]
    bn = n // 4
    return pl.pallas_call(
        _ada_kernel,
        grid=(4,),
        in_specs=[_full((rows, d)),
                  pl.BlockSpec((d, bn), lambda j: (0, j)),
                  pl.BlockSpec((1, bn), lambda j: (0, j))],
        out_specs=pl.BlockSpec((rows, bn), lambda j: (0, j)),
        out_shape=jax.ShapeDtypeStruct((rows, n), F32),
        compiler_params=_params("arbitrary"),
        name="ada",
    )(c_rows, w, b.reshape(1, n))


def _headsum(z, hs):
    zh, zl = _split(z)
    return _dot(zh, hs) + _dot(zl, hs)


def _mix_pre_kernel(n_ctx_tiles, xp_ref, xs_ref, mod_ref, nm_ref, win_ref, cw_ref, l1_ref, w2_ref, a2_ref,
                    g2_ref, vec_ref, hs_ref, sh_ref, dr_ref, gb_ref, hy_ref):
    tm, d = xp_ref.shape
    dr = hs_ref.shape[0]
    is_ctx = pl.program_id(0) < n_ctx_tiles
    x = jnp.where(is_ctx, xp_ref[...], xs_ref[...])
    mod = mod_ref[0]
    shift1 = mod[:, 0:d]
    scale1 = mod[:, d:2 * d]
    xn = x * lax.rsqrt(jnp.mean(x * x, axis=-1, keepdims=True) + EPS) * nm_ref[...]
    hb = (xn * (1.0 + scale1) + shift1).astype(BF16)

    pos = lax.broadcasted_iota(jnp.int32, (tm, dr), 0)
    row = pos % GRID_W
    keep_prev = jnp.where(is_ctx, jnp.where(pos == 0, 0.0, 1.0), jnp.where(row == 0, 0.0, 1.0))
    keep_next = jnp.where(is_ctx, jnp.where(pos == tm - 1, 0.0, 1.0),
                          jnp.where(row == GRID_W - 1, 0.0, 1.0))

    def proj(c):
        u = _dot(hb, win_ref[:, c * dr:(c + 1) * dr])
        cw = cw_ref[:, c * dr:(c + 1) * dr]
        up = pltpu.roll(u, 1, 0) * keep_prev
        un = pltpu.roll(u, tm - 1, 0) * keep_next
        return up * cw[0:1] + u * cw[1:2] + un * cw[2:3]

    r = proj(0)
    k = proj(1)
    v = proj(2)
    hy_ref[0] = proj(3)
    hy_ref[1] = proj(4)
    hy_ref[2] = proj(5)

    lo = _dot(hb, l1_ref[...])
    vec = vec_ref[...]
    k_k = vec[4:5]
    k_a = vec[5:6]
    hs = hs_ref[...]
    kkr = k * k_k
    kk = kkr * lax.rsqrt(jnp.maximum(_headsum(kkr * kkr, hs), 1e-24))
    sh_ref[0] = r
    sh_ref[1] = kk
    sh_ref[2] = v
    gb_ref[0] = _dot3(jax.nn.sigmoid(lo[:, 4 * LANES:5 * LANES]), g2_ref[...])
    bon = jnp.zeros_like(r)
    for dd in range(2):
        w = vec[dd:dd + 1] + _dot3(jnp.tanh(lo[:, dd * LANES:(dd + 1) * LANES]), w2_ref[dd])
        z = -w
        softplus = jnp.maximum(z, 0.0) + jnp.log1p(jnp.exp(-jnp.abs(z)))
        dr_ref[dd, 0] = jnp.exp(-jnp.exp(-softplus - 0.5))
        a = jax.nn.sigmoid(vec[2 + dd:3 + dd]
                           + _dot3(lo[:, (2 + dd) * LANES:(3 + dd) * LANES], a2_ref[dd]))
        kd = k * (1.0 + (a - 1.0) * k_a)
        dr_ref[dd, 1] = kd
        dr_ref[dd, 2] = kk * a
        bon = bon + r * kd * vec[6 + dd:7 + dd]
    gb_ref[1] = _headsum(bon, hs) * v


def _mix_pre(xp, xs, mods, n_ctx_tiles, tiles_per_lat, tm, norm_mix, w_in, conv_w, lora1, w2, a2, g2, vecs, hs):
    d = xp.shape[1]
    t = xp.shape[0] + xs.shape[0]
    dr = hs.shape[0]
    n_tiles = t // tm

    def mod_idx(i):
        return (jnp.where(i < n_ctx_tiles, 0, 1 + (i - n_ctx_tiles) // tiles_per_lat), 0, 0)

    return pl.pallas_call(
        functools.partial(_mix_pre_kernel, n_ctx_tiles),
        grid=(n_tiles,),
        in_specs=[*_two_part((tm, d), n_ctx_tiles),
                  pl.BlockSpec((1, 1, mods.shape[-1]), mod_idx),
                  _full(norm_mix.shape), _full(w_in.shape), _full(conv_w.shape), _full(lora1.shape),
                  _full(w2.shape), _full(a2.shape), _full(g2.shape), _full(vecs.shape), _full(hs.shape)],
        out_specs=[pl.BlockSpec((3, tm, dr), lambda i: (0, i, 0)),
                   pl.BlockSpec((2, 3, tm, dr), lambda i: (0, 0, i, 0)),
                   pl.BlockSpec((2, tm, dr), lambda i: (0, i, 0)),
                   pl.BlockSpec((3, tm, dr), lambda i: (0, i, 0))],
        out_shape=[jax.ShapeDtypeStruct((3, t, dr), F32),
                   jax.ShapeDtypeStruct((2, 3, t, dr), F32),
                   jax.ShapeDtypeStruct((2, t, dr), F32),
                   jax.ShapeDtypeStruct((3, t, dr), F32)],
        compiler_params=_params("arbitrary"),
        name="mix_pre",
    )(xp, xs, mods, norm_mix, w_in, conv_w, lora1, w2, a2, g2, vecs, hs)


def _wkv_ctx_kernel(reverse, sh_ref, pd_ref, y_ref, sfin_ref, s_scr, tr_scr, c_scr, yt_scr):
    nb, tb = sh_ref.shape[1], sh_ref.shape[2]
    nhp = sh_ref.shape[3] // LANES
    tblk = pl.program_id(0)

    @pl.when(tblk == 0)
    def _():
        s_scr[...] = jnp.zeros_like(s_scr)

    def to_lanes(ref, lead, t):
        blocks = [ref[lead + (slice(None), t, slice(hp * LANES, (hp + 1) * LANES))] for hp in range(nhp)]
        return jnp.concatenate(blocks, axis=0).T

    def relayout(t, slot):
        r_t = to_lanes(sh_ref, (0,), t)
        d_t = to_lanes(pd_ref, (0, 0), t)
        k_t = to_lanes(pd_ref, (0, 1), t)
        kka_t = to_lanes(pd_ref, (0, 2), t)
        tr_scr[slot, 0] = d_t * r_t
        tr_scr[slot, 1] = to_lanes(sh_ref, (1,), t)
        tr_scr[slot, 2] = to_lanes(sh_ref, (2,), t)
        tr_scr[slot, 3] = d_t
        tr_scr[slot, 4] = k_t
        tr_scr[slot, 5] = kka_t
        for h2 in range(2):
            rows = slice(h2 * HEAD, (h2 + 1) * HEAD)
            c_scr[slot, 2 * h2] = jnp.broadcast_to(jnp.sum(k_t[rows] * r_t[rows], axis=0, keepdims=True),
                                                   (SUBLANES, LANES))
            c_scr[slot, 2 * h2 + 1] = jnp.broadcast_to(jnp.sum(kka_t[rows] * r_t[rows], axis=0, keepdims=True),
                                                       (SUBLANES, LANES))

    def time_of(tau):
        return tb - 1 - tau if reverse else tau

    relayout(time_of(0), 0)
    for tau in range(tb):
        slot = tau % 2
        if tau + 1 < tb:
            relayout(time_of(tau + 1), 1 - slot)
        for h2 in range(2):
            rows = slice(h2 * HEAD, (h2 + 1) * HEAD)
            c1 = c_scr[slot, 2 * h2, 0:1, :]
            c2 = c_scr[slot, 2 * h2 + 1, 0:1, :]
            for i8 in range(HEAD // SUBLANES):
                base = h2 * HEAD + i8 * SUBLANES
                vblk = tr_scr[slot, 2, base:base + SUBLANES, :]
                outs = []
                for jj in range(SUBLANES):
                    s = s_scr[base + jj]
                    vi = vblk[jj:jj + 1]
                    sa = jnp.sum(s * tr_scr[slot, 1, rows, :], axis=0, keepdims=True)
                    q = jnp.sum(s * tr_scr[slot, 0, rows, :], axis=0, keepdims=True)
                    s_scr[base + jj] = (s * tr_scr[slot, 3, rows, :]
                                        + (vi * tr_scr[slot, 4, rows, :] - sa * tr_scr[slot, 5, rows, :]))
                    outs.append(q + vi * c1 - sa * c2)
                yt_scr[base:base + SUBLANES, :] = jnp.concatenate(outs, axis=0)
        y = yt_scr[...].T
        t = time_of(tau)
        for hp in range(nhp):
            y_ref[:, t, hp * LANES:(hp + 1) * LANES] = y[hp * nb:(hp + 1) * nb]

    @pl.when(tblk == pl.num_programs(0) - 1)
    def _():
        sfin_ref[...] = s_scr[...]


def _wkv_ctx(shared, perdir, nb, tb, reverse):
    _, nseq, lc, dr = shared.shape
    assert nb * (dr // LANES) == LANES and 2 * HEAD == LANES
    nt = lc // tb
    di = 1 if reverse else 0

    def tsel(ti):
        return nt - 1 - ti if reverse else ti

    return pl.pallas_call(
        functools.partial(_wkv_ctx_kernel, reverse),
        grid=(nt,),
        in_specs=[pl.BlockSpec((3, nb, tb, dr), lambda ti: (0, 0, tsel(ti), 0)),
                  pl.BlockSpec((1, 3, nb, tb, dr), lambda ti: (di, 0, 0, tsel(ti), 0))],
        out_specs=[pl.BlockSpec((nb, tb, dr), lambda ti: (0, tsel(ti), 0)),
                   pl.BlockSpec((LANES, HEAD, LANES), lambda ti: (0, 0, 0))],
        out_shape=[jax.ShapeDtypeStruct((nb, lc, dr), F32),
                   jax.ShapeDtypeStruct((LANES, HEAD, LANES), F32)],
        scratch_shapes=[pltpu.VMEM((LANES, HEAD, LANES), F32),
                        pltpu.VMEM((2, 6, LANES, LANES), F32),
                        pltpu.VMEM((2, 4, SUBLANES, LANES), F32),
                        pltpu.VMEM((LANES, LANES), F32)],
        compiler_params=_params("arbitrary"),
        name="wkv_ctx",
    )(shared, perdir)


def _wkv_lat_kernel(shf_ref, shb_ref, pdf_ref, pdb_ref, s0_ref, yf_ref, yb_ref, s_scr):
    nb, tb, dr = shf_ref.shape[1], shf_ref.shape[2], shf_ref.shape[3]
    nhp = dr // LANES
    parts = LANES // (nhp * nb)
    nil = HEAD // parts
    tblk = pl.program_id(0)
    lane_part = lax.broadcasted_iota(jnp.int32, (parts, LANES), 1) % parts
    diag = jnp.where(lane_part == lax.broadcasted_iota(jnp.int32, (parts, LANES), 0), 1.0, 0.0)

    @pl.when(tblk == 0)
    def _():
        s_scr[...] = s0_ref[...]

    def to_lanes(ref, lead, t):
        blocks = [jnp.broadcast_to(ref[lead + (b, slice(t, t + 1), slice(hp * LANES, (hp + 1) * LANES))],
                                   (parts, LANES))
                  for hp in range(nhp) for b in range(nb)]
        return jnp.concatenate(blocks, axis=0).T

    def fetch(sh_ref, pd_ref, t):
        return (to_lanes(sh_ref, (0,), t), to_lanes(sh_ref, (1,), t), to_lanes(sh_ref, (2,), t),
                to_lanes(pd_ref, (0, 0), t), to_lanes(pd_ref, (0, 1), t), to_lanes(pd_ref, (0, 2), t))

    def step(di, fetched, y_ref, t):
        r_t, kk_t, v_t, d_t, k_t, kka_t = fetched
        placed = []
        for h2 in range(2):
            rows = slice(h2 * HEAD, (h2 + 1) * HEAD)
            rt, kkt, dt, kt, kkat = r_t[rows], kk_t[rows], d_t[rows], k_t[rows], kka_t[rows]
            drt = dt * rt
            c1 = jnp.sum(kt * rt, axis=0, keepdims=True)
            c2 = jnp.sum(kkat * rt, axis=0, keepdims=True)
            for il in range(nil):
                r0 = h2 * HEAD + il * parts
                vi = jnp.sum(v_t[r0:r0 + parts] * diag, axis=0, keepdims=True)
                s = s_scr[di, h2 * nil + il]
                sa = jnp.sum(s * kkt, axis=0, keepdims=True)
                q = jnp.sum(s * drt, axis=0, keepdims=True)
                s_scr[di, h2 * nil + il] = s * dt + (vi * kt - sa * kkat)
                placed.append((q + vi * c1 - sa * c2) * diag)
        y = jnp.concatenate(placed, axis=0).T
        for hp in range(nhp):
            for b in range(nb):
                r0 = (hp * nb + b) * parts
                y_ref[b, t:t + 1, hp * LANES:(hp + 1) * LANES] = jnp.sum(y[r0:r0 + parts], axis=0, keepdims=True)

    ahead = (fetch(shf_ref, pdf_ref, 0), fetch(shb_ref, pdb_ref, tb - 1))
    for tau in range(tb):
        now = ahead
        if tau + 1 < tb:
            ahead = (fetch(shf_ref, pdf_ref, tau + 1), fetch(shb_ref, pdb_ref, tb - 2 - tau))
        step(0, now[0], yf_ref, tau)
        step(1, now[1], yb_ref, tb - 1 - tau)


def _wkv_lat(shared, perdir, s0, first_blk, nb, tb):
    _, nseq, ll, dr = shared.shape
    nt = ll // tb
    out = jax.ShapeDtypeStruct((nb, ll, dr), F32)
    return pl.pallas_call(
        _wkv_lat_kernel,
        grid=(nt,),
        in_specs=[pl.BlockSpec((3, nb, tb, dr), lambda ti: (0, first_blk, ti, 0)),
                  pl.BlockSpec((3, nb, tb, dr), lambda ti: (0, first_blk, nt - 1 - ti, 0)),
                  pl.BlockSpec((1, 3, nb, tb, dr), lambda ti: (0, 0, first_blk, ti, 0)),
                  pl.BlockSpec((1, 3, nb, tb, dr), lambda ti: (1, 0, first_blk, nt - 1 - ti, 0)),
                  _full(s0.shape)],
        out_specs=[pl.BlockSpec((nb, tb, dr), lambda ti: (0, ti, 0)),
                   pl.BlockSpec((nb, tb, dr), lambda ti: (0, nt - 1 - ti, 0))],
        out_shape=[out, out],
        scratch_shapes=[pltpu.VMEM(s0.shape, F32)],
        compiler_params=_params("arbitrary"),
        name="wkv_lat",
    )(shared, shared, perdir, perdir, s0)


def _dft_mats(l):
    n = 2 * l
    nfp = l + SUBLANES
    kf = np.arange(nfp, dtype=np.int64)
    ang = (2.0 * np.pi / n) * ((kf[:, None] * np.arange(l, dtype=np.int64)[None, :]) % n)
    valid = (kf <= l)[:, None]
    c = np.where(valid, np.cos(ang), 0.0)
    s = np.where(valid, np.sin(ang), 0.0)
    wk = np.where((kf == 0) | (kf == l), 1.0 / n, 2.0 / n)[None, :]
    mats = (c, s, c.T * wk, -(s.T) * wk)
    return [m for mat in mats for m in _split(jnp.asarray(mat.astype(np.float32)))]


def _hy_filter_kernel(zf_ref, w1_ref, b1_ref, w2_ref, b2_ref, fq_ref, w3_ref, dl_ref, bias_ref,
                      ch_ref, cl_ref, sh_ref, sl_ref, g_ref):
    zf = zf_ref[...]
    hid = jnp.sin(fq_ref[0:1] * (_dot3(zf, w1_ref[...]) + b1_ref[...]))
    hid = jnp.sin(fq_ref[1:2] * (_dot3(hid, w2_ref[...]) + b2_ref[...]))
    win = jnp.exp(-zf[:, 0:1] * jnp.abs(dl_ref[...]))
    hh, hl = _split(hid)
    filt = []
    for o in range(4):
        wh, wl = _split(w3_ref[o])
        filt.append(_dot_split(hh, hl, wh, wl) * win)
    for od in range(2):
        hf, hbk = filt[2 * od], filt[2 * od + 1]
        ph, plo = _split(hf + hbk)
        mh, ml = _split(hbk - hf)
        g_ref[od, 0] = _dot_split(ch_ref[...], cl_ref[...], ph, plo) + bias_ref[od:od + 1]
        g_ref[od, 1] = _dot_split(sh_ref[...], sl_ref[...], mh, ml)


def _hy_filter(zf, w1, b1, w2, b2, fq, w3, deltas, bias, mats):
    l = zf.shape[0]
    nfp = mats[0].shape[0]
    c = w3.shape[-1]
    cb = LANES
    return pl.pallas_call(
        _hy_filter_kernel,
        grid=(c // cb,),
        in_specs=[_full(zf.shape), _full(w1.shape), _full(b1.shape), _full(w2.shape), _full(b2.shape),
                  _full(fq.shape),
                  pl.BlockSpec((4, w3.shape[1], cb), lambda j: (0, 0, j)),
                  pl.BlockSpec((1, cb), lambda j: (0, j)),
                  pl.BlockSpec((2, cb), lambda j: (0, j)),
                  _const((nfp, l)), _const((nfp, l)), _const((nfp, l)), _const((nfp, l))],
        out_specs=pl.BlockSpec((2, 2, nfp, cb), lambda j: (0, 0, 0, j)),
        out_shape=jax.ShapeDtypeStruct((2, 2, nfp, c), F32),
        compiler_params=_params("arbitrary"),
        name="hy_filt",
    )(zf, w1, b1, w2, b2, fq, w3, deltas, bias, *mats[:4])


def _hy_conv_kernel(hy_ref, g_ref, ch_ref, cl_ref, sh_ref, sl_ref, cih_ref, cil_ref, sih_ref, sil_ref, o_ref):
    def conv(u, od):
        uh, ul = _split(u)
        re = _dot_split(ch_ref[...], cl_ref[...], uh, ul)
        im = -_dot_split(sh_ref[...], sl_ref[...], uh, ul)
        gre = g_ref[od, 0]
        gim = g_ref[od, 1]
        ah, al = _split(re * gre - im * gim)
        bh, bl = _split(re * gim + im * gre)
        return (_dot_split(cih_ref[...], cil_ref[...], ah, al)
                + _dot_split(sih_ref[...], sil_ref[...], bh, bl))

    z = hy_ref[1] * conv(hy_ref[0], 0)
    o_ref[...] = hy_ref[2] * conv(z, 1)


def _hy_conv(hy, g, mats, l, first_seq, n_seq):
    t, c = hy.shape[1], hy.shape[2]
    nfp = mats[0].shape[0]
    cb = c if l <= 2 * LANES else 2 * LANES
    return pl.pallas_call(
        _hy_conv_kernel,
        grid=(n_seq, c // cb),
        in_specs=[pl.BlockSpec((3, l, cb), lambda b, j: (0, first_seq + b, j)),
                  pl.BlockSpec((2, 2, nfp, cb), lambda b, j: (0, 0, 0, j)),
                  _const((nfp, l)), _const((nfp, l)), _const((nfp, l)), _const((nfp, l)),
                  _const((l, nfp)), _const((l, nfp)), _const((l, nfp)), _const((l, nfp))],
        out_specs=pl.BlockSpec((l, cb), lambda b, j: (b, j)),
        out_shape=jax.ShapeDtypeStruct((n_seq * l, c), F32),
        compiler_params=_params("arbitrary", "arbitrary"),
        name="hy_conv",
    )(hy, g, *mats)


def _mix_post_kernel(n_ctx_tiles, ycf_ref, ylf_ref, ycb_ref, ylb_ref, gb_ref, zc_ref, zl_ref, xp_ref, xs_ref,
                     mod_ref, vec_ref, wout_ref, nf_ref, hs_ref, x1_ref, h2_ref):
    d = xp_ref.shape[1]
    dr = hs_ref.shape[0]
    is_ctx = pl.program_id(0) < n_ctx_tiles
    hs = hs_ref[...]
    vec = vec_ref[...]
    mod = mod_ref[0]
    gate1 = mod[:, 2 * d:3 * d]
    shift2 = mod[:, 3 * d:4 * d]
    scale2 = mod[:, 4 * d:5 * d]
    y = jnp.where(is_ctx, ycf_ref[...] + ycb_ref[...], ylf_ref[...] + ylb_ref[...])
    mu = _headsum(y, hs) * (1.0 / HEAD)
    yc = y - mu
    var = _headsum(yc * yc, hs) * (1.0 / HEAD)
    yn = yc * lax.rsqrt(var + GN_EPS) * vec[0:1] + vec[1:2]
    ya = (yn + gb_ref[1]) * gb_ref[0]
    z = jnp.where(is_ctx, zc_ref[...], zl_ref[...])
    yb = z * lax.rsqrt(jnp.mean(z * z, axis=-1, keepdims=True) + EPS) * vec[2:3]
    m = _dot(ya.astype(BF16), wout_ref[0:dr]) + _dot(yb.astype(BF16), wout_ref[dr:])
    x1 = jnp.where(is_ctx, xp_ref[...], xs_ref[...]) + gate1 * m
    x1_ref[...] = x1
    xn = x1 * lax.rsqrt(jnp.mean(x1 * x1, axis=-1, keepdims=True) + EPS) * nf_ref[...]
    h2_ref[...] = (xn * (1.0 + scale2) + shift2).astype(BF16)


def _mix_post(y_fwd, y_bwd, gb, z, x, mods, n_ctx_tiles, tiles_per_lat, tm, vecs, w_out, norm_ffn, hs):
    d = x[0].shape[1]
    t = x[0].shape[0] + x[1].shape[0]
    dr = hs.shape[0]

    def mod_idx(i):
        return (jnp.where(i < n_ctx_tiles, 0, 1 + (i - n_ctx_tiles) // tiles_per_lat), 0, 0)

    tok = _two_part((tm, dr), n_ctx_tiles)
    tokd = pl.BlockSpec((tm, d), lambda i: (i, 0))
    return pl.pallas_call(
        functools.partial(_mix_post_kernel, n_ctx_tiles),
        grid=(t // tm,),
        in_specs=[*tok, *tok, pl.BlockSpec((2, tm, dr), lambda i: (0, i, 0)), *tok,
                  *_two_part((tm, d), n_ctx_tiles),
                  pl.BlockSpec((1, 1, mods.shape[-1]), mod_idx),
                  _full(vecs.shape), _full(w_out.shape), _full(norm_ffn.shape), _full(hs.shape)],
        out_specs=[tokd, tokd],
        out_shape=[jax.ShapeDtypeStruct((t, d), F32), jax.ShapeDtypeStruct((t, d), BF16)],
        compiler_params=_params("arbitrary"),
        name="mix_post",
    )(*y_fwd, *y_bwd, gb, *z, *x, mods, vecs, w_out, norm_ffn, hs)


def _tree(op, xs):
    xs = list(xs)
    while len(xs) > 1:
        xs = [op(xs[i], xs[i + 1]) for i in range(0, len(xs) - 1, 2)] + ([xs[-1]] if len(xs) % 2 else [])
    return xs[0]


def _second_level_pairs():
    return [(a, b) for a in range(TOPK) for b in range(TOPK) if (a + 1) * (b + 1) <= TOPK]


def _route_kernel(h2_ref, wq_ref, keys_ref, r2_ref, b1_ref, e1_ref, e2_ref, sv_scr, s_scr, rk_scr, cnt_scr):
    nh = keys_ref.shape[1]
    nk = keys_ref.shape[2]
    half = keys_ref.shape[3]
    tt = h2_ref.shape[0]
    q = _dot(h2_ref[...], wq_ref[...])
    key_id = lax.broadcasted_iota(jnp.int32, (nk, LANES), 0).astype(F32)
    neg = jnp.float32(-jnp.inf)

    for h in range(nh):
        for p in range(2):
            c0 = (h * 2 + p) * half
            qh, ql = _split(q[:, c0:c0 + half])
            kh, kl = _split(keys_ref[p, h])
            dims = (((1,), (1,)), ((), ()))
            s = (lax.dot_general(kh, qh, dims, preferred_element_type=F32)
                 + (lax.dot_general(kl, qh, dims, preferred_element_type=F32)
                    + lax.dot_general(kh, ql, dims, preferred_element_type=F32)))
            s_scr[p, h] = s

    nchunk = tt // LANES

    def first_level_quick():
        def chunk(c, ranked):
            col = pl.ds(pl.multiple_of(c * LANES, LANES), LANES)
            for h in range(nh):
                def extract(r, carry, h=h):
                    out = []
                    for p in range(2):
                        m_prev, above = carry[p]
                        s = s_scr[p, h, :, col]
                        below = s < m_prev
                        m = jnp.max(jnp.where(below, s, neg), axis=0, keepdims=True)
                        sv_scr[p, r, h:h + 1, col] = m
                        out.append((m, above + jnp.where(below, 1.0, 0.0)))
                    return tuple(out)

                start = (jnp.full((1, LANES), jnp.inf, F32), jnp.zeros((nk, LANES), F32))
                done = lax.fori_loop(0, TOPK, extract, (start, start))
                for p in range(2):
                    m_last, above = done[p]
                    rank = above - 1.0 + jnp.where(s_scr[p, h, :, col] < m_last, 1.0, 0.0)
                    rk_scr[p, h, :, col] = rank
                    ranked = jnp.maximum(ranked, jnp.sum(jnp.where(rank < float(TOPK), 1.0, 0.0),
                                                         axis=0, keepdims=True))
            return ranked

        return lax.fori_loop(0, nchunk, chunk, jnp.zeros((1, LANES), F32))

    def first_level_exact():
        def chunk(c, carry):
            col = pl.ds(pl.multiple_of(c * LANES, LANES), LANES)
            for h in range(nh):
                for p in range(2):
                    def extract(r, xr, p=p, h=h):
                        x, rank = xr
                        m = jnp.max(x, axis=0, keepdims=True)
                        first = jnp.min(jnp.where(x == m, key_id, float(nk)), axis=0, keepdims=True)
                        sel = key_id == first
                        sv_scr[p, r, h:h + 1, col] = m
                        return jnp.where(sel, neg, x), jnp.where(sel, lax.convert_element_type(r, F32), rank)

                    _, rank = lax.fori_loop(0, TOPK, extract,
                                            (s_scr[p, h, :, col], jnp.full((nk, LANES), float(TOPK), F32)))
                    rk_scr[p, h, :, col] = rank
            return carry

        lax.fori_loop(0, nchunk, chunk, 0)

    ranked = first_level_quick()

    @pl.when(jnp.max(ranked) > float(TOPK))
    def _():
        first_level_exact()

    pairs = _second_level_pairs()

    def second_level(c, carry):
        col = pl.ds(pl.multiple_of(c * LANES, LANES), LANES)
        v1 = [sv_scr[0, a, :, col] for a in range(TOPK)]
        v2 = [sv_scr[1, b, :, col] for b in range(TOPK)]
        top = v1[0] + v2[0]

        def select(_, carry):
            cand, cnt, zsum = list(carry[0]), list(carry[1]), carry[2]
            m = _tree(jnp.maximum, cand)
            zsum = zsum + jnp.exp(m - top)
            found = jnp.zeros((nh, LANES), F32)
            for ci, (a, _b) in enumerate(pairs):
                eq = jnp.where(cand[ci] == m, 1.0, 0.0)
                hit = eq * (1.0 - found)
                found = jnp.maximum(found, eq)
                cand[ci] = jnp.where(hit > 0.0, neg, cand[ci])
                cnt[a] = cnt[a] + hit
            return tuple(cand), tuple(cnt), zsum

        def select_quick(_, carry):
            cand, cnt, zsum, left, bad = list(carry[0]), list(carry[1]), carry[2], carry[3], carry[4]
            m = _tree(jnp.maximum, cand)
            eq = [jnp.where(cv == m, 1.0, 0.0) for cv in cand]
            n_eq = _tree(jnp.add, eq)
            act = jnp.where(left > 0.0, 1.0, 0.0)
            bad = jnp.maximum(bad, jnp.where(n_eq > left, act, 0.0))
            zsum = zsum + act * n_eq * jnp.exp(m - top)
            for ci, (a, _b) in enumerate(pairs):
                cnt[a] = cnt[a] + eq[ci] * act
                cand[ci] = jnp.where(cand[ci] == m, neg, cand[ci])
            return tuple(cand), tuple(cnt), zsum, left - act * n_eq, bad

        zero = jnp.zeros((nh, LANES), F32)
        cands = tuple(v1[a] + v2[b] for a, b in pairs)

        def finish(cnt, zsum):
            for a in range(TOPK):
                cnt_scr[a, :, col] = cnt[a]
            cnt_scr[TOPK, :, col] = 1.0 / zsum

        _, cnt, zsum, _, bad = lax.fori_loop(0, TOPK, select_quick,
                                             (cands, (zero,) * TOPK, zero, zero + float(TOPK), zero))
        finish(cnt, zsum)

        @pl.when(jnp.max(bad) > 0.0)
        def _():
            _, cnt_x, zsum_x = lax.fori_loop(0, TOPK, select, (cands, (zero,) * TOPK, zero))
            finish(cnt_x, zsum_x)

        return carry

    lax.fori_loop(0, nchunk, second_level, 0)

    for h in range(nh):
        rank1 = rk_scr[0, h]
        b1 = jnp.zeros((nk, tt), F32)
        for a in range(TOPK):
            b1 = b1 + jnp.where(rank1 == float(a), cnt_scr[a, h:h + 1, :], 0.0)
        b1_ref[h] = b1
        r2_ref[h] = rk_scr[1, h].astype(BF16)
        e1_ref[h] = jnp.exp(s_scr[0, h] - sv_scr[0, 0, h:h + 1, :]) * cnt_scr[TOPK, h:h + 1, :]
        e2_ref[h] = jnp.exp(s_scr[1, h] - sv_scr[1, 0, h:h + 1, :]).astype(BF16)


def _route(h2, wq, keys, tt):
    t, d = h2.shape
    _, nh, nk, _ = keys.shape
    out = jax.ShapeDtypeStruct((nh, nk, t), F32)
    out16 = jax.ShapeDtypeStruct((nh, nk, t), BF16)
    ospec = pl.BlockSpec((nh, nk, tt), lambda i: (0, 0, i))
    return pl.pallas_call(
        _route_kernel,
        grid=(t // tt,),
        in_specs=[pl.BlockSpec((tt, d), lambda i: (i, 0)), _full(wq.shape), _full(keys.shape)],
        out_specs=[ospec, ospec, ospec, ospec],
        out_shape=[out16, out, out, out16],
        scratch_shapes=[pltpu.VMEM((2, TOPK, nh, tt), F32),
                        pltpu.VMEM((2, nh, nk, tt), F32),
                        pltpu.VMEM((2, nh, nk, tt), F32),
                        pltpu.VMEM((TOPK + 1, nh, tt), F32)],
        compiler_params=_params("arbitrary"),
        name="route",
    )(h2, wq, keys)


def _peer_kernel(n_ctx_tiles, h2_ref, u_ref, v_ref, r2_ref, e2_ref, b1_ref, e1_ref, x1_ref, mod_ref, fn_ref,
                 op_ref, os_ref, acc_scr, a_scr):
    d = x1_ref.shape[1]
    nh, nk, tt = r2_ref.shape
    n1 = b1_ref.shape[1]
    eb = pl.program_id(1)

    @pl.when(eb == 0)
    def _():
        acc_scr[...] = jnp.zeros_like(acc_scr)

    hid = lax.dot_general(u_ref[...].astype(BF16), h2_ref[...], (((1,), (1,)), ((), ())),
                          preferred_element_type=F32)
    pk = 2 * SUBLANES
    for i in range(n1):
        w = jnp.zeros((nk // pk, pk, tt), BF16)
        for h in range(nh):
            b1r = jnp.broadcast_to(b1_ref[h, i:i + 1, :], (pk, tt)).astype(BF16)[None]
            e1r = jnp.broadcast_to(e1_ref[h, i:i + 1, :], (pk, tt)).astype(BF16)[None]
            r2 = r2_ref[h].reshape(nk // pk, pk, tt)
            e2 = e2_ref[h].reshape(nk // pk, pk, tt)
            w = w + jnp.where(r2 < b1r, e1r * e2, jnp.zeros((), BF16))
        hi = hid[i * nk:(i + 1) * nk]
        act = 0.5 * hi * (1.0 + lax.erf(hi * (1.0 / math.sqrt(2.0))))
        a_scr[i * nk:(i + 1) * nk, :] = (act.astype(BF16).reshape(nk // pk, pk, tt) * w).reshape(nk, tt)
    acc_scr[...] += lax.dot_general(v_ref[...], a_scr[...], (((0,), (0,)), ((), ())),
                                    preferred_element_type=F32)

    @pl.when(eb == pl.num_programs(1) - 1)
    def _():
        gate2 = mod_ref[0][:, 5 * d:6 * d]
        xo = x1_ref[...] + gate2 * acc_scr[...].T
        res = xo * lax.rsqrt(jnp.mean(xo * xo, axis=-1, keepdims=True) + EPS) * fn_ref[...]
        is_ctx = pl.program_id(0) < n_ctx_tiles

        @pl.when(is_ctx)
        def _():
            op_ref[...] = res

        @pl.when(jnp.logical_not(is_ctx))
        def _():
            os_ref[...] = res


def _peer(h2, u, v, r2, e2, b1, e1, x1, mods, final_norm, n_ctx_tiles, tiles_per_lat, tt, n1):
    t, d = h2.shape
    ne = u.shape[0]
    nh, nk, _ = r2.shape
    eb = n1 * nk

    def mod_idx(i, e):
        return (jnp.where(i < n_ctx_tiles, 0, 1 + (i - n_ctx_tiles) // tiles_per_lat), 0, 0)

    dense = pl.BlockSpec((nh, nk, tt), lambda i, e: (0, 0, i))
    rows = pl.BlockSpec((nh, n1, tt), lambda i, e: (0, e, i))
    tc = n_ctx_tiles * tt
    return pl.pallas_call(
        functools.partial(_peer_kernel, n_ctx_tiles),
        grid=(t // tt, ne // eb),
        in_specs=[pl.BlockSpec((tt, d), lambda i, e: (i, 0)),
                  pl.BlockSpec((eb, d), lambda i, e: (e, 0)),
                  pl.BlockSpec((eb, d), lambda i, e: (e, 0)),
                  dense, dense, rows, rows,
                  pl.BlockSpec((tt, d), lambda i, e: (i, 0)),
                  pl.BlockSpec((1, 1, mods.shape[-1]), mod_idx),
                  _full(final_norm.shape)],
        out_specs=list(_two_part((tt, d), n_ctx_tiles)),
        out_shape=[jax.ShapeDtypeStruct((tc, d), F32), jax.ShapeDtypeStruct((t - tc, d), F32)],
        scratch_shapes=[pltpu.VMEM((d, tt), F32), pltpu.VMEM((eb, tt), BF16)],
        compiler_params=_params("arbitrary", "arbitrary"),
        name="peer",
    )(h2, u, v, r2, e2, b1, e1, x1, mods, final_norm)


def _hyena_features(l, bands):
    t = jnp.linspace(0.0, 1.0, l, dtype=F32)[:, None]
    wpos = 2.0 * math.pi * jnp.arange(l, dtype=F32)[:, None] / l
    f = jnp.linspace(1e-4, bands - 1, bands, dtype=F32)[None, :]
    return jnp.concatenate([t, jnp.cos(f * wpos), -jnp.sin(f * wpos)], axis=-1)


def kernel(x_prompt, x_sample, state_rwkv, c, c_ctx, w_ada, b_ada, norm_mix, norm_ffn, w_in, conv_w, w_out, rwkv_w0, rwkv_w1, rwkv_w2, rwkv_a0, rwkv_a1, rwkv_a2, rwkv_g1, rwkv_g2, rwkv_k_k, rwkv_k_a, rwkv_r_k, rwkv_ln_w, rwkv_ln_b, hy_f_w1, hy_f_b1, hy_f_w2, hy_f_b2, hy_freq, hy_f_w3, hy_bias, hy_norm, peer_wq, peer_keys, peer_u, peer_v, final_norm):
    bc, lc, d = x_prompt.shape
    bl, ll, _ = x_sample.shape
    depth = w_ada.shape[0]
    assert depth == 1
    dr = rwkv_w0.shape[-1]
    dh = hy_norm.shape[-1]
    nh = dr // HEAD
    assert dr == dh and w_in.shape[-1] == 3 * dr + 3 * dh
    tc, tl = bc * lc, bl * ll
    t = tc + tl
    tm = lc
    assert ll % tm == 0 and tm % GRID_W == 0 and tc % ll == 0
    n_ctx_tiles = tc // tm
    tiles_per_lat = ll // tm
    parts = LANES // (bl * (nh // 2))
    assert parts * bl * (nh // 2) == LANES and HEAD % parts == 0 and tc % tl == 0
    nil = HEAD // parts
    l = 0

    rows = jnp.concatenate([c_ctx[None, :], c, jnp.zeros((SUBLANES - 1 - bl, d), F32)], axis=0)
    mods = _ada(rows, w_ada[l], b_ada[l]).reshape(SUBLANES, 1, 6 * d)

    xp = x_prompt.reshape(tc, d)
    xs = x_sample.reshape(tl, d)

    def pad_cols(w):
        return jnp.pad(w, ((0, 0), (0, LANES - w.shape[1])))

    def pad_rows(w):
        return jnp.pad(w, ((0, LANES - w.shape[0]), (0, 0)))

    lora1 = jnp.concatenate([pad_cols(rwkv_w1[l, 0]), pad_cols(rwkv_w1[l, 1]), pad_cols(rwkv_a1[l, 0]),
                             pad_cols(rwkv_a1[l, 1]), pad_cols(rwkv_g1[l])], axis=1).astype(BF16)
    w2 = jnp.stack([pad_rows(rwkv_w2[l, 0]), pad_rows(rwkv_w2[l, 1])])
    a2 = jnp.stack([pad_rows(rwkv_a2[l, 0]), pad_rows(rwkv_a2[l, 1])])
    vecs_pre = jnp.stack([rwkv_w0[l, 0], rwkv_w0[l, 1], rwkv_a0[l, 0], rwkv_a0[l, 1], rwkv_k_k[l], rwkv_k_a[l],
                          rwkv_r_k[l, 0].reshape(dr), rwkv_r_k[l, 1].reshape(dr)])
    head_id = jnp.arange(dr, dtype=jnp.int32) // HEAD
    hs = (head_id[:, None] == head_id[None, :]).astype(BF16)
    shared, perdir, gb, hy = _mix_pre(xp, xs, mods, n_ctx_tiles, tiles_per_lat, tm, norm_mix[l][None, :],
                                      w_in[l].astype(BF16), conv_w[l], lora1, w2, a2, rwkv_g2[l], vecs_pre, hs)

    shared4 = shared.reshape(3, t // lc, lc, dr)
    perdir4 = perdir.reshape(2, 3, t // lc, lc, dr)
    y_cf, s_cf = _wkv_ctx(shared4, perdir4, bc, 8, False)
    y_cb, s_cb = _wkv_ctx(shared4, perdir4, bc, 8, True)
    shared_l = shared.reshape(3, t // ll, ll, dr)
    perdir_l = perdir.reshape(2, 3, t // ll, ll, dr)
    s0 = state_rwkv[:, l].reshape(bl, 2, nh // 2, 2, nil, parts, HEAD).transpose(1, 3, 4, 6, 2, 0, 5)
    s0 = s0.reshape(2, 2 * nil, HEAD, LANES)
    y_lf, y_lb = _wkv_lat(shared_l, perdir_l, s0, tc // tl, bl, 16)
    new_state = jnp.stack([s_cf, s_cb]).reshape(2, 2, HEAD, HEAD, nh // 2, bc).transpose(5, 0, 4, 1, 2, 3)
    new_state = new_state.reshape(bc, 1, 2, nh, HEAD, HEAD)

    bands = (hy_f_w1.shape[1] - 1) // 2
    femb = hy_f_w1.shape[1]
    fpad = 32
    w1p = jnp.pad(hy_f_w1[l], ((0, fpad - femb), (0, 0)))
    hidden = hy_f_w2.shape[-1]
    w3 = hy_f_w3[l].reshape(hidden, 4, dh).transpose(1, 0, 2)
    max_decay = math.log(1e-2) / 0.3
    min_decay = math.log(1e-2) / 1.5
    deltas = jnp.linspace(min_decay, max_decay, dh, dtype=F32)[None, :]
    z_parts = []
    for seq_len, first_seq, n_seq in ((lc, 0, bc), (ll, tc // ll, bl)):
        mats = _dft_mats(seq_len)
        zf = jnp.pad(_hyena_features(seq_len, bands), ((0, 0), (0, fpad - femb)))
        g = _hy_filter(zf, w1p, hy_f_b1[l][None, :], hy_f_w2[l], hy_f_b2[l][None, :], hy_freq[l], w3,
                       deltas, hy_bias[l], mats)
        z_parts.append(_hy_conv(hy, g, mats, seq_len, first_seq, n_seq))

    vecs_post = jnp.concatenate([jnp.stack([rwkv_ln_w[l], rwkv_ln_b[l], hy_norm[l]]),
                                 jnp.zeros((SUBLANES - 3, dr), F32)], axis=0)
    x1, h2 = _mix_post((y_cf.reshape(tc, dr), y_lf.reshape(tl, dr)), (y_cb.reshape(tc, dr), y_lb.reshape(tl, dr)),
                       gb, z_parts, (xp, xs), mods, n_ctx_tiles, tiles_per_lat, tm, vecs_post,
                       w_out[l].astype(BF16), norm_ffn[l][None, :], hs)

    r2, b1, e1, e2 = _route(h2, peer_wq[l].astype(BF16), peer_keys[l], tt=512)
    tt = 512
    assert ll % tt == 0 and tc % tt == 0
    y_p, y_s = _peer(h2, peer_u[l], peer_v[l].astype(BF16), r2, e2, b1, e1, x1, mods,
                final_norm[None, :], tc // tt, ll // tt, tt, n1=SUBLANES)
    return y_p.reshape(bc, lc, d), y_s.reshape(bl, ll, d), new_state
```
